```python
import functools
import jax, jax.numpy as jnp
from jax import lax
import numpy as np

D_MODEL = 2048
BATCH = 4
SEQ = 2048
DEPTH = 2
DEC_BATCH = 128
DEC_SEQ = 8
PAST_LEN = 16384
PAGE_SIZE = 128

N_EVEN = (DEPTH + 1) // 2
N_ODD = DEPTH // 2
POOL_WINDOWS = (2, 4, 8, 16)
N_POOL_GROUPS = len(POOL_WINDOWS)
D_POOL = D_MODEL // 2
POOL_GROUP = D_POOL // N_POOL_GROUPS
POOL_BUF = max(POOL_WINDOWS) - 1
N_HEADS = 8
QK_NOPE = 128
QK_ROPE = 64
V_HEAD = 128
Q_LORA = 512
KV_LORA = 512
ROPE_THETA = 10000.0
ATTN_SCALE = (QK_NOPE + QK_ROPE) ** -0.5
Q_BLOCK = 128
D_MLA_OUT = N_HEADS * V_HEAD
D_IN_EVEN = D_POOL + Q_LORA + KV_LORA + QK_ROPE
D_MIX_EVEN = D_POOL + D_MLA_OUT
CONV_W = 3
CONV_BUF = CONV_W - 1
D_FF = ((8 * D_MODEL + 3 * 256 - 1) // (3 * 256)) * 256
DEEPNORM_ALPHA = (2 * DEPTH) ** 0.25
DEEPNORM_BETA = (8 * DEPTH) ** -0.25

kernel_name = "hybrid_pool_mla_shortconv_decoder_step"


def rms_norm(x, g, eps=1e-6):
    xf = x.astype(jnp.float32)
    y = xf * lax.rsqrt(jnp.mean(xf * xf, axis=-1, keepdims=True) + eps)
    return (y * g.astype(jnp.float32)).astype(x.dtype)


def layer_norm(x, g, b, eps=1e-5):
    xf = x.astype(jnp.float32)
    mu = jnp.mean(xf, axis=-1, keepdims=True)
    xc = xf - mu
    var = jnp.mean(xc * xc, axis=-1, keepdims=True)
    return (xc * lax.rsqrt(var + eps) * g.astype(jnp.float32) + b.astype(jnp.float32)).astype(x.dtype)


def rope_tables(pos):
    inv = ROPE_THETA ** (-jnp.arange(0, QK_ROPE, 2, dtype=jnp.float32) / QK_ROPE)
    ang = pos.astype(jnp.float32)[:, None] * inv[None, :]
    return jnp.cos(ang), jnp.sin(ang)


def apply_rope(x, cos, sin):
    half = QK_ROPE // 2
    xf = x.astype(jnp.float32)
    x1, x2 = xf[..., :half], xf[..., half:]
    return jnp.concatenate([x1 * cos - x2 * sin, x1 * sin + x2 * cos], axis=-1).astype(x.dtype)


def pool_mixer(u, buf, pos, pool_w, pool_scale):
    n, L, _ = u.shape
    xp = jnp.concatenate([buf.astype(u.dtype), u], axis=1)
    cs = jnp.cumsum(xp.astype(jnp.float32), axis=1)
    cs = jnp.concatenate([jnp.zeros_like(cs[:, :1]), cs], axis=1)
    end = cs[:, POOL_BUF + 1:]
    means = []
    for g, w in enumerate(POOL_WINDOWS):
        sl = slice(g * POOL_GROUP, (g + 1) * POOL_GROUP)
        win = end[..., sl] - cs[:, POOL_BUF + 1 - w:POOL_BUF + 1 - w + L, sl]
        cnt = jnp.minimum(pos + 1, w).astype(jnp.float32)[None, :, None]
        means.append(win / cnt)
    mean = jnp.stack(means, axis=2)
    diff = (mean - u.reshape(n, L, N_POOL_GROUPS, POOL_GROUP).astype(jnp.float32)).astype(u.dtype)
    y = jnp.einsum("nlgc,gcd->nlgd", diff, pool_w).reshape(n, L, D_POOL) * pool_scale
    return y, xp[:, -POOL_BUF:]


def mla_project(hq, hkv, hpe, pos, q_norm_g, w_q_b, kv_norm_g, w_uk):
    cq = rms_norm(hq, q_norm_g)
    q = jnp.einsum("nlr,rhd->nlhd", cq, w_q_b)
    q_nope, q_pe = q[..., :QK_NOPE], q[..., QK_NOPE:]
    cos, sin = rope_tables(pos)
    q_pe = apply_rope(q_pe, cos[:, None, :], sin[:, None, :])
    q_lat = jnp.einsum("nlhd,rhd->nlhr", q_nope, w_uk)
    ckv = rms_norm(hkv, kv_norm_g)
    kpe = apply_rope(hpe, cos, sin)
    return q_lat, q_pe, ckv, kpe


def mla_attend(q_lat, q_pe, ckv, kpe, q_pos, k_pos):
    s = (jnp.einsum("nqhr,nkr->nhqk", q_lat, ckv).astype(jnp.float32)
         + jnp.einsum("nqhp,nkp->nhqk", q_pe, kpe).astype(jnp.float32)) * ATTN_SCALE
    mask = k_pos[None, :] <= q_pos[:, None]
    s = jnp.where(mask, s, jnp.finfo(jnp.float32).min)
    p = jax.nn.softmax(s, axis=-1).astype(ckv.dtype)
    return jnp.einsum("nhqk,nkr->nqhr", p, ckv)


def prompt_attend(e, q_lat, q_pe, ckv, kpe):
    L = q_lat.shape[1]
    pos = jnp.arange(L)
    outs = []
    for i in range(L // Q_BLOCK):
        qs = slice(i * Q_BLOCK, (i + 1) * Q_BLOCK)
        kend = (i + 1) * Q_BLOCK
        outs.append(mla_attend(q_lat[:, qs], q_pe[:, qs], ckv[:, :kend], kpe[:, :kend], pos[qs], pos[:kend]))
    return jnp.concatenate(outs, axis=1)


def paged_attend(e, q_lat, q_pe, ckv, kpe, cache_ckv, cache_kpe, page_table):
    n_pages = page_table.shape[1]
    past = n_pages * PAGE_SIZE
    T = q_lat.shape[1]
    q_pos = PAST_LEN + jnp.arange(T)
    k_pos = jnp.concatenate([jnp.arange(past), q_pos])

    def one(args):
        ql, qp, cn, kn, pt = args
        ck = cache_ckv[e, pt].reshape(past, KV_LORA).astype(cn.dtype)
        kp = cache_kpe[e, pt].reshape(past, QK_ROPE).astype(kn.dtype)
        ck = jnp.concatenate([ck, cn], axis=0)
        kp = jnp.concatenate([kp, kn], axis=0)
        return mla_attend(ql[None], qp[None], ck[None], kp[None], q_pos, k_pos)[0]

    return lax.map(one, (q_lat, q_pe, ckv, kpe, page_table))


def conv_mixer(x, buf, w_in, conv_w, w_out):
    L = x.shape[1]
    h = x @ w_in
    hv, gb, gc = h[..., :D_MODEL], h[..., D_MODEL:2 * D_MODEL], h[..., 2 * D_MODEL:]
    z = gc * hv
    zp = jnp.concatenate([buf.astype(z.dtype), z], axis=1)
    conv = zp[:, 0:L] * conv_w[0]
    for j in range(1, CONV_W):
        conv = conv + zp[:, j:j + L] * conv_w[j]
    return (gb * conv) @ w_out, zp[:, -CONV_BUF:]


def swiglu(x, wg, wu, wd):
    return (jax.nn.silu(x @ wg) * (x @ wu)) @ wd


def trunk(x, pos, pool_bufs, conv_bufs, attend, params):
    (w_in_even, pool_w, pool_scale, q_norm_g, w_q_b, kv_norm_g, w_uk, w_uv, w_out_even,
     w_in_odd, conv_w, w_out_odd, ffn_w_gate, ffn_w_up, ffn_w_down, ln1_g, ln1_b, ln2_g, ln2_b) = params
    ckvs, kpes, pools, convs = [], [], [], []
    for layer in range(DEPTH):
        if layer % 2 == 0:
            e = layer // 2
            h = x @ w_in_even[e]
            u = h[..., :D_POOL]
            hq = h[..., D_POOL:D_POOL + Q_LORA]
            hkv = h[..., D_POOL + Q_LORA:D_POOL + Q_LORA + KV_LORA]
            hpe = h[..., D_POOL + Q_LORA + KV_LORA:]
            y_pool, nb = pool_mixer(u, pool_bufs[e], pos, pool_w[e], pool_scale[e])
            q_lat, q_pe, ckv, kpe = mla_project(hq, hkv, hpe, pos, q_norm_g[e], w_q_b[e], kv_norm_g[e], w_uk[e])
            o_lat = attend(e, q_lat, q_pe, ckv, kpe)
            o = jnp.einsum("nlhr,rhv->nlhv", o_lat, w_uv[e]).reshape(o_lat.shape[:2] + (D_MLA_OUT,))
            mix = jnp.concatenate([y_pool, o], axis=-1) @ w_out_even[e]
            ckvs.append(ckv)
            kpes.append(kpe)
            pools.append(nb)
        else:
            od = layer // 2
            mix, nb = conv_mixer(x, conv_bufs[od], w_in_odd[od], conv_w[od], w_out_odd[od])
            convs.append(nb)
        x = layer_norm(DEEPNORM_ALPHA * x + mix, ln1_g[layer], ln1_b[layer])
        x = layer_norm(DEEPNORM_ALPHA * x + swiglu(x, ffn_w_gate[layer], ffn_w_up[layer], ffn_w_down[layer]),
                       ln2_g[layer], ln2_b[layer])
    return x, jnp.stack(ckvs), jnp.stack(kpes), jnp.stack(pools), jnp.stack(convs)


def setup_inputs(seed: int = 0) -> dict:
    key = jax.random.key(seed)
    ks = jax.random.split(key, 32)
    f32 = jnp.float32
    n_pages = PAST_LEN // PAGE_SIZE
    n_used = DEC_BATCH * n_pages
    n_pool = n_used + n_used // 4

    def nrm(k, shape, scale=1.0):
        return jax.random.normal(k, shape, f32) * scale

    def gain(k, shape):
        return 1.0 + 0.02 * jax.random.normal(k, shape, f32)

    beta = DEEPNORM_BETA
    return {
        "x_prompt": nrm(ks[0], (BATCH, SEQ, D_MODEL)),
        "x_sample": nrm(ks[1], (DEC_BATCH, DEC_SEQ, D_MODEL)),
        "cache_ckv": nrm(ks[2], (N_EVEN, n_pool, PAGE_SIZE, KV_LORA)),
        "cache_kpe": nrm(ks[3], (N_EVEN, n_pool, PAGE_SIZE, QK_ROPE)),
        "state_pool": nrm(ks[4], (N_EVEN, DEC_BATCH, POOL_BUF, D_POOL)),
        "state_conv": nrm(ks[5], (N_ODD, DEC_BATCH, CONV_BUF, D_MODEL)),
        "page_table": jax.random.permutation(ks[6], n_pool)[:n_used].reshape(DEC_BATCH, n_pages).astype(jnp.int32),
        "w_in_even": nrm(ks[7], (N_EVEN, D_MODEL, D_IN_EVEN), D_MODEL ** -0.5),
        "pool_w": nrm(ks[8], (N_EVEN, N_POOL_GROUPS, POOL_GROUP, POOL_GROUP), POOL_GROUP ** -0.5),
        "pool_scale": gain(ks[9], (N_EVEN, D_POOL)),
        "q_norm_g": gain(ks[10], (N_EVEN, Q_LORA)),
        "w_q_b": nrm(ks[11], (N_EVEN, Q_LORA, N_HEADS, QK_NOPE + QK_ROPE), Q_LORA ** -0.5),
        "kv_norm_g": gain(ks[12], (N_EVEN, KV_LORA)),
        "w_uk": nrm(ks[13], (N_EVEN, KV_LORA, N_HEADS, QK_NOPE), KV_LORA ** -0.5),
        "w_uv": nrm(ks[14], (N_EVEN, KV_LORA, N_HEADS, V_HEAD), beta * KV_LORA ** -0.5),
        "w_out_even": nrm(ks[15], (N_EVEN, D_MIX_EVEN, D_MODEL), beta * D_MIX_EVEN ** -0.5),
        "w_in_odd": nrm(ks[16], (N_ODD, D_MODEL, 3 * D_MODEL), D_MODEL ** -0.5),
        "conv_w": nrm(ks[17], (N_ODD, CONV_W, D_MODEL), CONV_W ** -0.5),
        "w_out_odd": nrm(ks[18], (N_ODD, D_MODEL, D_MODEL), beta * D_MODEL ** -0.5),
        "ffn_w_gate": nrm(ks[19], (DEPTH, D_MODEL, D_FF), D_MODEL ** -0.5),
        "ffn_w_up": nrm(ks[20], (DEPTH, D_MODEL, D_FF), D_MODEL ** -0.5),
        "ffn_w_down": nrm(ks[21], (DEPTH, D_FF, D_MODEL), beta * D_FF ** -0.5),
        "ln1_g": gain(ks[22], (DEPTH, D_MODEL)),
        "ln1_b": nrm(ks[23], (DEPTH, D_MODEL), 0.02),
        "ln2_g": gain(ks[24], (DEPTH, D_MODEL)),
        "ln2_b": nrm(ks[25], (DEPTH, D_MODEL), 0.02),
    }


def reference(x_prompt, x_sample, cache_ckv, cache_kpe, state_pool, state_conv, page_table,
              w_in_even, pool_w, pool_scale, q_norm_g, w_q_b, kv_norm_g, w_uk, w_uv, w_out_even,
              w_in_odd, conv_w, w_out_odd, ffn_w_gate, ffn_w_up, ffn_w_down,
              ln1_g, ln1_b, ln2_g, ln2_b):
    params = (w_in_even, pool_w, pool_scale, q_norm_g, w_q_b, kv_norm_g, w_uk, w_uv, w_out_even,
              w_in_odd, conv_w, w_out_odd, ffn_w_gate, ffn_w_up, ffn_w_down, ln1_g, ln1_b, ln2_g, ln2_b)
    n_p, L_p, _ = x_prompt.shape
    n_s, L_s, _ = x_sample.shape

    pos_p = jnp.arange(L_p)
    pool0 = jnp.zeros((N_EVEN, n_p, POOL_BUF, D_POOL), x_prompt.dtype)
    conv0 = jnp.zeros((N_ODD, n_p, CONV_BUF, D_MODEL), x_prompt.dtype)
    y_prompt, ckv_prompt, kpe_prompt, pool_prompt, conv_prompt = trunk(
        x_prompt, pos_p, pool0, conv0, prompt_attend, params)

    pos_s = PAST_LEN + jnp.arange(L_s)
    s_attend = functools.partial(paged_attend, cache_ckv=cache_ckv, cache_kpe=cache_kpe, page_table=page_table)
    y_sample, ckv_sample, kpe_sample, pool_sample, conv_sample = trunk(
        x_sample, pos_s, state_pool, state_conv, s_attend, params)

    return (y_prompt, y_sample, ckv_prompt, kpe_prompt, pool_prompt, conv_prompt,
            ckv_sample, kpe_sample, pool_sample, conv_sample)
```

```python
import functools

import jax
import jax.numpy as jnp
from jax import lax
from jax.experimental import pallas as pl
from jax.experimental.pallas import tpu as pltpu

F32 = jnp.float32
BF16 = jnp.bfloat16

D_MODEL = 2048
BATCH = 4
SEQ = 2048
DEPTH = 2
DEC_BATCH = 128
DEC_SEQ = 8
PAST_LEN = 16384
PAGE_SIZE = 128
N_PAGES = PAST_LEN // PAGE_SIZE
POOL_WINDOWS = (2, 4, 8, 16)
N_POOL_GROUPS = len(POOL_WINDOWS)
D_POOL = D_MODEL // 2
POOL_GROUP = D_POOL // N_POOL_GROUPS
POOL_BUF = max(POOL_WINDOWS) - 1
N_HEADS = 8
QK_NOPE = 128
QK_ROPE = 64
V_HEAD = 128
Q_LORA = 512
KV_LORA = 512
ROPE_THETA = 10000.0
ATTN_SCALE = (QK_NOPE + QK_ROPE) ** -0.5
Q_BLOCK = 128
D_MLA_OUT = N_HEADS * V_HEAD
CONV_W = 3
CONV_BUF = CONV_W - 1
D_FF = ((8 * D_MODEL + 3 * 256 - 1) // (3 * 256)) * 256
DEEPNORM_ALPHA = (2 * DEPTH) ** 0.25

N_PROMPT = BATCH * SEQ
N_SAMPLE = DEC_BATCH * DEC_SEQ
N_TOK = N_PROMPT + N_SAMPLE
D_QK = KV_LORA + QK_ROPE
POOL_HIST = 16
CONV_HIST = 8
VMEM_LIMIT_MB = 56


def _cparams(dims, vmem_mb=VMEM_LIMIT_MB):
    return pltpu.CompilerParams(dimension_semantics=dims, vmem_limit_bytes=vmem_mb * 1024 * 1024)


def _dot(a, b):
    return jnp.dot(a, b, preferred_element_type=F32)


def _dot_nt(a, b):
    return lax.dot_general(a, b, (((1,), (1,)), ((), ())), preferred_element_type=F32)


def _rms(h, g, eps=1e-6):
    return h * lax.rsqrt(jnp.mean(h * h, axis=-1, keepdims=True) + eps) * g


def _layer_norm(v, g, b, eps=1e-5):
    mu = jnp.mean(v, axis=-1, keepdims=True)
    vc = v - mu
    var = jnp.mean(vc * vc, axis=-1, keepdims=True)
    return vc * lax.rsqrt(var + eps) * g + b


def _in_even_kernel(x_ref, wu_ref, wq_ref, wkv_ref, wpe_ref, wper_ref, qg_ref, kvg_ref, cos_ref, sin_ref,
                    u_ref, cq_ref, ckv_ref, kpe_ref, kvb_ref):
    xb = x_ref[...].astype(BF16)
    u_ref[...] = _dot(xb, wu_ref[...])
    cq_ref[...] = _rms(_dot(xb, wq_ref[...]), qg_ref[...]).astype(BF16)
    ckv = _rms(_dot(xb, wkv_ref[...]), kvg_ref[...])
    ckv_ref[...] = ckv
    kpe = _dot(xb, wpe_ref[...]) * cos_ref[:, :QK_ROPE] + _dot(xb, wper_ref[...]) * sin_ref[:, :QK_ROPE]
    kpe_ref[...] = kpe
    kvb_ref[:, :KV_LORA] = ckv.astype(BF16)
    kvb_ref[:, KV_LORA:] = kpe.astype(BF16)


def _in_even(x, wu, wq, wkv, wpe, wper, qg, kvg, cos, sin, tm=512):
    n = x.shape[0]
    row = lambda i: (i, 0)
    const = lambda i: (0, 0)
    return pl.pallas_call(
        _in_even_kernel,
        grid=(n // tm,),
        in_specs=[
            pl.BlockSpec((tm, D_MODEL), row),
            pl.BlockSpec((D_MODEL, D_POOL), const),
            pl.BlockSpec((D_MODEL, Q_LORA), const),
            pl.BlockSpec((D_MODEL, KV_LORA), const),
            pl.BlockSpec((D_MODEL, QK_ROPE), const),
            pl.BlockSpec((D_MODEL, QK_ROPE), const),
            pl.BlockSpec((1, Q_LORA), const),
            pl.BlockSpec((1, KV_LORA), const),
            pl.BlockSpec((tm, 2 * QK_ROPE), row),
            pl.BlockSpec((tm, 2 * QK_ROPE), row),
        ],
        out_specs=[
            pl.BlockSpec((tm, D_POOL), row),
            pl.BlockSpec((tm, Q_LORA), row),
            pl.BlockSpec((tm, KV_LORA), row),
            pl.BlockSpec((tm, QK_ROPE), row),
            pl.BlockSpec((tm, D_QK), row),
        ],
        out_shape=[
            jax.ShapeDtypeStruct((n, D_POOL), F32),
            jax.ShapeDtypeStruct((n, Q_LORA), BF16),
            jax.ShapeDtypeStruct((n, KV_LORA), F32),
            jax.ShapeDtypeStruct((n, QK_ROPE), F32),
            jax.ShapeDtypeStruct((n, D_QK), BF16),
        ],
        compiler_params=_cparams(("parallel",)),
        name="in_even",
    )(x, wu, wq, wkv, wpe, wper, qg, kvg, cos, sin)


def _q_proj_kernel(cq_ref, wqn_ref, wqr_ref, wqrr_ref, wukt_ref, cos_ref, sin_ref, q_ref, *, nsub):
    cq = cq_ref[...]
    qn = _dot(cq, wqn_ref[...])
    cos = jnp.concatenate([cos_ref[...]] * (N_HEADS // 2), axis=1)
    sin = jnp.concatenate([sin_ref[...]] * (N_HEADS // 2), axis=1)
    qpe = (_dot(cq, wqr_ref[...]) * cos + _dot(cq, wqrr_ref[...]) * sin) * ATTN_SCALE
    for h in range(N_HEADS):
        ql = _dot(qn[:, h * QK_NOPE:(h + 1) * QK_NOPE].astype(BF16), wukt_ref[h]) * ATTN_SCALE
        for j in range(nsub):
            rows = slice(j * Q_BLOCK, (j + 1) * Q_BLOCK)
            q_ref[j, h, :, :KV_LORA] = ql[rows]
            q_ref[j, h, :, KV_LORA:] = qpe[rows, h * QK_ROPE:(h + 1) * QK_ROPE]


def _q_proj(cq, wqn, wqr, wqrr, wukt, cos, sin, tm=256):
    n = cq.shape[0]
    nsub = tm // Q_BLOCK
    row = lambda i: (i, 0)
    const = lambda i: (0, 0)
    return pl.pallas_call(
        functools.partial(_q_proj_kernel, nsub=nsub),
        grid=(n // tm,),
        in_specs=[
            pl.BlockSpec((tm, Q_LORA), row),
            pl.BlockSpec((Q_LORA, N_HEADS * QK_NOPE), const),
            pl.BlockSpec((Q_LORA, N_HEADS * QK_ROPE), const),
            pl.BlockSpec((Q_LORA, N_HEADS * QK_ROPE), const),
            pl.BlockSpec((N_HEADS, QK_NOPE, KV_LORA), lambda i: (0, 0, 0)),
            pl.BlockSpec((tm, 2 * QK_ROPE), row),
            pl.BlockSpec((tm, 2 * QK_ROPE), row),
        ],
        out_specs=pl.BlockSpec((nsub, N_HEADS, Q_BLOCK, D_QK), lambda i: (i, 0, 0, 0)),
        out_shape=jax.ShapeDtypeStruct((n // Q_BLOCK, N_HEADS, Q_BLOCK, D_QK), F32),
        compiler_params=_cparams(("parallel",)),
        name="q_proj",
    )(cq, wqn, wqr, wqrr, wukt, cos, sin)


def _pool_groups(read, pos, w_ref, scale_ref, write):
    for g, w in enumerate(POOL_WINDOWS):
        lanes = slice(g * POOL_GROUP, (g + 1) * POOL_GROUP)
        cur = read(0, lanes)
        win = cur
        for j in range(1, w):
            win = win + read(j, lanes)
        cnt = jnp.minimum(pos + 1, w).astype(F32)
        diff = (win / cnt - cur).astype(BF16)
        write(lanes, _dot(diff, w_ref[g]) * scale_ref[:, lanes])


def _pool_prompt_kernel(u_ref, w_ref, scale_ref, y_ref, xp_ref, *, T):
    t = pl.program_id(1)

    @pl.when(t == 0)
    def _():
        xp_ref[0:POOL_HIST, :] = jnp.zeros((POOL_HIST, D_POOL), F32)

    xp_ref[POOL_HIST:POOL_HIST + T, :] = u_ref[...]
    pos = t * T + lax.broadcasted_iota(jnp.int32, (T, 1), 0)

    def read(j, lanes):
        return xp_ref[POOL_HIST - j:POOL_HIST - j + T, lanes]

    def write(lanes, y):
        y_ref[:, lanes] = y.astype(y_ref.dtype)

    _pool_groups(read, pos, w_ref, scale_ref, write)
    xp_ref[0:POOL_HIST, :] = xp_ref[T:T + POOL_HIST, :]


def _pool_prompt(u, pool_w, pool_scale, T=512):
    nt = SEQ // T
    return pl.pallas_call(
        functools.partial(_pool_prompt_kernel, T=T),
        grid=(BATCH, nt),
        in_specs=[
            pl.BlockSpec((T, D_POOL), lambda b, t: (b * nt + t, 0)),
            pl.BlockSpec((N_POOL_GROUPS, POOL_GROUP, POOL_GROUP), lambda b, t: (0, 0, 0)),
            pl.BlockSpec((1, D_POOL), lambda b, t: (0, 0)),
        ],
        out_specs=pl.BlockSpec((T, D_POOL), lambda b, t: (b * nt + t, 0)),
        out_shape=jax.ShapeDtypeStruct((N_PROMPT, D_POOL), BF16),
        scratch_shapes=[pltpu.VMEM((T + POOL_HIST, D_POOL), F32)],
        compiler_params=_cparams(("parallel", "arbitrary")),
        name="pool_prompt",
    )(u, pool_w, pool_scale)


def _pool_sample_kernel(u_ref, hist_ref, w_ref, scale_ref, y_ref, xp_ref, *, S):
    xp_ref[:, 0:POOL_HIST, :] = hist_ref[...]
    xp_ref[:, POOL_HIST:, :] = u_ref[...]
    pos = PAST_LEN + lax.broadcasted_iota(jnp.int32, (S, DEC_SEQ, 1), 1)
    pos = pos.reshape(S * DEC_SEQ, 1)

    def read(j, lanes):
        return xp_ref[:, POOL_HIST - j:POOL_HIST - j + DEC_SEQ, lanes].reshape(S * DEC_SEQ, POOL_GROUP)

    def write(lanes, y):
        y_ref[:, lanes] = y.astype(y_ref.dtype)

    _pool_groups(read, pos, w_ref, scale_ref, write)


def _pool_sample(u3, hist, pool_w, pool_scale, S=32):
    return pl.pallas_call(
        functools.partial(_pool_sample_kernel, S=S),
        grid=(DEC_BATCH // S,),
        in_specs=[
            pl.BlockSpec((S, DEC_SEQ, D_POOL), lambda i: (i, 0, 0)),
            pl.BlockSpec((S, POOL_HIST, D_POOL), lambda i: (i, 0, 0)),
            pl.BlockSpec((N_POOL_GROUPS, POOL_GROUP, POOL_GROUP), lambda i: (0, 0, 0)),
            pl.BlockSpec((1, D_POOL), lambda i: (0, 0)),
        ],
        out_specs=pl.BlockSpec((S * DEC_SEQ, D_POOL), lambda i: (i, 0)),
        out_shape=jax.ShapeDtypeStruct((N_SAMPLE, D_POOL), BF16),
        scratch_shapes=[pltpu.VMEM((S, POOL_HIST + DEC_SEQ, D_POOL), F32)],
        compiler_params=_cparams(("parallel",)),
        name="pool_sample",
    )(u3, hist, pool_w, pool_scale)


KV_CHUNK = 256


def _prompt_attn_kernel(q_ref, kv_ref, o_ref, m_ref, l_ref, acc_ref):
    i = pl.program_id(1)
    rows = N_HEADS * Q_BLOCK
    q = q_ref[0].reshape(rows, D_QK).astype(BF16)
    m_ref[...] = jnp.full((rows, 1), -jnp.inf, F32)
    l_ref[...] = jnp.zeros((rows, 1), F32)
    acc_ref[...] = jnp.zeros((rows, KV_LORA), F32)

    def step(j, masked):
        kv = kv_ref[pl.ds(pl.multiple_of(j * KV_CHUNK, KV_CHUNK), KV_CHUNK), :]
        s = _dot_nt(q, kv)
        if masked:
            q_pos = i * Q_BLOCK + (lax.broadcasted_iota(jnp.int32, (rows, KV_CHUNK), 0) & (Q_BLOCK - 1))
            k_pos = j * KV_CHUNK + lax.broadcasted_iota(jnp.int32, (rows, KV_CHUNK), 1)
            s = jnp.where(k_pos <= q_pos, s, -jnp.inf)
        m_old = m_ref[...]
        m_new = jnp.maximum(m_old, jnp.max(s, axis=-1, keepdims=True))
        alpha = jnp.exp(m_old - m_new)
        p = jnp.exp(s - m_new)
        l_ref[...] = alpha * l_ref[...] + jnp.sum(p, axis=-1, keepdims=True)
        acc_ref[...] = alpha * acc_ref[...] + _dot(p.astype(BF16), kv[:, :KV_LORA])
        m_ref[...] = m_new

    def body(j, c):
        step(j, False)
        return c

    n_chunks = (i * Q_BLOCK + Q_BLOCK + KV_CHUNK - 1) // KV_CHUNK
    lax.fori_loop(0, n_chunks - 1, body, 0)
    step(n_chunks - 1, True)
    o = acc_ref[...] / l_ref[...]
    o_ref[0] = o.reshape(N_HEADS, Q_BLOCK, KV_LORA).astype(o_ref.dtype)


def _prompt_attn(q4, kvb):
    nqb = SEQ // Q_BLOCK
    rows = N_HEADS * Q_BLOCK
    return pl.pallas_call(
        _prompt_attn_kernel,
        grid=(BATCH, nqb),
        in_specs=[
            pl.BlockSpec((1, N_HEADS, Q_BLOCK, D_QK), lambda b, i: (b * nqb + i, 0, 0, 0)),
            pl.BlockSpec((SEQ, D_QK), lambda b, i: (b, 0)),
        ],
        out_specs=pl.BlockSpec((1, N_HEADS, Q_BLOCK, KV_LORA), lambda b, i: (b * nqb + i, 0, 0, 0)),
        out_shape=jax.ShapeDtypeStruct((BATCH * nqb, N_HEADS, Q_BLOCK, KV_LORA), BF16),
        scratch_shapes=[
            pltpu.VMEM((rows, 1), F32),
            pltpu.VMEM((rows, 1), F32),
            pltpu.VMEM((rows, KV_LORA), F32),
        ],
        compiler_params=_cparams(("parallel", "arbitrary")),
        name="prompt_attn",
    )(q4, kvb)


PAGES_PER_CHUNK = 16
CHUNK_KEYS = PAGES_PER_CHUNK * PAGE_SIZE
CHUNKS_PER_SEQ = N_PAGES // PAGES_PER_CHUNK
NEW_KEYS_PAD = 16


def _decode_attn_kernel(pt_ref, q_ref, ckvn_ref, kpen_ref, cckv_hbm, ckpe_hbm, o_ref, ckv_buf, kpe_buf, sem):
    s = pl.program_id(0)
    nseq = pl.num_programs(0)
    rows = N_HEADS * DEC_SEQ

    def page_copies(seq, c, slot, p):
        page = pt_ref[seq, c * PAGES_PER_CHUNK + p]
        dst = pl.ds(p * PAGE_SIZE, PAGE_SIZE)
        return (pltpu.make_async_copy(cckv_hbm.at[page], ckv_buf.at[slot, dst, :], sem.at[0, slot]),
                pltpu.make_async_copy(ckpe_hbm.at[page], kpe_buf.at[slot, dst, :], sem.at[1, slot]))

    def start_chunk(seq, c, slot):
        for p in range(PAGES_PER_CHUNK):
            for cp in page_copies(seq, c, slot, p):
                cp.start()

    def wait_chunk(seq, c, slot):
        for p in range(PAGES_PER_CHUNK):
            for cp in page_copies(seq, c, slot, p):
                cp.wait()

    @pl.when(s == 0)
    def _():
        start_chunk(0, 0, 0)

    q = q_ref[0].reshape(rows, D_QK)
    q_lat = q[:, :KV_LORA].astype(BF16)
    q_pe = q[:, KV_LORA:].astype(BF16)

    def update(carry, sc, v):
        m_old, l_old, acc = carry
        m_new = jnp.maximum(m_old, jnp.max(sc, axis=-1, keepdims=True))
        alpha = jnp.exp(m_old - m_new)
        p = jnp.exp(sc - m_new)
        l_new = alpha * l_old + jnp.sum(p, axis=-1, keepdims=True)
        return m_new, l_new, alpha * acc + _dot(p.astype(BF16), v)

    def body(c, carry):
        slot = c % 2

        @pl.when(c + 1 < CHUNKS_PER_SEQ)
        def _():
            start_chunk(s, c + 1, 1 - slot)

        @pl.when(jnp.logical_and(c + 1 == CHUNKS_PER_SEQ, s + 1 < nseq))
        def _():
            start_chunk(s + 1, 0, 1 - slot)

        wait_chunk(s, c, slot)
        kvc = ckv_buf[slot].astype(BF16)
        kpc = kpe_buf[slot].astype(BF16)
        sc = _dot_nt(q_lat, kvc) + _dot_nt(q_pe, kpc)
        return update(carry, sc, kvc)

    init = (jnp.full((rows, 1), -jnp.inf, F32), jnp.zeros((rows, 1), F32), jnp.zeros((rows, KV_LORA), F32))
    carry = lax.fori_loop(0, CHUNKS_PER_SEQ, body, init)

    pad = NEW_KEYS_PAD - DEC_SEQ
    kvn = jnp.concatenate([ckvn_ref[...], jnp.zeros((pad, KV_LORA), F32)], axis=0).astype(BF16)
    kpn = jnp.concatenate([kpen_ref[...], jnp.zeros((pad, QK_ROPE), F32)], axis=0).astype(BF16)
    sc = _dot_nt(q_lat, kvn) + _dot_nt(q_pe, kpn)
    tok = lax.broadcasted_iota(jnp.int32, (rows, NEW_KEYS_PAD), 0) & (DEC_SEQ - 1)
    key = lax.broadcasted_iota(jnp.int32, (rows, NEW_KEYS_PAD), 1)
    sc = jnp.where(key <= tok, sc, -jnp.inf)
    _, l_fin, acc = update(carry, sc, kvn)
    o_ref[0] = (acc / l_fin).reshape(N_HEADS, DEC_SEQ, KV_LORA)


def _decode_attn(page_table, q4, ckv, kpe, cache_ckv, cache_kpe):
    seq_per_blk = Q_BLOCK // DEC_SEQ
    blk0 = N_PROMPT // Q_BLOCK
    row0 = N_PROMPT // DEC_SEQ
    grid_spec = pltpu.PrefetchScalarGridSpec(
        num_scalar_prefetch=1,
        grid=(DEC_BATCH,),
        in_specs=[
            pl.BlockSpec((1, N_HEADS, DEC_SEQ, D_QK),
                         lambda s, pt: (blk0 + s // seq_per_blk, 0, s % seq_per_blk, 0)),
            pl.BlockSpec((DEC_SEQ, KV_LORA), lambda s, pt: (row0 + s, 0)),
            pl.BlockSpec((DEC_SEQ, QK_ROPE), lambda s, pt: (row0 + s, 0)),
            pl.BlockSpec(memory_space=pl.ANY),
            pl.BlockSpec(memory_space=pl.ANY),
        ],
        out_specs=pl.BlockSpec((1, N_HEADS, DEC_SEQ, KV_LORA),
                               lambda s, pt: (s // seq_per_blk, 0, s % seq_per_blk, 0)),
        scratch_shapes=[
            pltpu.VMEM((2, CHUNK_KEYS, KV_LORA), F32),
            pltpu.VMEM((2, CHUNK_KEYS, QK_ROPE), F32),
            pltpu.SemaphoreType.DMA((2, 2)),
        ],
    )
    return pl.pallas_call(
        _decode_attn_kernel,
        grid_spec=grid_spec,
        out_shape=jax.ShapeDtypeStruct((N_SAMPLE // Q_BLOCK, N_HEADS, Q_BLOCK, KV_LORA), F32),
        compiler_params=_cparams(("arbitrary",)),
        name="decode_attn",
    )(page_table, q4, ckv, kpe, cache_ckv, cache_kpe)


def _uv_proj_kernel(ol_ref, wuv_ref, o_ref, *, nsub):
    for j in range(nsub):
        for h in range(N_HEADS):
            o = _dot(ol_ref[j, h].astype(BF16), wuv_ref[h])
            o_ref[j * Q_BLOCK:(j + 1) * Q_BLOCK, h * V_HEAD:(h + 1) * V_HEAD] = o.astype(o_ref.dtype)


def _uv_proj(o_lat, wuv, nsub=4):
    nb = o_lat.shape[0]
    return pl.pallas_call(
        functools.partial(_uv_proj_kernel, nsub=nsub),
        grid=(nb // nsub,),
        in_specs=[
            pl.BlockSpec((nsub, N_HEADS, Q_BLOCK, KV_LORA), lambda i: (i, 0, 0, 0)),
            pl.BlockSpec((N_HEADS, KV_LORA, V_HEAD), lambda i: (0, 0, 0)),
        ],
        out_specs=pl.BlockSpec((nsub * Q_BLOCK, D_MLA_OUT), lambda i: (i, 0)),
        out_shape=jax.ShapeDtypeStruct((nb * Q_BLOCK, D_MLA_OUT), BF16),
        compiler_params=_cparams(("parallel",)),
        name="uv_proj",
    )(o_lat, wuv)


def _mm_ln_kernel(lhs_ref, w_ref, res_ref, g_ref, b_ref, of_ref, ob_ref, acc_ref):
    k = pl.program_id(1)

    @pl.when(k == 0)
    def _():
        acc_ref[...] = jnp.zeros_like(acc_ref)

    acc_ref[...] += _dot(lhs_ref[...], w_ref[...])

    @pl.when(k == pl.num_programs(1) - 1)
    def _():
        y = _layer_norm(DEEPNORM_ALPHA * res_ref[...] + acc_ref[...], g_ref[...], b_ref[...])
        of_ref[...] = y
        ob_ref[...] = y.astype(BF16)


def _mm_ln(lhs, w, resid, g, b, tm=512, tk=None):
    n, kdim = lhs.shape
    tk = kdim if tk is None else tk
    return pl.pallas_call(
        _mm_ln_kernel,
        grid=(n // tm, kdim // tk),
        in_specs=[
            pl.BlockSpec((tm, tk), lambda i, k: (i, k)),
            pl.BlockSpec((tk, D_MODEL), lambda i, k: (k, 0)),
            pl.BlockSpec((tm, D_MODEL), lambda i, k: (i, 0)),
            pl.BlockSpec((1, D_MODEL), lambda i, k: (0, 0)),
            pl.BlockSpec((1, D_MODEL), lambda i, k: (0, 0)),
        ],
        out_specs=[
            pl.BlockSpec((tm, D_MODEL), lambda i, k: (i, 0)),
            pl.BlockSpec((tm, D_MODEL), lambda i, k: (i, 0)),
        ],
        out_shape=[
            jax.ShapeDtypeStruct((n, D_MODEL), F32),
            jax.ShapeDtypeStruct((n, D_MODEL), BF16),
        ],
        scratch_shapes=[pltpu.VMEM((tm, D_MODEL), F32)],
        compiler_params=_cparams(("parallel", "arbitrary")),
        name="mm_ln",
    )(lhs, w, resid, g, b)


def _ffn_up_kernel(x_ref, wg_ref, wu_ref, h_ref):
    x = x_ref[...]
    h_ref[...] = (jax.nn.silu(_dot(x, wg_ref[...])) * _dot(x, wu_ref[...])).astype(h_ref.dtype)


def _ffn_up(xb, wg, wu, tm=1024, tn=512):
    n = xb.shape[0]
    return pl.pallas_call(
        _ffn_up_kernel,
        grid=(n // tm, D_FF // tn),
        in_specs=[
            pl.BlockSpec((tm, D_MODEL), lambda i, j: (i, 0)),
            pl.BlockSpec((D_MODEL, tn), lambda i, j: (0, j)),
            pl.BlockSpec((D_MODEL, tn), lambda i, j: (0, j)),
        ],
        out_specs=pl.BlockSpec((tm, tn), lambda i, j: (i, j)),
        out_shape=jax.ShapeDtypeStruct((n, D_FF), BF16),
        compiler_params=_cparams(("parallel", "arbitrary")),
        name="ffn_up",
    )(xb, wg, wu)


def _in_odd_kernel(x_ref, wv_ref, wb_ref, wc_ref, z_ref, gb_ref):
    x = x_ref[...]
    z_ref[...] = _dot(x, wc_ref[...]) * _dot(x, wv_ref[...])
    gb_ref[...] = _dot(x, wb_ref[...])


def _in_odd(xb, w_in, tm=1024, tn=512):
    n = xb.shape[0]
    nb = D_MODEL // tn
    return pl.pallas_call(
        _in_odd_kernel,
        grid=(n // tm, nb),
        in_specs=[
            pl.BlockSpec((tm, D_MODEL), lambda i, j: (i, 0)),
            pl.BlockSpec((D_MODEL, tn), lambda i, j: (0, j)),
            pl.BlockSpec((D_MODEL, tn), lambda i, j: (0, nb + j)),
            pl.BlockSpec((D_MODEL, tn), lambda i, j: (0, 2 * nb + j)),
        ],
        out_specs=[
            pl.BlockSpec((tm, tn), lambda i, j: (i, j)),
            pl.BlockSpec((tm, tn), lambda i, j: (i, j)),
        ],
        out_shape=[
            jax.ShapeDtypeStruct((n, D_MODEL), F32),
            jax.ShapeDtypeStruct((n, D_MODEL), F32),
        ],
        compiler_params=_cparams(("parallel", "arbitrary")),
        name="in_odd",
    )(xb, w_in, w_in, w_in)


def _conv_taps(read, cw_ref):
    conv = read(CONV_W - 1) * cw_ref[0:1, :]
    for j in range(1, CONV_W):
        conv = conv + read(CONV_W - 1 - j) * cw_ref[j:j + 1, :]
    return conv


def _conv_prompt_kernel(z_ref, gb_ref, cw_ref, g_ref, zp_ref, *, T):
    t = pl.program_id(1)

    @pl.when(t == 0)
    def _():
        zp_ref[0:CONV_HIST, :] = jnp.zeros((CONV_HIST, D_MODEL), F32)

    zp_ref[CONV_HIST:CONV_HIST + T, :] = z_ref[...]
    conv = _conv_taps(lambda back: zp_ref[CONV_HIST - back:CONV_HIST - back + T, :], cw_ref)
    g_ref[...] = (gb_ref[...] * conv).astype(g_ref.dtype)
    zp_ref[0:CONV_HIST, :] = zp_ref[T:T + CONV_HIST, :]


def _conv_prompt(z, gb, conv_w, T=256):
    nt = SEQ // T
    tile = lambda b, t: (b * nt + t, 0)
    return pl.pallas_call(
        functools.partial(_conv_prompt_kernel, T=T),
        grid=(BATCH, nt),
        in_specs=[
            pl.BlockSpec((T, D_MODEL), tile),
            pl.BlockSpec((T, D_MODEL), tile),
            pl.BlockSpec((CONV_W, D_MODEL), lambda b, t: (0, 0)),
        ],
        out_specs=pl.BlockSpec((T, D_MODEL), tile),
        out_shape=jax.ShapeDtypeStruct((N_PROMPT, D_MODEL), BF16),
        scratch_shapes=[pltpu.VMEM((T + CONV_HIST, D_MODEL), F32)],
        compiler_params=_cparams(("parallel", "arbitrary")),
        name="conv_prompt",
    )(z, gb, conv_w)


def _conv_sample_kernel(z_ref, hist_ref, gb_ref, cw_ref, g_ref, zp_ref, *, S):
    zp_ref[:, 0:CONV_HIST, :] = hist_ref[...]
    zp_ref[:, CONV_HIST:, :] = z_ref[...]
    cw = cw_ref[...]
    conv = zp_ref[:, CONV_HIST - (CONV_W - 1):CONV_HIST - (CONV_W - 1) + DEC_SEQ, :] * cw[0:1, :]
    for j in range(1, CONV_W):
        back = CONV_W - 1 - j
        conv = conv + zp_ref[:, CONV_HIST - back:CONV_HIST - back + DEC_SEQ, :] * cw[j:j + 1, :]
    g_ref[...] = (gb_ref[...] * conv).astype(g_ref.dtype)


def _conv_sample(z3, hist, gb3, conv_w, S=32):
    blk = lambda i: (i, 0, 0)
    return pl.pallas_call(
        functools.partial(_conv_sample_kernel, S=S),
        grid=(DEC_BATCH // S,),
        in_specs=[
            pl.BlockSpec((S, DEC_SEQ, D_MODEL), blk),
            pl.BlockSpec((S, CONV_HIST, D_MODEL), blk),
            pl.BlockSpec((S, DEC_SEQ, D_MODEL), blk),
            pl.BlockSpec((CONV_W, D_MODEL), lambda i: (0, 0)),
        ],
        out_specs=pl.BlockSpec((S, DEC_SEQ, D_MODEL), blk),
        out_shape=jax.ShapeDtypeStruct((DEC_BATCH, DEC_SEQ, D_MODEL), F32),
        scratch_shapes=[pltpu.VMEM((S, CONV_HIST + DEC_SEQ, D_MODEL), F32)],
        compiler_params=_cparams(("parallel",)),
        name="conv_sample",
    )(z3, hist, gb3, conv_w)


def _rope_tables():
    pos = jnp.concatenate([jnp.tile(jnp.arange(SEQ), BATCH),
                           jnp.tile(PAST_LEN + jnp.arange(DEC_SEQ), DEC_BATCH)])
    inv = ROPE_THETA ** (-jnp.arange(0, QK_ROPE, 2, dtype=F32) / QK_ROPE)
    ang = pos.astype(F32)[:, None] * inv[None, :]
    cos, sin = jnp.cos(ang), jnp.sin(ang)
    return jnp.tile(jnp.concatenate([cos, cos], axis=1), (1, 2)), jnp.tile(jnp.concatenate([-sin, sin], axis=1), (1, 2))


def _swap_halves(w):
    half = QK_ROPE // 2
    return jnp.concatenate([w[..., half:], w[..., :half]], axis=-1)


def kernel(x_prompt, x_sample, cache_ckv, cache_kpe, state_pool, state_conv, page_table, w_in_even, pool_w, pool_scale, q_norm_g, w_q_b, kv_norm_g, w_uk, w_uv, w_out_even, w_in_odd, conv_w, w_out_odd, ffn_w_gate, ffn_w_up, ffn_w_down, ln1_g, ln1_b, ln2_g, ln2_b):
    assert DEPTH == 2
    n_pool_pages = cache_ckv.shape[1]
    cos, sin = _rope_tables()
    x = jnp.concatenate([x_prompt.reshape(N_PROMPT, D_MODEL), x_sample.reshape(N_SAMPLE, D_MODEL)], axis=0)

    def row(v):
        return v.reshape(1, -1)

    def ffn(layer, xf, xb):
        hmid = _ffn_up(xb, ffn_w_gate[layer].astype(BF16), ffn_w_up[layer].astype(BF16))
        return _mm_ln(hmid, ffn_w_down[layer].astype(BF16), xf, row(ln2_g[layer]), row(ln2_b[layer]), tk=D_FF // 4)

    e = 0
    w_in = w_in_even[e]
    o0, o1, o2 = D_POOL, D_POOL + Q_LORA, D_POOL + Q_LORA + KV_LORA
    w_pe = w_in[:, o2:]
    u, cq, ckv, kpe, kvb = _in_even(
        x, w_in[:, :o0].astype(BF16), w_in[:, o0:o1].astype(BF16), w_in[:, o1:o2].astype(BF16),
        w_pe.astype(BF16), _swap_halves(w_pe).astype(BF16), row(q_norm_g[e]), row(kv_norm_g[e]), cos, sin)

    wq = w_q_b[e]
    wq_rope = wq[:, :, QK_NOPE:]
    q4 = _q_proj(
        cq,
        wq[:, :, :QK_NOPE].reshape(Q_LORA, N_HEADS * QK_NOPE).astype(BF16),
        wq_rope.reshape(Q_LORA, N_HEADS * QK_ROPE).astype(BF16),
        _swap_halves(wq_rope).reshape(Q_LORA, N_HEADS * QK_ROPE).astype(BF16),
        jnp.transpose(w_uk[e], (1, 2, 0)).astype(BF16), cos, sin)

    pw = pool_w[e].astype(BF16)
    ps = row(pool_scale[e])
    u_sample = u[N_PROMPT:].reshape(DEC_BATCH, DEC_SEQ, D_POOL)
    pool_hist = jnp.pad(state_pool[e], ((0, 0), (POOL_HIST - POOL_BUF, 0), (0, 0)))
    y_pool = jnp.concatenate([_pool_prompt(u, pw, ps), _pool_sample(u_sample, pool_hist, pw, ps)], axis=0)

    wuv = jnp.transpose(w_uv[e], (1, 0, 2)).astype(BF16)
    o_prompt = _uv_proj(_prompt_attn(q4, kvb), wuv)
    o_lat_s = _decode_attn(page_table, q4, ckv, kpe,
                           cache_ckv[e].reshape(n_pool_pages, PAGE_SIZE, KV_LORA),
                           cache_kpe[e].reshape(n_pool_pages, PAGE_SIZE, QK_ROPE))
    o_sample = _uv_proj(o_lat_s, wuv)
    mix_in = jnp.concatenate([y_pool, jnp.concatenate([o_prompt, o_sample], axis=0)], axis=1)
    xf, xb = _mm_ln(mix_in, w_out_even[e].astype(BF16), x, row(ln1_g[0]), row(ln1_b[0]))
    xf, xb = ffn(0, xf, xb)

    od = 0
    z, gb = _in_odd(xb, w_in_odd[od].astype(BF16))
    z_sample = z[N_PROMPT:].reshape(DEC_BATCH, DEC_SEQ, D_MODEL)
    conv_hist = jnp.pad(state_conv[od], ((0, 0), (CONV_HIST - CONV_BUF, 0), (0, 0)))
    g_prompt = _conv_prompt(z, gb, conv_w[od])
    g_sample = _conv_sample(z_sample, conv_hist, gb[N_PROMPT:].reshape(DEC_BATCH, DEC_SEQ, D_MODEL), conv_w[od])
    g = jnp.concatenate([g_prompt, g_sample.reshape(N_SAMPLE, D_MODEL).astype(BF16)], axis=0)
    xf, xb = _mm_ln(g, w_out_odd[od].astype(BF16), xf, row(ln1_g[1]), row(ln1_b[1]))
    xf, xb = ffn(1, xf, xb)

    def split(v, d):
        return v[:N_PROMPT].reshape(BATCH, SEQ, d), v[N_PROMPT:].reshape(DEC_BATCH, DEC_SEQ, d)

    y_prompt, y_sample = split(xf, D_MODEL)
    ckv_p, ckv_s = split(ckv, KV_LORA)
    kpe_p, kpe_s = split(kpe, QK_ROPE)
    u_p, u_s = split(u, D_POOL)
    z_p, z_s = split(z, D_MODEL)
    pool_prompt = u_p[:, SEQ - POOL_BUF:]
    pool_sample = jnp.concatenate([state_pool[e][:, DEC_SEQ:], u_s], axis=1)
    conv_prompt = z_p[:, SEQ - CONV_BUF:]
    conv_sample = z_s[:, DEC_SEQ - CONV_BUF:]
    return (y_prompt, y_sample, ckv_p[None], kpe_p[None], pool_prompt[None], conv_prompt[None],
            ckv_s[None], kpe_s[None], pool_sample[None], conv_sample[None])
```

```python
import functools

import jax
import jax.numpy as jnp
from jax import lax
from jax.experimental import pallas as pl
from jax.experimental.pallas import tpu as pltpu

F32 = jnp.float32
BF16 = jnp.bfloat16

D_MODEL = 2048
BATCH = 4
SEQ = 2048
DEPTH = 2
DEC_BATCH = 128
DEC_SEQ = 8
PAST_LEN = 16384
PAGE_SIZE = 128
N_PAGES = PAST_LEN // PAGE_SIZE
POOL_WINDOWS = (2, 4, 8, 16)
N_POOL_GROUPS = len(POOL_WINDOWS)
D_POOL = D_MODEL // 2
POOL_GROUP = D_POOL // N_POOL_GROUPS
POOL_BUF = max(POOL_WINDOWS) - 1
N_HEADS = 8
QK_NOPE = 128
QK_ROPE = 64
V_HEAD = 128
Q_LORA = 512
KV_LORA = 512
ROPE_THETA = 10000.0
ATTN_SCALE = (QK_NOPE + QK_ROPE) ** -0.5
Q_BLOCK = 128
D_MLA_OUT = N_HEADS * V_HEAD
CONV_W = 3
CONV_BUF = CONV_W - 1
D_FF = ((8 * D_MODEL + 3 * 256 - 1) // (3 * 256)) * 256
DEEPNORM_ALPHA = (2 * DEPTH) ** 0.25

N_PROMPT = BATCH * SEQ
N_SAMPLE = DEC_BATCH * DEC_SEQ
N_TOK = N_PROMPT + N_SAMPLE
D_QK = KV_LORA + QK_ROPE
POOL_HIST = 16
CONV_HIST = 8
KV_CHUNK = 256
VMEM_LIMIT_MB = 56


def _cparams(dims, vmem_mb=VMEM_LIMIT_MB):
    return pltpu.CompilerParams(dimension_semantics=dims, vmem_limit_bytes=vmem_mb * 1024 * 1024)


def _resident(shape):
    zeros = (0,) * len(shape)
    return pl.BlockSpec(shape, lambda *_: zeros, pipeline_mode=pl.Buffered(1))


def _prompt_rows(tm, width):
    last = N_PROMPT // tm - 1
    return pl.BlockSpec((tm, width), lambda i, *_: (jnp.minimum(i, last), 0))


def _sample_rows(tm, width):
    first = N_PROMPT // tm
    return pl.BlockSpec((tm, width), lambda i, *_: (jnp.maximum(i - first, 0), 0))


def _dot(a, b):
    return jnp.dot(a, b, preferred_element_type=F32)


def _dot_nt(a, b):
    return lax.dot_general(a, b, (((1,), (1,)), ((), ())), preferred_element_type=F32)


def _rms(h, g, eps=1e-6):
    return h * lax.rsqrt(jnp.mean(h * h, axis=-1, keepdims=True) + eps) * g


def _layer_norm(v, g, b, eps=1e-5):
    mu = jnp.mean(v, axis=-1, keepdims=True)
    vc = v - mu
    var = jnp.mean(vc * vc, axis=-1, keepdims=True)
    return vc * lax.rsqrt(var + eps) * g + b


def _softmax_update(m_old, l_old, acc, s, v):
    m_new = jnp.maximum(m_old, jnp.max(s, axis=-1, keepdims=True))
    alpha = jnp.exp(m_old - m_new)
    p = jnp.exp(s - m_new)
    l_new = alpha * l_old + jnp.sum(p, axis=-1, keepdims=True)
    return m_new, l_new, alpha * acc + _dot(p.astype(BF16), v)


def _in_even_kernel(xp_ref, xs_ref, wu_ref, wq_ref, wkv_ref, wpe_ref, wper_ref, wpet_ref, wpert_ref,
                    qg_ref, kvg_ref, cos_ref, sin_ref, cost_ref, sint_ref,
                    u_ref, cq_ref, ckvp_ref, ckvs_ref, kpep_ref, kpes_ref, vb_ref, kt_ref, *, tm):
    i = pl.program_id(0)
    is_prompt = i < N_PROMPT // tm
    xb = jnp.where(is_prompt, xp_ref[...], xs_ref[...]).astype(BF16)
    u_ref[...] = _dot(xb, wu_ref[...])
    cq_ref[...] = _rms(_dot(xb, wq_ref[...]), qg_ref[...]).astype(BF16)
    ckv = _rms(_dot(xb, wkv_ref[...]), kvg_ref[...])
    kpe = _dot(xb, wpe_ref[...]) * cos_ref[:, :QK_ROPE] + _dot(xb, wper_ref[...]) * sin_ref[:, :QK_ROPE]

    @pl.when(is_prompt)
    def _():
        ckvp_ref[...] = ckv
        kpep_ref[...] = kpe
        ckvb = ckv.astype(BF16)
        vb_ref[...] = ckvb
        kpet = _dot_nt(wpet_ref[...], xb) * cost_ref[...] + _dot_nt(wpert_ref[...], xb) * sint_ref[...]
        for c in range(tm // KV_CHUNK):
            cols = slice(c * KV_CHUNK, (c + 1) * KV_CHUNK)
            kt_ref[c, :KV_LORA, :] = ckv[cols, :].T.astype(BF16)
            kt_ref[c, KV_LORA:, :] = kpet[:, cols].astype(BF16)

    @pl.when(jnp.logical_not(is_prompt))
    def _():
        ckvs_ref[...] = ckv
        kpes_ref[...] = kpe


def _in_even(xp, xs, wu, wq, wkv, wpe, wper, qg, kvg, cos, sin, cost, sint, tm=512):
    row = lambda i: (i, 0)
    cpb = tm // KV_CHUNK
    last = N_PROMPT // tm - 1
    return pl.pallas_call(
        functools.partial(_in_even_kernel, tm=tm),
        grid=(N_TOK // tm,),
        in_specs=[
            _prompt_rows(tm, D_MODEL),
            _sample_rows(tm, D_MODEL),
            _resident((D_MODEL, D_POOL)),
            _resident((D_MODEL, Q_LORA)),
            _resident((D_MODEL, KV_LORA)),
            _resident((D_MODEL, QK_ROPE)),
            _resident((D_MODEL, QK_ROPE)),
            _resident((QK_ROPE, D_MODEL)),
            _resident((QK_ROPE, D_MODEL)),
            _resident((1, Q_LORA)),
            _resident((1, KV_LORA)),
            pl.BlockSpec((tm, 2 * QK_ROPE), row),
            pl.BlockSpec((tm, 2 * QK_ROPE), row),
            pl.BlockSpec((QK_ROPE, tm), lambda i: (0, jnp.minimum(i, last))),
            pl.BlockSpec((QK_ROPE, tm), lambda i: (0, jnp.minimum(i, last))),
        ],
        out_specs=[
            pl.BlockSpec((tm, D_POOL), row),
            pl.BlockSpec((tm, Q_LORA), row),
            _prompt_rows(tm, KV_LORA),
            _sample_rows(tm, KV_LORA),
            _prompt_rows(tm, QK_ROPE),
            _sample_rows(tm, QK_ROPE),
            _prompt_rows(tm, KV_LORA),
            pl.BlockSpec((cpb, D_QK, KV_CHUNK), lambda i: (jnp.minimum(i, last), 0, 0)),
        ],
        out_shape=[
            jax.ShapeDtypeStruct((N_TOK, D_POOL), F32),
            jax.ShapeDtypeStruct((N_TOK, Q_LORA), BF16),
            jax.ShapeDtypeStruct((N_PROMPT, KV_LORA), F32),
            jax.ShapeDtypeStruct((N_SAMPLE, KV_LORA), F32),
            jax.ShapeDtypeStruct((N_PROMPT, QK_ROPE), F32),
            jax.ShapeDtypeStruct((N_SAMPLE, QK_ROPE), F32),
            jax.ShapeDtypeStruct((N_PROMPT, KV_LORA), BF16),
            jax.ShapeDtypeStruct((N_PROMPT // KV_CHUNK, D_QK, KV_CHUNK), BF16),
        ],
        compiler_params=_cparams(("arbitrary",)),
        name="in_even",
    )(xp, xs, wu, wq, wkv, wpe, wper, wpe.T, wper.T, qg, kvg, cos, sin, cost, sint)


def _q_proj_kernel(cq_ref, wqn_ref, wqr_ref, wqrr_ref, wukt_ref, cos_ref, sin_ref, qp_ref, qs_ref, *, nsub):
    i = pl.program_id(0)
    is_prompt = i < N_PROMPT // (nsub * Q_BLOCK)
    cq = cq_ref[...]
    qn = _dot(cq, wqn_ref[...])
    cos = jnp.concatenate([cos_ref[...]] * (N_HEADS // 2), axis=1)
    sin = jnp.concatenate([sin_ref[...]] * (N_HEADS // 2), axis=1)
    qpe = (_dot(cq, wqr_ref[...]) * cos + _dot(cq, wqrr_ref[...]) * sin) * ATTN_SCALE

    def emit(q_ref):
        for h in range(N_HEADS):
            ql = _dot(qn[:, h * QK_NOPE:(h + 1) * QK_NOPE].astype(BF16), wukt_ref[h]) * ATTN_SCALE
            for j in range(nsub):
                rows = slice(j * Q_BLOCK, (j + 1) * Q_BLOCK)
                q_ref[j, h, :, :KV_LORA] = ql[rows].astype(q_ref.dtype)
                q_ref[j, h, :, KV_LORA:] = qpe[rows, h * QK_ROPE:(h + 1) * QK_ROPE].astype(q_ref.dtype)

    @pl.when(is_prompt)
    def _():
        emit(qp_ref)

    @pl.when(jnp.logical_not(is_prompt))
    def _():
        emit(qs_ref)


def _q_proj(cq, wqn, wqr, wqrr, wukt, cos, sin, tm=256):
    nsub = tm // Q_BLOCK
    nbp = N_PROMPT // tm
    row = lambda i: (i, 0)
    blk = (nsub, N_HEADS, Q_BLOCK, D_QK)
    return pl.pallas_call(
        functools.partial(_q_proj_kernel, nsub=nsub),
        grid=(N_TOK // tm,),
        in_specs=[
            pl.BlockSpec((tm, Q_LORA), row),
            _resident((Q_LORA, N_HEADS * QK_NOPE)),
            _resident((Q_LORA, N_HEADS * QK_ROPE)),
            _resident((Q_LORA, N_HEADS * QK_ROPE)),
            _resident((N_HEADS, QK_NOPE, KV_LORA)),
            pl.BlockSpec((tm, 2 * QK_ROPE), row),
            pl.BlockSpec((tm, 2 * QK_ROPE), row),
        ],
        out_specs=[
            pl.BlockSpec(blk, lambda i: (jnp.minimum(i, nbp - 1), 0, 0, 0)),
            pl.BlockSpec(blk, lambda i: (jnp.maximum(i - nbp, 0), 0, 0, 0)),
        ],
        out_shape=[
            jax.ShapeDtypeStruct((N_PROMPT // Q_BLOCK, N_HEADS, Q_BLOCK, D_QK), BF16),
            jax.ShapeDtypeStruct((N_SAMPLE // Q_BLOCK, N_HEADS, Q_BLOCK, D_QK), F32),
        ],
        compiler_params=_cparams(("arbitrary",)),
        name="q_proj",
    )(cq, wqn, wqr, wqrr, wukt, cos, sin)


def _pool_groups(read, pos, w_ref, scale_ref, write):
    for g, w in enumerate(POOL_WINDOWS):
        lanes = slice(g * POOL_GROUP, (g + 1) * POOL_GROUP)
        cur = read(0, lanes)
        win = cur
        for j in range(1, w):
            win = win + read(j, lanes)
        cnt = jnp.minimum(pos + 1, w).astype(F32)
        diff = (win / cnt - cur).astype(BF16)
        write(lanes, _dot(diff, w_ref[g]) * scale_ref[:, lanes])


def _pool_prompt_kernel(u_ref, w_ref, scale_ref, y_ref, xp_ref, *, T):
    t = pl.program_id(1)

    @pl.when(t == 0)
    def _():
        xp_ref[0:POOL_HIST, :] = jnp.zeros((POOL_HIST, D_POOL), F32)

    xp_ref[POOL_HIST:POOL_HIST + T, :] = u_ref[...]
    pos = t * T + lax.broadcasted_iota(jnp.int32, (T, 1), 0)

    def read(j, lanes):
        return xp_ref[POOL_HIST - j:POOL_HIST - j + T, lanes]

    def write(lanes, y):
        y_ref[:, lanes] = y.astype(y_ref.dtype)

    _pool_groups(read, pos, w_ref, scale_ref, write)
    xp_ref[0:POOL_HIST, :] = xp_ref[T:T + POOL_HIST, :]


def _pool_prompt(u, pool_w, pool_scale, T=512):
    nt = SEQ // T
    return pl.pallas_call(
        functools.partial(_pool_prompt_kernel, T=T),
        grid=(BATCH, nt),
        in_specs=[
            pl.BlockSpec((T, D_POOL), lambda b, t: (b * nt + t, 0)),
            _resident((N_POOL_GROUPS, POOL_GROUP, POOL_GROUP)),
            _resident((1, D_POOL)),
        ],
        out_specs=pl.BlockSpec((T, D_POOL), lambda b, t: (b * nt + t, 0)),
        out_shape=jax.ShapeDtypeStruct((N_PROMPT, D_POOL), BF16),
        scratch_shapes=[pltpu.VMEM((T + POOL_HIST, D_POOL), F32)],
        compiler_params=_cparams(("arbitrary", "arbitrary")),
        name="pool_prompt",
    )(u, pool_w, pool_scale)


def _pool_sample_kernel(u_ref, hist_ref, w_ref, scale_ref, y_ref, xp_ref, *, S):
    xp_ref[:, 0:POOL_HIST, :] = hist_ref[...]
    xp_ref[:, POOL_HIST:, :] = u_ref[...]
    pos = PAST_LEN + lax.broadcasted_iota(jnp.int32, (S, DEC_SEQ, 1), 1)
    pos = pos.reshape(S * DEC_SEQ, 1)

    def read(j, lanes):
        return xp_ref[:, POOL_HIST - j:POOL_HIST - j + DEC_SEQ, lanes].reshape(S * DEC_SEQ, POOL_GROUP)

    def write(lanes, y):
        y_ref[:, lanes] = y.astype(y_ref.dtype)

    _pool_groups(read, pos, w_ref, scale_ref, write)


def _pool_sample(u3, hist, pool_w, pool_scale, S=32):
    rows = S * DEC_SEQ
    return pl.pallas_call(
        functools.partial(_pool_sample_kernel, S=S),
        grid=(DEC_BATCH // S,),
        in_specs=[
            pl.BlockSpec((S, DEC_SEQ, D_POOL), lambda i: (i, 0, 0)),
            pl.BlockSpec((S, POOL_HIST, D_POOL), lambda i: (i, 0, 0)),
            _resident((N_POOL_GROUPS, POOL_GROUP, POOL_GROUP)),
            _resident((1, D_POOL)),
        ],
        out_specs=pl.BlockSpec((rows, D_POOL), lambda i: (i, 0)),
        out_shape=jax.ShapeDtypeStruct((N_SAMPLE, D_POOL), BF16),
        scratch_shapes=[pltpu.VMEM((S, POOL_HIST + DEC_SEQ, D_POOL), F32)],
        compiler_params=_cparams(("arbitrary",)),
        name="pool_sample",
    )(u3, hist, pool_w, pool_scale)


def _prompt_attn_kernel(q_ref, kt_ref, v_ref, wuv_ref, o_ref, m_ref, l_ref, acc_ref, sa_ref, sb_ref):
    i = pl.program_id(1)
    rows_all = N_HEADS * Q_BLOCK
    m_ref[...] = jnp.full((rows_all, 1), -jnp.inf, F32)
    l_ref[...] = jnp.zeros((rows_all, 1), F32)
    acc_ref[...] = jnp.zeros((rows_all, KV_LORA), F32)
    q = q_ref[0].reshape(rows_all, D_QK)
    q_pos = i * Q_BLOCK + (lax.broadcasted_iota(jnp.int32, (rows_all, KV_CHUNK), 0) & (Q_BLOCK - 1))
    key = lax.broadcasted_iota(jnp.int32, (rows_all, KV_CHUNK), 1)

    def scores(j, s_ref):
        s = _dot(q, kt_ref[j])
        s_ref[...] = jnp.where(j * KV_CHUNK + key <= q_pos, s, -jnp.inf)

    def attend(j, s_ref):
        v = v_ref[pl.ds(pl.multiple_of(j * KV_CHUNK, KV_CHUNK), KV_CHUNK), :]
        m_new, l_new, acc_new = _softmax_update(m_ref[...], l_ref[...], acc_ref[...], s_ref[...], v)
        m_ref[...] = m_new
        l_ref[...] = l_new
        acc_ref[...] = acc_new

    n_chunks = (i * Q_BLOCK + Q_BLOCK + KV_CHUNK - 1) // KV_CHUNK
    n_pairs = (n_chunks - 1) // 2
    scores(0, sa_ref)

    def body(t, c):
        scores(2 * t + 1, sb_ref)
        attend(2 * t, sa_ref)
        scores(2 * t + 2, sa_ref)
        attend(2 * t + 1, sb_ref)
        return c

    lax.fori_loop(0, n_pairs, body, 0)
    done = 2 * n_pairs

    @pl.when(done + 2 == n_chunks)
    def _():
        scores(done + 1, sb_ref)
        attend(done, sa_ref)
        attend(done + 1, sb_ref)

    @pl.when(done + 1 == n_chunks)
    def _():
        attend(done, sa_ref)

    for h in range(N_HEADS):
        rows = slice(h * Q_BLOCK, (h + 1) * Q_BLOCK)
        o_lat = (acc_ref[rows] * (1.0 / l_ref[rows])).astype(BF16)
        o_ref[:, h * V_HEAD:(h + 1) * V_HEAD] = _dot(o_lat, wuv_ref[h]).astype(o_ref.dtype)


def _prompt_attn(qp, kt, vb, wuv):
    nqb = SEQ // Q_BLOCK
    cps = SEQ // KV_CHUNK
    rows = N_HEADS * Q_BLOCK
    return pl.pallas_call(
        _prompt_attn_kernel,
        grid=(BATCH, nqb),
        in_specs=[
            pl.BlockSpec((1, N_HEADS, Q_BLOCK, D_QK), lambda b, i: (b * nqb + i, 0, 0, 0)),
            pl.BlockSpec((cps, D_QK, KV_CHUNK), lambda b, i: (b, 0, 0)),
            pl.BlockSpec((SEQ, KV_LORA), lambda b, i: (b, 0)),
            _resident((N_HEADS, KV_LORA, V_HEAD)),
        ],
        out_specs=pl.BlockSpec((Q_BLOCK, D_MLA_OUT), lambda b, i: (b * nqb + i, 0)),
        out_shape=jax.ShapeDtypeStruct((N_PROMPT, D_MLA_OUT), BF16),
        scratch_shapes=[
            pltpu.VMEM((rows, 1), F32),
            pltpu.VMEM((rows, 1), F32),
            pltpu.VMEM((rows, KV_LORA), F32),
            pltpu.VMEM((rows, KV_CHUNK), F32),
            pltpu.VMEM((rows, KV_CHUNK), F32),
        ],
        compiler_params=_cparams(("arbitrary", "arbitrary")),
        name="prompt_attn",
    )(qp, kt, vb, wuv)


PAGES_PER_CHUNK = 16
CHUNK_KEYS = PAGES_PER_CHUNK * PAGE_SIZE
CHUNKS_PER_SEQ = N_PAGES // PAGES_PER_CHUNK
DEC_SLOTS = 4
DEC_AHEAD = DEC_SLOTS - 1
NEW_KEYS_PAD = 16
assert CHUNKS_PER_SEQ % DEC_SLOTS == 0


def _decode_attn_kernel(pt_ref, q_ref, ckvn_ref, kpen_ref, cckv_hbm, ckpet_hbm, o_ref,
                        ckv_buf, kpet_buf, kvb_a, kvb_b, sc_a, sc_b, sem):
    s = pl.program_id(0)
    nseq = pl.num_programs(0)
    rows = N_HEADS * DEC_SEQ

    def page_copies(seq, c, p):
        slot = c % DEC_SLOTS
        page = pt_ref[(seq * CHUNKS_PER_SEQ + c) * PAGES_PER_CHUNK + p]
        keys = pl.ds(p * PAGE_SIZE, PAGE_SIZE)
        return (pltpu.make_async_copy(cckv_hbm.at[page], ckv_buf.at[slot, keys, :], sem.at[0, slot]),
                pltpu.make_async_copy(ckpet_hbm.at[page], kpet_buf.at[slot, :, keys], sem.at[1, slot]))

    def start_chunk(seq, c):
        for p in range(PAGES_PER_CHUNK):
            for cp in page_copies(seq, c, p):
                cp.start()

    def wait_chunk(seq, c):
        for p in range(PAGES_PER_CHUNK):
            for cp in page_copies(seq, c, p):
                cp.wait()

    @pl.when(s == 0)
    def _():
        for c in range(DEC_AHEAD):
            start_chunk(0, c)

    q = q_ref[0].reshape(rows, D_QK)
    q_lat = q[:, :KV_LORA].astype(BF16)
    q_pe = q[:, KV_LORA:].astype(BF16)

    def scores(c, kvb_ref, sc_ref):
        nxt = c + DEC_AHEAD
        if nxt < CHUNKS_PER_SEQ:
            start_chunk(s, nxt)
        else:
            @pl.when(s + 1 < nseq)
            def _():
                start_chunk(s + 1, nxt - CHUNKS_PER_SEQ)
        wait_chunk(s, c)
        slot = c % DEC_SLOTS
        kvb_ref[...] = ckv_buf[slot].astype(BF16)
        sc_ref[...] = _dot_nt(q_lat, kvb_ref[...]) + _dot(q_pe, kpet_buf[slot].astype(BF16))

    def attend(carry, kvb_ref, sc_ref):
        return _softmax_update(*carry, sc_ref[...], kvb_ref[...])

    bufs = ((kvb_a, sc_a), (kvb_b, sc_b))
    carry = (jnp.full((rows, 1), -jnp.inf, F32), jnp.zeros((rows, 1), F32), jnp.zeros((rows, KV_LORA), F32))
    scores(0, *bufs[0])
    for c in range(1, CHUNKS_PER_SEQ):
        scores(c, *bufs[c % 2])
        carry = attend(carry, *bufs[(c - 1) % 2])
    carry = attend(carry, *bufs[(CHUNKS_PER_SEQ - 1) % 2])

    pad = NEW_KEYS_PAD - DEC_SEQ
    kvn = jnp.concatenate([ckvn_ref[...], jnp.zeros((pad, KV_LORA), F32)], axis=0).astype(BF16)
    kpn = jnp.concatenate([kpen_ref[...], jnp.zeros((pad, QK_ROPE), F32)], axis=0).astype(BF16)
    sc = _dot_nt(q_lat, kvn) + _dot_nt(q_pe, kpn)
    tok = lax.broadcasted_iota(jnp.int32, (rows, NEW_KEYS_PAD), 0) & (DEC_SEQ - 1)
    key = lax.broadcasted_iota(jnp.int32, (rows, NEW_KEYS_PAD), 1)
    sc = jnp.where(key <= tok, sc, -jnp.inf)
    _, l_fin, acc = _softmax_update(*carry, sc, kvn)
    o_ref[0] = (acc * (1.0 / l_fin)).reshape(N_HEADS, DEC_SEQ, KV_LORA)


def _decode_attn(page_table, qs, ckv_s, kpe_s, cache_ckv, cache_kpet):
    seq_per_blk = Q_BLOCK // DEC_SEQ
    grid_spec = pltpu.PrefetchScalarGridSpec(
        num_scalar_prefetch=1,
        grid=(DEC_BATCH,),
        in_specs=[
            pl.BlockSpec((1, N_HEADS, DEC_SEQ, D_QK), lambda s, pt: (s // seq_per_blk, 0, s % seq_per_blk, 0)),
            pl.BlockSpec((DEC_SEQ, KV_LORA), lambda s, pt: (s, 0)),
            pl.BlockSpec((DEC_SEQ, QK_ROPE), lambda s, pt: (s, 0)),
            pl.BlockSpec(memory_space=pl.ANY),
            pl.BlockSpec(memory_space=pl.ANY),
        ],
        out_specs=pl.BlockSpec((1, N_HEADS, DEC_SEQ, KV_LORA),
                               lambda s, pt: (s // seq_per_blk, 0, s % seq_per_blk, 0)),
        scratch_shapes=[
            pltpu.VMEM((DEC_SLOTS, CHUNK_KEYS, KV_LORA), F32),
            pltpu.VMEM((DEC_SLOTS, QK_ROPE, CHUNK_KEYS), F32),
            pltpu.VMEM((CHUNK_KEYS, KV_LORA), BF16),
            pltpu.VMEM((CHUNK_KEYS, KV_LORA), BF16),
            pltpu.VMEM((N_HEADS * DEC_SEQ, CHUNK_KEYS), F32),
            pltpu.VMEM((N_HEADS * DEC_SEQ, CHUNK_KEYS), F32),
            pltpu.SemaphoreType.DMA((2, DEC_SLOTS)),
        ],
    )
    return pl.pallas_call(
        _decode_attn_kernel,
        grid_spec=grid_spec,
        out_shape=jax.ShapeDtypeStruct((N_SAMPLE // Q_BLOCK, N_HEADS, Q_BLOCK, KV_LORA), F32),
        compiler_params=_cparams(("arbitrary",)),
        name="decode_attn",
    )(page_table.reshape(-1), qs, ckv_s, kpe_s, cache_ckv, cache_kpet)


def _uv_proj_kernel(ol_ref, wuv_ref, o_ref, *, nsub):
    for j in range(nsub):
        for h in range(N_HEADS):
            o = _dot(ol_ref[j, h].astype(BF16), wuv_ref[h])
            o_ref[j * Q_BLOCK:(j + 1) * Q_BLOCK, h * V_HEAD:(h + 1) * V_HEAD] = o.astype(o_ref.dtype)


def _uv_proj_sample(o_lat, wuv, nsub=4):
    nb = o_lat.shape[0]
    rows = nsub * Q_BLOCK
    return pl.pallas_call(
        functools.partial(_uv_proj_kernel, nsub=nsub),
        grid=(nb // nsub,),
        in_specs=[
            pl.BlockSpec((nsub, N_HEADS, Q_BLOCK, KV_LORA), lambda i: (i, 0, 0, 0)),
            _resident((N_HEADS, KV_LORA, V_HEAD)),
        ],
        out_specs=pl.BlockSpec((rows, D_MLA_OUT), lambda i: (i, 0)),
        out_shape=jax.ShapeDtypeStruct((nb * Q_BLOCK, D_MLA_OUT), BF16),
        compiler_params=_cparams(("arbitrary",)),
        name="uv_proj_sample",
    )(o_lat, wuv)


def _mm_ln_kernel(*refs, lhs_split, split_resid, split_out, tm):
    is_prompt = pl.program_id(0) < N_PROMPT // tm
    refs = list(refs)

    def take(split):
        if split:
            p_ref, s_ref = refs.pop(0), refs.pop(0)
            return jnp.where(is_prompt, p_ref[...], s_ref[...])
        return refs.pop(0)[...]

    lhs = [take(split) for split in lhs_split]
    w_ref = refs.pop(0)
    resid = take(split_resid)
    g_ref, b_ref = refs.pop(0), refs.pop(0)
    out_refs = refs

    acc = None
    off = 0
    for piece in lhs:
        k = piece.shape[1]
        part = _dot(piece, w_ref[off:off + k, :])
        acc = part if acc is None else acc + part
        off += k
    y = _layer_norm(DEEPNORM_ALPHA * resid + acc, g_ref[...], b_ref[...])
    if split_out:
        @pl.when(is_prompt)
        def _():
            out_refs[0][...] = y

        @pl.when(jnp.logical_not(is_prompt))
        def _():
            out_refs[1][...] = y
    else:
        out_refs[0][...] = y
        out_refs[1][...] = y.astype(BF16)


def _mm_ln(lhs_list, w, resid, g, b, *, tm, split_out=False):
    operands, in_specs = [], []

    def add_rows(v):
        if isinstance(v, tuple):
            operands.extend(v)
            in_specs.extend([_prompt_rows(tm, v[0].shape[1]), _sample_rows(tm, v[0].shape[1])])
        else:
            operands.append(v)
            in_specs.append(pl.BlockSpec((tm, v.shape[1]), lambda i: (i, 0)))
        return isinstance(v, tuple)

    lhs_split = tuple(add_rows(v) for v in lhs_list)
    operands.append(w)
    in_specs.append(_resident(w.shape))
    split_resid = add_rows(resid)
    operands.extend([g, b])
    in_specs.extend([_resident((1, D_MODEL)), _resident((1, D_MODEL))])
    if split_out:
        out_specs = [_prompt_rows(tm, D_MODEL), _sample_rows(tm, D_MODEL)]
        out_shape = [jax.ShapeDtypeStruct((N_PROMPT, D_MODEL), F32), jax.ShapeDtypeStruct((N_SAMPLE, D_MODEL), F32)]
    else:
        row = pl.BlockSpec((tm, D_MODEL), lambda i: (i, 0))
        out_specs = [row, row]
        out_shape = [jax.ShapeDtypeStruct((N_TOK, D_MODEL), F32), jax.ShapeDtypeStruct((N_TOK, D_MODEL), BF16)]
    return pl.pallas_call(
        functools.partial(_mm_ln_kernel, lhs_split=lhs_split, split_resid=split_resid, split_out=split_out, tm=tm),
        grid=(N_TOK // tm,),
        in_specs=in_specs,
        out_specs=out_specs,
        out_shape=out_shape,
        compiler_params=_cparams(("arbitrary",)),
        name="mm_ln",
    )(*operands)


def _ffn_up_kernel(x_ref, wg_ref, wu_ref, h_ref, wgb_ref, wub_ref):
    @pl.when(pl.program_id(1) == 0)
    def _():
        wgb_ref[...] = wg_ref[...].astype(BF16)
        wub_ref[...] = wu_ref[...].astype(BF16)

    x = x_ref[...]
    h_ref[...] = (jax.nn.silu(_dot(x, wgb_ref[...])) * _dot(x, wub_ref[...])).astype(h_ref.dtype)


def _ffn_up(xb, wg, wu, tm=1024, tn=512):
    return pl.pallas_call(
        _ffn_up_kernel,
        grid=(D_FF // tn, N_TOK // tm),
        in_specs=[
            pl.BlockSpec((tm, D_MODEL), lambda j, i: (i, 0)),
            pl.BlockSpec((D_MODEL, tn), lambda j, i: (0, j)),
            pl.BlockSpec((D_MODEL, tn), lambda j, i: (0, j)),
        ],
        out_specs=pl.BlockSpec((tm, tn), lambda j, i: (i, j)),
        out_shape=jax.ShapeDtypeStruct((N_TOK, D_FF), BF16),
        scratch_shapes=[pltpu.VMEM((D_MODEL, tn), BF16), pltpu.VMEM((D_MODEL, tn), BF16)],
        compiler_params=_cparams(("arbitrary", "arbitrary")),
        name="ffn_up",
    )(xb, wg, wu)


def _in_odd_kernel(x_ref, wv_ref, wb_ref, wc_ref, z_ref, gb_ref, wvb_ref, wbb_ref, wcb_ref):
    @pl.when(pl.program_id(1) == 0)
    def _():
        wvb_ref[...] = wv_ref[...].astype(BF16)
        wbb_ref[...] = wb_ref[...].astype(BF16)
        wcb_ref[...] = wc_ref[...].astype(BF16)

    x = x_ref[...]
    z_ref[...] = _dot(x, wcb_ref[...]) * _dot(x, wvb_ref[...])
    gb_ref[...] = _dot(x, wbb_ref[...])


def _in_odd(xb, w_in, tm=512, tn=512):
    nb = D_MODEL // tn
    wblk = lambda part: pl.BlockSpec((D_MODEL, tn), lambda j, i: (0, part * nb + j))
    return pl.pallas_call(
        _in_odd_kernel,
        grid=(nb, N_TOK // tm),
        in_specs=[pl.BlockSpec((tm, D_MODEL), lambda j, i: (i, 0)), wblk(0), wblk(1), wblk(2)],
        out_specs=[
            pl.BlockSpec((tm, tn), lambda j, i: (i, j)),
            pl.BlockSpec((tm, tn), lambda j, i: (i, j)),
        ],
        out_shape=[
            jax.ShapeDtypeStruct((N_TOK, D_MODEL), F32),
            jax.ShapeDtypeStruct((N_TOK, D_MODEL), F32),
        ],
        scratch_shapes=[pltpu.VMEM((D_MODEL, tn), BF16)] * 3,
        compiler_params=_cparams(("arbitrary", "arbitrary")),
        name="in_odd",
    )(xb, w_in, w_in, w_in)


def _conv_taps(read, cw_ref):
    conv = read(CONV_W - 1) * cw_ref[0:1, :]
    for j in range(1, CONV_W):
        conv = conv + read(CONV_W - 1 - j) * cw_ref[j:j + 1, :]
    return conv


def _conv_prompt_kernel(z_ref, gb_ref, cw_ref, g_ref, zp_ref, *, T):
    t = pl.program_id(1)

    @pl.when(t == 0)
    def _():
        zp_ref[0:CONV_HIST, :] = jnp.zeros((CONV_HIST, D_MODEL), F32)

    zp_ref[CONV_HIST:CONV_HIST + T, :] = z_ref[...]
    conv = _conv_taps(lambda back: zp_ref[CONV_HIST - back:CONV_HIST - back + T, :], cw_ref)
    g_ref[...] = (gb_ref[...] * conv).astype(g_ref.dtype)
    zp_ref[0:CONV_HIST, :] = zp_ref[T:T + CONV_HIST, :]


def _conv_prompt(z, gb, conv_w, T=256):
    nt = SEQ // T
    tile = lambda b, t: (b * nt + t, 0)
    return pl.pallas_call(
        functools.partial(_conv_prompt_kernel, T=T),
        grid=(BATCH, nt),
        in_specs=[
            pl.BlockSpec((T, D_MODEL), tile),
            pl.BlockSpec((T, D_MODEL), tile),
            _resident((CONV_W, D_MODEL)),
        ],
        out_specs=pl.BlockSpec((T, D_MODEL), tile),
        out_shape=jax.ShapeDtypeStruct((N_PROMPT, D_MODEL), BF16),
        scratch_shapes=[pltpu.VMEM((T + CONV_HIST, D_MODEL), F32)],
        compiler_params=_cparams(("arbitrary", "arbitrary")),
        name="conv_prompt",
    )(z, gb, conv_w)


def _conv_sample_kernel(z_ref, hist_ref, gb_ref, cw_ref, g_ref, zp_ref, *, S):
    zp_ref[:, 0:CONV_HIST, :] = hist_ref[...]
    zp_ref[:, CONV_HIST:, :] = z_ref[...]
    conv = _conv_taps(lambda back: zp_ref[:, CONV_HIST - back:CONV_HIST - back + DEC_SEQ, :], cw_ref)
    g_ref[...] = (gb_ref[...] * conv).reshape(S * DEC_SEQ, D_MODEL).astype(g_ref.dtype)


def _conv_sample(z3, hist, gb3, conv_w, S=32):
    blk = lambda i: (i, 0, 0)
    rows = S * DEC_SEQ
    return pl.pallas_call(
        functools.partial(_conv_sample_kernel, S=S),
        grid=(DEC_BATCH // S,),
        in_specs=[
            pl.BlockSpec((S, DEC_SEQ, D_MODEL), blk),
            pl.BlockSpec((S, CONV_HIST, D_MODEL), blk),
            pl.BlockSpec((S, DEC_SEQ, D_MODEL), blk),
            _resident((CONV_W, D_MODEL)),
        ],
        out_specs=pl.BlockSpec((rows, D_MODEL), lambda i: (i, 0)),
        out_shape=jax.ShapeDtypeStruct((N_SAMPLE, D_MODEL), BF16),
        scratch_shapes=[pltpu.VMEM((S, CONV_HIST + DEC_SEQ, D_MODEL), F32)],
        compiler_params=_cparams(("arbitrary",)),
        name="conv_sample",
    )(z3, hist, gb3, conv_w)


def _rope_tables():
    pos = jnp.concatenate([jnp.tile(jnp.arange(SEQ), BATCH),
                           jnp.tile(PAST_LEN + jnp.arange(DEC_SEQ), DEC_BATCH)])
    inv = ROPE_THETA ** (-jnp.arange(0, QK_ROPE, 2, dtype=F32) / QK_ROPE)
    ang = pos.astype(F32)[:, None] * inv[None, :]
    cos, sin = jnp.cos(ang), jnp.sin(ang)
    cos2 = jnp.concatenate([cos, cos], axis=1)
    sin2 = jnp.concatenate([-sin, sin], axis=1)
    return jnp.tile(cos2, (1, 2)), jnp.tile(sin2, (1, 2)), cos2[:N_PROMPT].T, sin2[:N_PROMPT].T


def _swap_halves(w):
    half = QK_ROPE // 2
    return jnp.concatenate([w[..., half:], w[..., :half]], axis=-1)


def kernel(x_prompt, x_sample, cache_ckv, cache_kpe, state_pool, state_conv, page_table, w_in_even, pool_w, pool_scale, q_norm_g, w_q_b, kv_norm_g, w_uk, w_uv, w_out_even, w_in_odd, conv_w, w_out_odd, ffn_w_gate, ffn_w_up, ffn_w_down, ln1_g, ln1_b, ln2_g, ln2_b):
    assert DEPTH == 2
    n_pool_pages = cache_ckv.shape[1]
    cos, sin, cost, sint = _rope_tables()
    xp = x_prompt.reshape(N_PROMPT, D_MODEL)
    xs = x_sample.reshape(N_SAMPLE, D_MODEL)

    def row(v):
        return v.reshape(1, -1)

    def ffn(layer, xf, xb, last):
        hmid = _ffn_up(xb, ffn_w_gate[layer], ffn_w_up[layer])
        return _mm_ln([hmid], ffn_w_down[layer].astype(BF16), xf, row(ln2_g[layer]), row(ln2_b[layer]),
                      tm=256, split_out=last)

    e = 0
    w_in = w_in_even[e]
    o0, o1, o2 = D_POOL, D_POOL + Q_LORA, D_POOL + Q_LORA + KV_LORA
    w_pe = w_in[:, o2:]
    u, cq, ckv_p, ckv_s, kpe_p, kpe_s, vb, kt = _in_even(
        xp, xs, w_in[:, :o0].astype(BF16), w_in[:, o0:o1].astype(BF16), w_in[:, o1:o2].astype(BF16),
        w_pe.astype(BF16), _swap_halves(w_pe).astype(BF16), row(q_norm_g[e]), row(kv_norm_g[e]),
        cos, sin, cost, sint)

    wq = w_q_b[e]
    wq_rope = wq[:, :, QK_NOPE:]
    qp, qs = _q_proj(
        cq,
        wq[:, :, :QK_NOPE].reshape(Q_LORA, N_HEADS * QK_NOPE).astype(BF16),
        wq_rope.reshape(Q_LORA, N_HEADS * QK_ROPE).astype(BF16),
        _swap_halves(wq_rope).reshape(Q_LORA, N_HEADS * QK_ROPE).astype(BF16),
        jnp.transpose(w_uk[e], (1, 2, 0)).astype(BF16), cos, sin)

    pw = pool_w[e].astype(BF16)
    ps = row(pool_scale[e])
    u_sample = u[N_PROMPT:].reshape(DEC_BATCH, DEC_SEQ, D_POOL)
    pool_hist = jnp.pad(state_pool[e], ((0, 0), (POOL_HIST - POOL_BUF, 0), (0, 0)))
    y_pool = (_pool_prompt(u, pw, ps), _pool_sample(u_sample, pool_hist, pw, ps))

    wuv = jnp.transpose(w_uv[e], (1, 0, 2)).astype(BF16)
    o_prompt = _prompt_attn(qp, kt, vb, wuv)
    o_lat_s = _decode_attn(page_table, qs, ckv_s, kpe_s,
                           cache_ckv[e].reshape(n_pool_pages, PAGE_SIZE, KV_LORA),
                           jnp.swapaxes(cache_kpe[e], 1, 2))
    o_sample = _uv_proj_sample(o_lat_s, wuv)
    xf, xb = _mm_ln([y_pool, (o_prompt, o_sample)], w_out_even[e].astype(BF16), (xp, xs),
                    row(ln1_g[0]), row(ln1_b[0]), tm=512)
    xf, xb = ffn(0, xf, xb, False)

    od = 0
    z, gb = _in_odd(xb, w_in_odd[od])
    z_sample = z[N_PROMPT:].reshape(DEC_BATCH, DEC_SEQ, D_MODEL)
    gb_sample = gb[N_PROMPT:].reshape(DEC_BATCH, DEC_SEQ, D_MODEL)
    conv_hist = jnp.pad(state_conv[od], ((0, 0), (CONV_HIST - CONV_BUF, 0), (0, 0)))
    g = (_conv_prompt(z, gb, conv_w[od]), _conv_sample(z_sample, conv_hist, gb_sample, conv_w[od]))
    xf, xb = _mm_ln([g], w_out_odd[od].astype(BF16), xf, row(ln1_g[1]), row(ln1_b[1]), tm=512)
    y_p, y_s = ffn(1, xf, xb, True)

    u_p = u[:N_PROMPT].reshape(BATCH, SEQ, D_POOL)
    z_p = z[:N_PROMPT].reshape(BATCH, SEQ, D_MODEL)
    pool_prompt = u_p[:, SEQ - POOL_BUF:]
    pool_sample = jnp.concatenate([state_pool[e][:, DEC_SEQ:], u_sample], axis=1)
    conv_prompt = z_p[:, SEQ - CONV_BUF:]
    conv_sample = z_sample[:, DEC_SEQ - CONV_BUF:]
    return (y_p.reshape(BATCH, SEQ, D_MODEL), y_s.reshape(DEC_BATCH, DEC_SEQ, D_MODEL),
            ckv_p.reshape(1, BATCH, SEQ, KV_LORA), kpe_p.reshape(1, BATCH, SEQ, QK_ROPE),
            pool_prompt[None], conv_prompt[None],
            ckv_s.reshape(1, DEC_BATCH, DEC_SEQ, KV_LORA), kpe_s.reshape(1, DEC_BATCH, DEC_SEQ, QK_ROPE),
            pool_sample[None], conv_sample[None])
```

```python
import functools

import jax
import jax.numpy as jnp
from jax import lax
from jax.experimental import pallas as pl
from jax.experimental.pallas import tpu as pltpu

F32 = jnp.float32
BF16 = jnp.bfloat16

D_MODEL = 2048
BATCH = 4
SEQ = 2048
DEPTH = 2
DEC_BATCH = 128
DEC_SEQ = 8
PAST_LEN = 16384
PAGE_SIZE = 128
N_PAGES = PAST_LEN // PAGE_SIZE
POOL_WINDOWS = (2, 4, 8, 16)
N_POOL_GROUPS = len(POOL_WINDOWS)
D_POOL = D_MODEL // 2
POOL_GROUP = D_POOL // N_POOL_GROUPS
POOL_BUF = max(POOL_WINDOWS) - 1
N_HEADS = 8
QK_NOPE = 128
QK_ROPE = 64
V_HEAD = 128
Q_LORA = 512
KV_LORA = 512
ROPE_THETA = 10000.0
ATTN_SCALE = (QK_NOPE + QK_ROPE) ** -0.5
Q_BLOCK = 128
D_MLA_OUT = N_HEADS * V_HEAD
CONV_W = 3
CONV_BUF = CONV_W - 1
D_FF = ((8 * D_MODEL + 3 * 256 - 1) // (3 * 256)) * 256
DEEPNORM_ALPHA = (2 * DEPTH) ** 0.25

N_PROMPT = BATCH * SEQ
N_SAMPLE = DEC_BATCH * DEC_SEQ
N_TOK = N_PROMPT + N_SAMPLE
D_QK = KV_LORA + QK_ROPE
POOL_HIST = 16
CONV_HIST = 8
KV_CHUNK = 512
VMEM_LIMIT_MB = 56


def _cparams(dims, vmem_mb=VMEM_LIMIT_MB):
    return pltpu.CompilerParams(dimension_semantics=dims, vmem_limit_bytes=vmem_mb * 1024 * 1024)


def _resident(shape):
    zeros = (0,) * len(shape)
    return pl.BlockSpec(shape, lambda *_: zeros, pipeline_mode=pl.Buffered(1))


def _prompt_rows(tm, width):
    last = N_PROMPT // tm - 1
    return pl.BlockSpec((tm, width), lambda i, *_: (jnp.minimum(i, last), 0))


def _sample_rows(tm, width):
    first = N_PROMPT // tm
    return pl.BlockSpec((tm, width), lambda i, *_: (jnp.maximum(i - first, 0), 0))


def _dot(a, b):
    return jnp.dot(a, b, preferred_element_type=F32)


def _dot_nt(a, b):
    return lax.dot_general(a, b, (((1,), (1,)), ((), ())), preferred_element_type=F32)


def _rms(h, g, eps=1e-6):
    return h * lax.rsqrt(jnp.mean(h * h, axis=-1, keepdims=True) + eps) * g


def _layer_norm(v, g, b, eps=1e-5):
    mu = jnp.mean(v, axis=-1, keepdims=True)
    vc = v - mu
    var = jnp.mean(vc * vc, axis=-1, keepdims=True)
    return vc * lax.rsqrt(var + eps) * g + b


def _softmax_update(m_old, l_old, acc, s, v):
    m_new = jnp.maximum(m_old, jnp.max(s, axis=-1, keepdims=True))
    alpha = jnp.exp(m_old - m_new)
    p = jnp.exp(s - m_new)
    l_new = alpha * l_old + jnp.sum(p, axis=-1, keepdims=True)
    return m_new, l_new, alpha * acc + _dot(p.astype(BF16), v)


def _in_even_kernel(xp_ref, xs_ref, wu_ref, wq_ref, wkv_ref, wpe_ref, wper_ref, wpet_ref, wpert_ref,
                    qg_ref, kvg_ref, cos_ref, sin_ref, cost_ref, sint_ref,
                    u_ref, cq_ref, ckvp_ref, ckvs_ref, kpep_ref, kpes_ref, vb_ref, kt_ref, *, tm):
    i = pl.program_id(0)
    is_prompt = i < N_PROMPT // tm
    xb = jnp.where(is_prompt, xp_ref[...], xs_ref[...]).astype(BF16)
    u_ref[...] = _dot(xb, wu_ref[...])
    cq_ref[...] = _rms(_dot(xb, wq_ref[...]), qg_ref[...]).astype(BF16)
    ckv = _rms(_dot(xb, wkv_ref[...]), kvg_ref[...])
    kpe = _dot(xb, wpe_ref[...]) * cos_ref[:, :QK_ROPE] + _dot(xb, wper_ref[...]) * sin_ref[:, :QK_ROPE]

    @pl.when(is_prompt)
    def _():
        ckvp_ref[...] = ckv
        kpep_ref[...] = kpe
        ckvb = ckv.astype(BF16)
        vb_ref[...] = ckvb
        kpet = _dot_nt(wpet_ref[...], xb) * cost_ref[...] + _dot_nt(wpert_ref[...], xb) * sint_ref[...]
        for c in range(tm // KV_CHUNK):
            cols = slice(c * KV_CHUNK, (c + 1) * KV_CHUNK)
            kt_ref[c, :KV_LORA, :] = ckv[cols, :].T.astype(BF16)
            kt_ref[c, KV_LORA:, :] = kpet[:, cols].astype(BF16)

    @pl.when(jnp.logical_not(is_prompt))
    def _():
        ckvs_ref[...] = ckv
        kpes_ref[...] = kpe


def _in_even(xp, xs, wu, wq, wkv, wpe, wper, qg, kvg, cos, sin, cost, sint, tm=512):
    row = lambda i: (i, 0)
    cpb = tm // KV_CHUNK
    last = N_PROMPT // tm - 1
    return pl.pallas_call(
        functools.partial(_in_even_kernel, tm=tm),
        grid=(N_TOK // tm,),
        in_specs=[
            _prompt_rows(tm, D_MODEL),
            _sample_rows(tm, D_MODEL),
            _resident((D_MODEL, D_POOL)),
            _resident((D_MODEL, Q_LORA)),
            _resident((D_MODEL, KV_LORA)),
            _resident((D_MODEL, QK_ROPE)),
            _resident((D_MODEL, QK_ROPE)),
            _resident((QK_ROPE, D_MODEL)),
            _resident((QK_ROPE, D_MODEL)),
            _resident((1, Q_LORA)),
            _resident((1, KV_LORA)),
            pl.BlockSpec((tm, 2 * QK_ROPE), row),
            pl.BlockSpec((tm, 2 * QK_ROPE), row),
            pl.BlockSpec((QK_ROPE, tm), lambda i: (0, jnp.minimum(i, last))),
            pl.BlockSpec((QK_ROPE, tm), lambda i: (0, jnp.minimum(i, last))),
        ],
        out_specs=[
            pl.BlockSpec((tm, D_POOL), row),
            pl.BlockSpec((tm, Q_LORA), row),
            _prompt_rows(tm, KV_LORA),
            _sample_rows(tm, KV_LORA),
            _prompt_rows(tm, QK_ROPE),
            _sample_rows(tm, QK_ROPE),
            _prompt_rows(tm, KV_LORA),
            pl.BlockSpec((cpb, D_QK, KV_CHUNK), lambda i: (jnp.minimum(i, last), 0, 0)),
        ],
        out_shape=[
            jax.ShapeDtypeStruct((N_TOK, D_POOL), F32),
            jax.ShapeDtypeStruct((N_TOK, Q_LORA), BF16),
            jax.ShapeDtypeStruct((N_PROMPT, KV_LORA), F32),
            jax.ShapeDtypeStruct((N_SAMPLE, KV_LORA), F32),
            jax.ShapeDtypeStruct((N_PROMPT, QK_ROPE), F32),
            jax.ShapeDtypeStruct((N_SAMPLE, QK_ROPE), F32),
            jax.ShapeDtypeStruct((N_PROMPT, KV_LORA), BF16),
            jax.ShapeDtypeStruct((N_PROMPT // KV_CHUNK, D_QK, KV_CHUNK), BF16),
        ],
        compiler_params=_cparams(("arbitrary",)),
        name="in_even",
    )(xp, xs, wu, wq, wkv, wpe, wper, wpe.T, wper.T, qg, kvg, cos, sin, cost, sint)


def _q_proj_kernel(cq_ref, wqn_ref, wqr_ref, wqrr_ref, wukt_ref, cos_ref, sin_ref, qp_ref, qs_ref, *, nsub):
    i = pl.program_id(0)
    is_prompt = i < N_PROMPT // (nsub * Q_BLOCK)
    cq = cq_ref[...]
    qn = _dot(cq, wqn_ref[...])
    cos = jnp.concatenate([cos_ref[...]] * (N_HEADS // 2), axis=1)
    sin = jnp.concatenate([sin_ref[...]] * (N_HEADS // 2), axis=1)
    qpe = (_dot(cq, wqr_ref[...]) * cos + _dot(cq, wqrr_ref[...]) * sin) * ATTN_SCALE

    def emit(q_ref):
        for h in range(N_HEADS):
            ql = _dot(qn[:, h * QK_NOPE:(h + 1) * QK_NOPE].astype(BF16), wukt_ref[h]) * ATTN_SCALE
            for j in range(nsub):
                rows = slice(j * Q_BLOCK, (j + 1) * Q_BLOCK)
                q_ref[j, h, :, :KV_LORA] = ql[rows].astype(q_ref.dtype)
                q_ref[j, h, :, KV_LORA:] = qpe[rows, h * QK_ROPE:(h + 1) * QK_ROPE].astype(q_ref.dtype)

    @pl.when(is_prompt)
    def _():
        emit(qp_ref)

    @pl.when(jnp.logical_not(is_prompt))
    def _():
        emit(qs_ref)


def _q_proj(cq, wqn, wqr, wqrr, wukt, cos, sin, tm=256):
    nsub = tm // Q_BLOCK
    nbp = N_PROMPT // tm
    row = lambda i: (i, 0)
    blk = (nsub, N_HEADS, Q_BLOCK, D_QK)
    return pl.pallas_call(
        functools.partial(_q_proj_kernel, nsub=nsub),
        grid=(N_TOK // tm,),
        in_specs=[
            pl.BlockSpec((tm, Q_LORA), row),
            _resident((Q_LORA, N_HEADS * QK_NOPE)),
            _resident((Q_LORA, N_HEADS * QK_ROPE)),
            _resident((Q_LORA, N_HEADS * QK_ROPE)),
            _resident((N_HEADS, QK_NOPE, KV_LORA)),
            pl.BlockSpec((tm, 2 * QK_ROPE), row),
            pl.BlockSpec((tm, 2 * QK_ROPE), row),
        ],
        out_specs=[
            pl.BlockSpec(blk, lambda i: (jnp.minimum(i, nbp - 1), 0, 0, 0)),
            pl.BlockSpec(blk, lambda i: (jnp.maximum(i - nbp, 0), 0, 0, 0)),
        ],
        out_shape=[
            jax.ShapeDtypeStruct((N_PROMPT // Q_BLOCK, N_HEADS, Q_BLOCK, D_QK), BF16),
            jax.ShapeDtypeStruct((N_SAMPLE // Q_BLOCK, N_HEADS, Q_BLOCK, D_QK), F32),
        ],
        compiler_params=_cparams(("arbitrary",)),
        name="q_proj",
    )(cq, wqn, wqr, wqrr, wukt, cos, sin)


def _pool_groups(read, pos, w_ref, scale_ref, write):
    for g, w in enumerate(POOL_WINDOWS):
        lanes = slice(g * POOL_GROUP, (g + 1) * POOL_GROUP)
        cur = read(0, lanes)
        win = cur
        for j in range(1, w):
            win = win + read(j, lanes)
        cnt = jnp.minimum(pos + 1, w).astype(F32)
        diff = (win / cnt - cur).astype(BF16)
        write(lanes, _dot(diff, w_ref[g]) * scale_ref[:, lanes])


def _pool_prompt_kernel(u_ref, w_ref, scale_ref, y_ref, xp_ref, *, T):
    t = pl.program_id(1)

    @pl.when(t == 0)
    def _():
        xp_ref[0:POOL_HIST, :] = jnp.zeros((POOL_HIST, D_POOL), F32)

    xp_ref[POOL_HIST:POOL_HIST + T, :] = u_ref[...]
    pos = t * T + lax.broadcasted_iota(jnp.int32, (T, 1), 0)

    def read(j, lanes):
        return xp_ref[POOL_HIST - j:POOL_HIST - j + T, lanes]

    def write(lanes, y):
        y_ref[:, lanes] = y.astype(y_ref.dtype)

    _pool_groups(read, pos, w_ref, scale_ref, write)
    xp_ref[0:POOL_HIST, :] = xp_ref[T:T + POOL_HIST, :]


def _pool_prompt(u, pool_w, pool_scale, T=512):
    nt = SEQ // T
    return pl.pallas_call(
        functools.partial(_pool_prompt_kernel, T=T),
        grid=(BATCH, nt),
        in_specs=[
            pl.BlockSpec((T, D_POOL), lambda b, t: (b * nt + t, 0)),
            _resident((N_POOL_GROUPS, POOL_GROUP, POOL_GROUP)),
            _resident((1, D_POOL)),
        ],
        out_specs=pl.BlockSpec((T, D_POOL), lambda b, t: (b * nt + t, 0)),
        out_shape=jax.ShapeDtypeStruct((N_PROMPT, D_POOL), BF16),
        scratch_shapes=[pltpu.VMEM((T + POOL_HIST, D_POOL), F32)],
        compiler_params=_cparams(("arbitrary", "arbitrary")),
        name="pool_prompt",
    )(u, pool_w, pool_scale)


def _pool_sample_kernel(u_ref, hist_ref, w_ref, scale_ref, y_ref, xp_ref, *, S):
    xp_ref[:, 0:POOL_HIST, :] = hist_ref[...]
    xp_ref[:, POOL_HIST:, :] = u_ref[...]
    pos = PAST_LEN + lax.broadcasted_iota(jnp.int32, (S, DEC_SEQ, 1), 1)
    pos = pos.reshape(S * DEC_SEQ, 1)

    def read(j, lanes):
        return xp_ref[:, POOL_HIST - j:POOL_HIST - j + DEC_SEQ, lanes].reshape(S * DEC_SEQ, POOL_GROUP)

    def write(lanes, y):
        y_ref[:, lanes] = y.astype(y_ref.dtype)

    _pool_groups(read, pos, w_ref, scale_ref, write)


def _pool_sample(u3, hist, pool_w, pool_scale, S=32):
    rows = S * DEC_SEQ
    return pl.pallas_call(
        functools.partial(_pool_sample_kernel, S=S),
        grid=(DEC_BATCH // S,),
        in_specs=[
            pl.BlockSpec((S, DEC_SEQ, D_POOL), lambda i: (i, 0, 0)),
            pl.BlockSpec((S, POOL_HIST, D_POOL), lambda i: (i, 0, 0)),
            _resident((N_POOL_GROUPS, POOL_GROUP, POOL_GROUP)),
            _resident((1, D_POOL)),
        ],
        out_specs=pl.BlockSpec((rows, D_POOL), lambda i: (i, 0)),
        out_shape=jax.ShapeDtypeStruct((N_SAMPLE, D_POOL), BF16),
        scratch_shapes=[pltpu.VMEM((S, POOL_HIST + DEC_SEQ, D_POOL), F32)],
        compiler_params=_cparams(("arbitrary",)),
        name="pool_sample",
    )(u3, hist, pool_w, pool_scale)


def _prompt_attn_kernel(q_ref, kt_ref, v_ref, wuv_ref, o_ref, m_ref, l_ref, acc_ref, sa_ref, sb_ref):
    i = pl.program_id(1)
    rows_all = N_HEADS * Q_BLOCK
    m_ref[...] = jnp.full((rows_all, 1), -jnp.inf, F32)
    l_ref[...] = jnp.zeros((rows_all, 1), F32)
    acc_ref[...] = jnp.zeros((rows_all, KV_LORA), F32)
    q = q_ref[0].reshape(rows_all, D_QK)
    q_pos = i * Q_BLOCK + (lax.broadcasted_iota(jnp.int32, (rows_all, KV_CHUNK), 0) & (Q_BLOCK - 1))
    key = lax.broadcasted_iota(jnp.int32, (rows_all, KV_CHUNK), 1)

    def scores(j, s_ref):
        s = _dot(q, kt_ref[j])
        s_ref[...] = jnp.where(j * KV_CHUNK + key <= q_pos, s, -jnp.inf)

    def attend(j, s_ref):
        v = v_ref[pl.ds(pl.multiple_of(j * KV_CHUNK, KV_CHUNK), KV_CHUNK), :]
        m_new, l_new, acc_new = _softmax_update(m_ref[...], l_ref[...], acc_ref[...], s_ref[...], v)
        m_ref[...] = m_new
        l_ref[...] = l_new
        acc_ref[...] = acc_new

    n_chunks = (i * Q_BLOCK + Q_BLOCK + KV_CHUNK - 1) // KV_CHUNK
    n_pairs = (n_chunks - 1) // 2
    scores(0, sa_ref)

    def body(t, c):
        scores(2 * t + 1, sb_ref)
        attend(2 * t, sa_ref)
        scores(2 * t + 2, sa_ref)
        attend(2 * t + 1, sb_ref)
        return c

    lax.fori_loop(0, n_pairs, body, 0)
    done = 2 * n_pairs

    @pl.when(done + 2 == n_chunks)
    def _():
        scores(done + 1, sb_ref)
        attend(done, sa_ref)
        attend(done + 1, sb_ref)

    @pl.when(done + 1 == n_chunks)
    def _():
        attend(done, sa_ref)

    for h in range(N_HEADS):
        rows = slice(h * Q_BLOCK, (h + 1) * Q_BLOCK)
        o_lat = (acc_ref[rows] * (1.0 / l_ref[rows])).astype(BF16)
        o_ref[:, h * V_HEAD:(h + 1) * V_HEAD] = _dot(o_lat, wuv_ref[h]).astype(o_ref.dtype)


def _prompt_attn(qp, kt, vb, wuv):
    nqb = SEQ // Q_BLOCK
    cps = SEQ // KV_CHUNK
    rows = N_HEADS * Q_BLOCK
    return pl.pallas_call(
        _prompt_attn_kernel,
        grid=(BATCH, nqb),
        in_specs=[
            pl.BlockSpec((1, N_HEADS, Q_BLOCK, D_QK), lambda b, i: (b * nqb + i, 0, 0, 0)),
            pl.BlockSpec((cps, D_QK, KV_CHUNK), lambda b, i: (b, 0, 0)),
            pl.BlockSpec((SEQ, KV_LORA), lambda b, i: (b, 0)),
            _resident((N_HEADS, KV_LORA, V_HEAD)),
        ],
        out_specs=pl.BlockSpec((Q_BLOCK, D_MLA_OUT), lambda b, i: (b * nqb + i, 0)),
        out_shape=jax.ShapeDtypeStruct((N_PROMPT, D_MLA_OUT), BF16),
        scratch_shapes=[
            pltpu.VMEM((rows, 1), F32),
            pltpu.VMEM((rows, 1), F32),
            pltpu.VMEM((rows, KV_LORA), F32),
            pltpu.VMEM((rows, KV_CHUNK), F32),
            pltpu.VMEM((rows, KV_CHUNK), F32),
        ],
        compiler_params=_cparams(("arbitrary", "arbitrary")),
        name="prompt_attn",
    )(qp, kt, vb, wuv)


PAGES_PER_CHUNK = 16
CHUNK_KEYS = PAGES_PER_CHUNK * PAGE_SIZE
CHUNKS_PER_SEQ = N_PAGES // PAGES_PER_CHUNK
DEC_SLOTS = 4
DEC_AHEAD = DEC_SLOTS - 1
NEW_KEYS_PAD = 16
assert CHUNKS_PER_SEQ % DEC_SLOTS == 0


def _decode_attn_kernel(pt_ref, q_ref, ckvn_ref, kpen_ref, cckv_hbm, ckpet_hbm, o_ref,
                        ckv_buf, kpet_buf, kvb_a, kvb_b, sc_a, sc_b, sem):
    s = pl.program_id(0)
    nseq = pl.num_programs(0)
    rows = N_HEADS * DEC_SEQ

    def page_copies(seq, c, p):
        slot = c % DEC_SLOTS
        page = pt_ref[(seq * CHUNKS_PER_SEQ + c) * PAGES_PER_CHUNK + p]
        keys = pl.ds(p * PAGE_SIZE, PAGE_SIZE)
        return (pltpu.make_async_copy(cckv_hbm.at[page], ckv_buf.at[slot, keys, :], sem.at[0, slot]),
                pltpu.make_async_copy(ckpet_hbm.at[page], kpet_buf.at[slot, :, keys], sem.at[1, slot]))

    def start_chunk(seq, c):
        for p in range(PAGES_PER_CHUNK):
            for cp in page_copies(seq, c, p):
                cp.start()

    def wait_chunk(seq, c):
        for p in range(PAGES_PER_CHUNK):
            for cp in page_copies(seq, c, p):
                cp.wait()

    @pl.when(s == 0)
    def _():
        for c in range(DEC_AHEAD):
            start_chunk(0, c)

    q = q_ref[0].reshape(rows, D_QK)
    q_lat = q[:, :KV_LORA].astype(BF16)
    q_pe = q[:, KV_LORA:].astype(BF16)

    def scores(c, kvb_ref, sc_ref):
        nxt = c + DEC_AHEAD
        if nxt < CHUNKS_PER_SEQ:
            start_chunk(s, nxt)
        else:
            @pl.when(s + 1 < nseq)
            def _():
                start_chunk(s + 1, nxt - CHUNKS_PER_SEQ)
        wait_chunk(s, c)
        slot = c % DEC_SLOTS
        kvb_ref[...] = ckv_buf[slot].astype(BF16)
        sc_ref[...] = _dot_nt(q_lat, kvb_ref[...]) + _dot(q_pe, kpet_buf[slot].astype(BF16))

    def attend(carry, kvb_ref, sc_ref):
        return _softmax_update(*carry, sc_ref[...], kvb_ref[...])

    bufs = ((kvb_a, sc_a), (kvb_b, sc_b))
    carry = (jnp.full((rows, 1), -jnp.inf, F32), jnp.zeros((rows, 1), F32), jnp.zeros((rows, KV_LORA), F32))
    scores(0, *bufs[0])
    for c in range(1, CHUNKS_PER_SEQ):
        scores(c, *bufs[c % 2])
        carry = attend(carry, *bufs[(c - 1) % 2])
    carry = attend(carry, *bufs[(CHUNKS_PER_SEQ - 1) % 2])

    pad = NEW_KEYS_PAD - DEC_SEQ
    kvn = jnp.concatenate([ckvn_ref[...], jnp.zeros((pad, KV_LORA), F32)], axis=0).astype(BF16)
    kpn = jnp.concatenate([kpen_ref[...], jnp.zeros((pad, QK_ROPE), F32)], axis=0).astype(BF16)
    sc = _dot_nt(q_lat, kvn) + _dot_nt(q_pe, kpn)
    tok = lax.broadcasted_iota(jnp.int32, (rows, NEW_KEYS_PAD), 0) & (DEC_SEQ - 1)
    key = lax.broadcasted_iota(jnp.int32, (rows, NEW_KEYS_PAD), 1)
    sc = jnp.where(key <= tok, sc, -jnp.inf)
    _, l_fin, acc = _softmax_update(*carry, sc, kvn)
    o_ref[0] = (acc * (1.0 / l_fin)).reshape(N_HEADS, DEC_SEQ, KV_LORA)


def _decode_attn(page_table, qs, ckv_s, kpe_s, cache_ckv, cache_kpet):
    seq_per_blk = Q_BLOCK // DEC_SEQ
    grid_spec = pltpu.PrefetchScalarGridSpec(
        num_scalar_prefetch=1,
        grid=(DEC_BATCH,),
        in_specs=[
            pl.BlockSpec((1, N_HEADS, DEC_SEQ, D_QK), lambda s, pt: (s // seq_per_blk, 0, s % seq_per_blk, 0)),
            pl.BlockSpec((DEC_SEQ, KV_LORA), lambda s, pt: (s, 0)),
            pl.BlockSpec((DEC_SEQ, QK_ROPE), lambda s, pt: (s, 0)),
            pl.BlockSpec(memory_space=pl.ANY),
            pl.BlockSpec(memory_space=pl.ANY),
        ],
        out_specs=pl.BlockSpec((1, N_HEADS, DEC_SEQ, KV_LORA),
                               lambda s, pt: (s // seq_per_blk, 0, s % seq_per_blk, 0)),
        scratch_shapes=[
            pltpu.VMEM((DEC_SLOTS, CHUNK_KEYS, KV_LORA), F32),
            pltpu.VMEM((DEC_SLOTS, QK_ROPE, CHUNK_KEYS), F32),
            pltpu.VMEM((CHUNK_KEYS, KV_LORA), BF16),
            pltpu.VMEM((CHUNK_KEYS, KV_LORA), BF16),
            pltpu.VMEM((N_HEADS * DEC_SEQ, CHUNK_KEYS), F32),
            pltpu.VMEM((N_HEADS * DEC_SEQ, CHUNK_KEYS), F32),
            pltpu.SemaphoreType.DMA((2, DEC_SLOTS)),
        ],
    )
    return pl.pallas_call(
        _decode_attn_kernel,
        grid_spec=grid_spec,
        out_shape=jax.ShapeDtypeStruct((N_SAMPLE // Q_BLOCK, N_HEADS, Q_BLOCK, KV_LORA), F32),
        compiler_params=_cparams(("arbitrary",)),
        name="decode_attn",
    )(page_table.reshape(-1), qs, ckv_s, kpe_s, cache_ckv, cache_kpet)


def _uv_proj_kernel(ol_ref, wuv_ref, o_ref, *, nsub):
    for j in range(nsub):
        for h in range(N_HEADS):
            o = _dot(ol_ref[j, h].astype(BF16), wuv_ref[h])
            o_ref[j * Q_BLOCK:(j + 1) * Q_BLOCK, h * V_HEAD:(h + 1) * V_HEAD] = o.astype(o_ref.dtype)


def _uv_proj_sample(o_lat, wuv, nsub=4):
    nb = o_lat.shape[0]
    rows = nsub * Q_BLOCK
    return pl.pallas_call(
        functools.partial(_uv_proj_kernel, nsub=nsub),
        grid=(nb // nsub,),
        in_specs=[
            pl.BlockSpec((nsub, N_HEADS, Q_BLOCK, KV_LORA), lambda i: (i, 0, 0, 0)),
            _resident((N_HEADS, KV_LORA, V_HEAD)),
        ],
        out_specs=pl.BlockSpec((rows, D_MLA_OUT), lambda i: (i, 0)),
        out_shape=jax.ShapeDtypeStruct((nb * Q_BLOCK, D_MLA_OUT), BF16),
        compiler_params=_cparams(("arbitrary",)),
        name="uv_proj_sample",
    )(o_lat, wuv)


def _mm_ln_kernel(*refs, lhs_split, split_resid, split_out, tm):
    is_prompt = pl.program_id(0) < N_PROMPT // tm
    refs = list(refs)

    def take(split):
        if split:
            p_ref, s_ref = refs.pop(0), refs.pop(0)
            return jnp.where(is_prompt, p_ref[...], s_ref[...])
        return refs.pop(0)[...]

    lhs = [take(split) for split in lhs_split]
    w_ref = refs.pop(0)
    resid = take(split_resid)
    g_ref, b_ref = refs.pop(0), refs.pop(0)
    out_refs = refs

    acc = None
    off = 0
    for piece in lhs:
        k = piece.shape[1]
        part = _dot(piece, w_ref[off:off + k, :])
        acc = part if acc is None else acc + part
        off += k
    y = _layer_norm(DEEPNORM_ALPHA * resid + acc, g_ref[...], b_ref[...])
    if split_out:
        @pl.when(is_prompt)
        def _():
            out_refs[0][...] = y

        @pl.when(jnp.logical_not(is_prompt))
        def _():
            out_refs[1][...] = y
    else:
        out_refs[0][...] = y
        out_refs[1][...] = y.astype(BF16)


def _mm_ln(lhs_list, w, resid, g, b, *, tm, split_out=False, layer=None):
    operands, in_specs = [], []

    def add_rows(v):
        if isinstance(v, tuple):
            operands.extend(v)
            in_specs.extend([_prompt_rows(tm, v[0].shape[1]), _sample_rows(tm, v[0].shape[1])])
        else:
            operands.append(v)
            in_specs.append(pl.BlockSpec((tm, v.shape[1]), lambda i: (i, 0)))
        return isinstance(v, tuple)

    lhs_split = tuple(add_rows(v) for v in lhs_list)
    operands.append(w)
    if layer is None:
        in_specs.append(_resident(w.shape))
    else:
        in_specs.append(pl.BlockSpec((None,) + w.shape[1:], lambda i: (layer, 0, 0), pipeline_mode=pl.Buffered(1)))
    split_resid = add_rows(resid)
    operands.extend([g, b])
    in_specs.extend([_resident((1, D_MODEL)), _resident((1, D_MODEL))])
    if split_out:
        out_specs = [_prompt_rows(tm, D_MODEL), _sample_rows(tm, D_MODEL)]
        out_shape = [jax.ShapeDtypeStruct((N_PROMPT, D_MODEL), F32), jax.ShapeDtypeStruct((N_SAMPLE, D_MODEL), F32)]
    else:
        row = pl.BlockSpec((tm, D_MODEL), lambda i: (i, 0))
        out_specs = [row, row]
        out_shape = [jax.ShapeDtypeStruct((N_TOK, D_MODEL), F32), jax.ShapeDtypeStruct((N_TOK, D_MODEL), BF16)]
    return pl.pallas_call(
        functools.partial(_mm_ln_kernel, lhs_split=lhs_split, split_resid=split_resid, split_out=split_out, tm=tm),
        grid=(N_TOK // tm,),
        in_specs=in_specs,
        out_specs=out_specs,
        out_shape=out_shape,
        compiler_params=_cparams(("arbitrary",)),
        name="mm_ln",
    )(*operands)


def _ffn_up_kernel(x_ref, wg_ref, wu_ref, h_ref, wgb_ref, wub_ref):
    @pl.when(pl.program_id(1) == 0)
    def _():
        wgb_ref[...] = wg_ref[...].astype(BF16)
        wub_ref[...] = wu_ref[...].astype(BF16)

    x = x_ref[...]
    h_ref[...] = (jax.nn.silu(_dot(x, wgb_ref[...])) * _dot(x, wub_ref[...])).astype(h_ref.dtype)


def _ffn_up(xb, wg, wu, layer, tm=1024, tn=512):
    wblk = pl.BlockSpec((None, D_MODEL, tn), lambda j, i: (layer, 0, j))
    return pl.pallas_call(
        _ffn_up_kernel,
        grid=(D_FF // tn, N_TOK // tm),
        in_specs=[pl.BlockSpec((tm, D_MODEL), lambda j, i: (i, 0)), wblk, wblk],
        out_specs=pl.BlockSpec((tm, tn), lambda j, i: (i, j)),
        out_shape=jax.ShapeDtypeStruct((N_TOK, D_FF), BF16),
        scratch_shapes=[pltpu.VMEM((D_MODEL, tn), BF16), pltpu.VMEM((D_MODEL, tn), BF16)],
        compiler_params=_cparams(("arbitrary", "arbitrary")),
        name="ffn_up",
    )(xb, wg, wu)


def _conv_taps(read, cw_ref):
    conv = read(CONV_W - 1) * cw_ref[0:1, :]
    for j in range(1, CONV_W):
        conv = conv + read(CONV_W - 1 - j) * cw_ref[j:j + 1, :]
    return conv


def _in_odd_kernel(x_ref, wv_ref, wb_ref, wc_ref, cw_ref, g_ref, ztail_ref, zs_ref, gbs_ref,
                   wvb_ref, wbb_ref, wcb_ref, zp_ref, *, tm):
    i = pl.program_id(1)
    tiles_per_seq = SEQ // tm
    is_prompt = i < N_PROMPT // tm

    @pl.when(i == 0)
    def _():
        wvb_ref[...] = wv_ref[...].astype(BF16)
        wbb_ref[...] = wb_ref[...].astype(BF16)
        wcb_ref[...] = wc_ref[...].astype(BF16)

    x = x_ref[...]
    z = _dot(x, wcb_ref[...]) * _dot(x, wvb_ref[...])
    gb = _dot(x, wbb_ref[...])

    @pl.when(is_prompt)
    def _():
        @pl.when(i % tiles_per_seq == 0)
        def _():
            zp_ref[0:CONV_HIST, :] = jnp.zeros((CONV_HIST, zp_ref.shape[1]), F32)

        zp_ref[CONV_HIST:CONV_HIST + tm, :] = z
        conv = _conv_taps(lambda back: zp_ref[CONV_HIST - back:CONV_HIST - back + tm, :], cw_ref)
        g_ref[...] = (gb * conv).astype(g_ref.dtype)
        zp_ref[0:CONV_HIST, :] = zp_ref[tm:tm + CONV_HIST, :]

        @pl.when(i % tiles_per_seq == tiles_per_seq - 1)
        def _():
            ztail_ref[...] = zp_ref[tm:tm + CONV_HIST, :]

    @pl.when(jnp.logical_not(is_prompt))
    def _():
        zs_ref[...] = z
        gbs_ref[...] = gb


def _in_odd(xb, w_in, conv_w, tm=512, tn=512):
    nb = D_MODEL // tn
    nbp = N_PROMPT // tm
    tiles_per_seq = SEQ // tm
    wblk = lambda part: pl.BlockSpec((D_MODEL, tn), lambda j, i: (0, part * nb + j))
    sample_blk = pl.BlockSpec((tm, tn), lambda j, i: (jnp.maximum(i - nbp, 0), j))
    return pl.pallas_call(
        functools.partial(_in_odd_kernel, tm=tm),
        grid=(nb, N_TOK // tm),
        in_specs=[pl.BlockSpec((tm, D_MODEL), lambda j, i: (i, 0)), wblk(0), wblk(1), wblk(2),
                  pl.BlockSpec((CONV_W, tn), lambda j, i: (0, j))],
        out_specs=[
            pl.BlockSpec((tm, tn), lambda j, i: (jnp.minimum(i, nbp - 1), j)),
            pl.BlockSpec((CONV_HIST, tn), lambda j, i: (jnp.minimum(i, nbp - 1) // tiles_per_seq, j)),
            sample_blk,
            sample_blk,
        ],
        out_shape=[
            jax.ShapeDtypeStruct((N_PROMPT, D_MODEL), BF16),
            jax.ShapeDtypeStruct((BATCH * CONV_HIST, D_MODEL), F32),
            jax.ShapeDtypeStruct((N_SAMPLE, D_MODEL), F32),
            jax.ShapeDtypeStruct((N_SAMPLE, D_MODEL), F32),
        ],
        scratch_shapes=[pltpu.VMEM((D_MODEL, tn), BF16)] * 3 + [pltpu.VMEM((tm + CONV_HIST, tn), F32)],
        compiler_params=_cparams(("arbitrary", "arbitrary")),
        name="in_odd",
    )(xb, w_in, w_in, w_in, conv_w)


def _conv_sample_kernel(z_ref, hist_ref, gb_ref, cw_ref, g_ref, zp_ref, *, S):
    zp_ref[:, 0:CONV_HIST, :] = hist_ref[...]
    zp_ref[:, CONV_HIST:, :] = z_ref[...]
    conv = _conv_taps(lambda back: zp_ref[:, CONV_HIST - back:CONV_HIST - back + DEC_SEQ, :], cw_ref)
    g_ref[...] = (gb_ref[...] * conv).reshape(S * DEC_SEQ, D_MODEL).astype(g_ref.dtype)


def _conv_sample(z3, hist, gb3, conv_w, S=32):
    blk = lambda i: (i, 0, 0)
    rows = S * DEC_SEQ
    return pl.pallas_call(
        functools.partial(_conv_sample_kernel, S=S),
        grid=(DEC_BATCH // S,),
        in_specs=[
            pl.BlockSpec((S, DEC_SEQ, D_MODEL), blk),
            pl.BlockSpec((S, CONV_HIST, D_MODEL), blk),
            pl.BlockSpec((S, DEC_SEQ, D_MODEL), blk),
            _resident((CONV_W, D_MODEL)),
        ],
        out_specs=pl.BlockSpec((rows, D_MODEL), lambda i: (i, 0)),
        out_shape=jax.ShapeDtypeStruct((N_SAMPLE, D_MODEL), BF16),
        scratch_shapes=[pltpu.VMEM((S, CONV_HIST + DEC_SEQ, D_MODEL), F32)],
        compiler_params=_cparams(("arbitrary",)),
        name="conv_sample",
    )(z3, hist, gb3, conv_w)


def _rope_tables():
    pos = jnp.concatenate([jnp.tile(jnp.arange(SEQ), BATCH),
                           jnp.tile(PAST_LEN + jnp.arange(DEC_SEQ), DEC_BATCH)])
    inv = ROPE_THETA ** (-jnp.arange(0, QK_ROPE, 2, dtype=F32) / QK_ROPE)
    ang = pos.astype(F32)[:, None] * inv[None, :]
    cos, sin = jnp.cos(ang), jnp.sin(ang)
    cos2 = jnp.concatenate([cos, cos], axis=1)
    sin2 = jnp.concatenate([-sin, sin], axis=1)
    return jnp.tile(cos2, (1, 2)), jnp.tile(sin2, (1, 2)), cos2[:N_PROMPT].T, sin2[:N_PROMPT].T


def _swap_halves(w):
    half = QK_ROPE // 2
    return jnp.concatenate([w[..., half:], w[..., :half]], axis=-1)


def kernel(x_prompt, x_sample, cache_ckv, cache_kpe, state_pool, state_conv, page_table, w_in_even, pool_w, pool_scale, q_norm_g, w_q_b, kv_norm_g, w_uk, w_uv, w_out_even, w_in_odd, conv_w, w_out_odd, ffn_w_gate, ffn_w_up, ffn_w_down, ln1_g, ln1_b, ln2_g, ln2_b):
    assert DEPTH == 2
    n_pool_pages = cache_ckv.shape[1]
    cos, sin, cost, sint = _rope_tables()
    xp = x_prompt.reshape(N_PROMPT, D_MODEL)
    xs = x_sample.reshape(N_SAMPLE, D_MODEL)

    def row(v):
        return v.reshape(1, -1)

    w_down = ffn_w_down.astype(BF16)

    def ffn(layer, xf, xb, last):
        hmid = _ffn_up(xb, ffn_w_gate, ffn_w_up, layer)
        return _mm_ln([hmid], w_down, xf, row(ln2_g[layer]), row(ln2_b[layer]),
                      tm=256, split_out=last, layer=layer)

    e = 0
    w_in = w_in_even[e]
    o0, o1, o2 = D_POOL, D_POOL + Q_LORA, D_POOL + Q_LORA + KV_LORA
    w_pe = w_in[:, o2:]
    u, cq, ckv_p, ckv_s, kpe_p, kpe_s, vb, kt = _in_even(
        xp, xs, w_in[:, :o0].astype(BF16), w_in[:, o0:o1].astype(BF16), w_in[:, o1:o2].astype(BF16),
        w_pe.astype(BF16), _swap_halves(w_pe).astype(BF16), row(q_norm_g[e]), row(kv_norm_g[e]),
        cos, sin, cost, sint)

    wq = w_q_b[e]
    wq_rope = wq[:, :, QK_NOPE:]
    qp, qs = _q_proj(
        cq,
        wq[:, :, :QK_NOPE].reshape(Q_LORA, N_HEADS * QK_NOPE).astype(BF16),
        wq_rope.reshape(Q_LORA, N_HEADS * QK_ROPE).astype(BF16),
        _swap_halves(wq_rope).reshape(Q_LORA, N_HEADS * QK_ROPE).astype(BF16),
        jnp.transpose(w_uk[e], (1, 2, 0)).astype(BF16), cos, sin)

    pw = pool_w[e].astype(BF16)
    ps = row(pool_scale[e])
    u_sample = u[N_PROMPT:].reshape(DEC_BATCH, DEC_SEQ, D_POOL)
    pool_hist = jnp.pad(state_pool[e], ((0, 0), (POOL_HIST - POOL_BUF, 0), (0, 0)))
    y_pool = (_pool_prompt(u, pw, ps), _pool_sample(u_sample, pool_hist, pw, ps))

    wuv = jnp.transpose(w_uv[e], (1, 0, 2)).astype(BF16)
    o_prompt = _prompt_attn(qp, kt, vb, wuv)
    o_lat_s = _decode_attn(page_table, qs, ckv_s, kpe_s,
                           cache_ckv[e].reshape(n_pool_pages, PAGE_SIZE, KV_LORA),
                           jnp.swapaxes(cache_kpe[e], 1, 2))
    o_sample = _uv_proj_sample(o_lat_s, wuv)
    xf, xb = _mm_ln([y_pool, (o_prompt, o_sample)], w_out_even[e].astype(BF16), (xp, xs),
                    row(ln1_g[0]), row(ln1_b[0]), tm=512)
    xf, xb = ffn(0, xf, xb, False)

    od = 0
    g_prompt, z_tail, z_s, gb_s = _in_odd(xb, w_in_odd[od], conv_w[od])
    z_sample = z_s.reshape(DEC_BATCH, DEC_SEQ, D_MODEL)
    gb_sample = gb_s.reshape(DEC_BATCH, DEC_SEQ, D_MODEL)
    conv_hist = jnp.pad(state_conv[od], ((0, 0), (CONV_HIST - CONV_BUF, 0), (0, 0)))
    g = (g_prompt, _conv_sample(z_sample, conv_hist, gb_sample, conv_w[od]))
    xf, xb = _mm_ln([g], w_out_odd[od].astype(BF16), xf, row(ln1_g[1]), row(ln1_b[1]), tm=512)
    y_p, y_s = ffn(1, xf, xb, True)

    pool_prompt = jnp.stack([u[(b + 1) * SEQ - POOL_BUF:(b + 1) * SEQ] for b in range(BATCH)])
    pool_sample = jnp.concatenate([state_pool[e][:, DEC_SEQ:], u_sample], axis=1)
    conv_prompt = z_tail.reshape(BATCH, CONV_HIST, D_MODEL)[:, CONV_HIST - CONV_BUF:]
    conv_sample = z_sample[:, DEC_SEQ - CONV_BUF:]
    return (y_p.reshape(BATCH, SEQ, D_MODEL), y_s.reshape(DEC_BATCH, DEC_SEQ, D_MODEL),
            ckv_p.reshape(1, BATCH, SEQ, KV_LORA), kpe_p.reshape(1, BATCH, SEQ, QK_ROPE),
            pool_prompt[None], conv_prompt[None],
            ckv_s.reshape(1, DEC_BATCH, DEC_SEQ, KV_LORA), kpe_s.reshape(1, DEC_BATCH, DEC_SEQ, QK_ROPE),
            pool_sample[None], conv_sample[None])
```

```python
import functools

import jax
import jax.numpy as jnp
from jax import lax
from jax.experimental import pallas as pl
from jax.experimental.pallas import tpu as pltpu

F32 = jnp.float32
BF16 = jnp.bfloat16

D_MODEL = 2048
BATCH = 4
SEQ = 2048
DEPTH = 2
DEC_BATCH = 128
DEC_SEQ = 8
PAST_LEN = 16384
PAGE_SIZE = 128
N_PAGES = PAST_LEN // PAGE_SIZE
POOL_WINDOWS = (2, 4, 8, 16)
N_POOL_GROUPS = len(POOL_WINDOWS)
D_POOL = D_MODEL // 2
POOL_GROUP = D_POOL // N_POOL_GROUPS
POOL_BUF = max(POOL_WINDOWS) - 1
N_HEADS = 8
QK_NOPE = 128
QK_ROPE = 64
V_HEAD = 128
Q_LORA = 512
KV_LORA = 512
ROPE_THETA = 10000.0
ATTN_SCALE = (QK_NOPE + QK_ROPE) ** -0.5
Q_BLOCK = 128
D_MLA_OUT = N_HEADS * V_HEAD
CONV_W = 3
CONV_BUF = CONV_W - 1
D_FF = ((8 * D_MODEL + 3 * 256 - 1) // (3 * 256)) * 256
DEEPNORM_ALPHA = (2 * DEPTH) ** 0.25

N_PROMPT = BATCH * SEQ
N_SAMPLE = DEC_BATCH * DEC_SEQ
N_TOK = N_PROMPT + N_SAMPLE
D_QK = KV_LORA + QK_ROPE
POOL_HIST = 16
CONV_HIST = 8
KV_CHUNK = 512
FFN_SUB_ROWS = 256
VMEM_LIMIT_MB = 56


def _cparams(dims, vmem_mb=VMEM_LIMIT_MB):
    return pltpu.CompilerParams(dimension_semantics=dims, vmem_limit_bytes=vmem_mb * 1024 * 1024)


def _resident(shape):
    zeros = (0,) * len(shape)
    return pl.BlockSpec(shape, lambda *_: zeros, pipeline_mode=pl.Buffered(1))


def _prompt_rows(tm, width):
    last = N_PROMPT // tm - 1
    return pl.BlockSpec((tm, width), lambda i, *_: (jnp.minimum(i, last), 0))


def _sample_rows(tm, width):
    first = N_PROMPT // tm
    return pl.BlockSpec((tm, width), lambda i, *_: (jnp.maximum(i - first, 0), 0))


def _dot(a, b):
    return jnp.dot(a, b, preferred_element_type=F32)


def _dot_nt(a, b):
    return lax.dot_general(a, b, (((1,), (1,)), ((), ())), preferred_element_type=F32)


def _rms(h, g, eps=1e-6):
    return h * lax.rsqrt(jnp.mean(h * h, axis=-1, keepdims=True) + eps) * g


def _layer_norm(v, g, b, eps=1e-5):
    mu = jnp.mean(v, axis=-1, keepdims=True)
    vc = v - mu
    var = jnp.mean(vc * vc, axis=-1, keepdims=True)
    return vc * lax.rsqrt(var + eps) * g + b


def _softmax_update(m_old, l_old, acc, s, v):
    m_new = jnp.maximum(m_old, jnp.max(s, axis=-1, keepdims=True))
    alpha = jnp.exp(m_old - m_new)
    p = jnp.exp(s - m_new)
    l_new = alpha * l_old + jnp.sum(p, axis=-1, keepdims=True)
    return m_new, l_new, alpha * acc + _dot(p.astype(BF16), v)


D_IN_EVEN = D_POOL + Q_LORA + KV_LORA + 2 * QK_ROPE


def _in_even_kernel(xp_ref, xs_ref, w_ref, qg_ref, kvg_ref, cs_ref,
                    u_ref, cq_ref, ckvp_ref, ckvs_ref, kpep_ref, kpes_ref, vb_ref, kt_ref, *, tm):
    i = pl.program_id(0)
    is_prompt = i < N_PROMPT // tm
    xb = jnp.where(is_prompt, xp_ref[...], xs_ref[...]).astype(BF16)
    h = _dot(xb, w_ref[...])
    o1, o2 = D_POOL + Q_LORA, D_POOL + Q_LORA + KV_LORA
    u_ref[...] = h[:, :D_POOL]
    cq_ref[...] = _rms(h[:, D_POOL:o1], qg_ref[...]).astype(BF16)
    ckv = _rms(h[:, o1:o2], kvg_ref[...])
    prod = h[:, o2:] * cs_ref[...]
    kpe2 = prod + pltpu.roll(prod, QK_ROPE, 1)
    kpe = kpe2[:, :QK_ROPE]

    @pl.when(is_prompt)
    def _():
        ckvp_ref[...] = ckv
        kpep_ref[...] = kpe
        vb_ref[...] = ckv.astype(BF16)
        for c in range(tm // KV_CHUNK):
            rows = slice(c * KV_CHUNK, (c + 1) * KV_CHUNK)
            kt_ref[c, :KV_LORA, :] = ckv[rows, :].T.astype(BF16)
            kt_ref[c, KV_LORA:, :] = kpe2[rows, :].T[:QK_ROPE, :].astype(BF16)

    @pl.when(jnp.logical_not(is_prompt))
    def _():
        ckvs_ref[...] = ckv
        kpes_ref[...] = kpe


def _in_even(xp, xs, w_all, qg, kvg, cs, tm=512):
    row = lambda i: (i, 0)
    cpb = tm // KV_CHUNK
    last = N_PROMPT // tm - 1
    return pl.pallas_call(
        functools.partial(_in_even_kernel, tm=tm),
        grid=(N_TOK // tm,),
        in_specs=[
            _prompt_rows(tm, D_MODEL),
            _sample_rows(tm, D_MODEL),
            _resident((D_MODEL, D_IN_EVEN)),
            _resident((1, Q_LORA)),
            _resident((1, KV_LORA)),
            pl.BlockSpec((tm, 2 * QK_ROPE), row),
        ],
        out_specs=[
            pl.BlockSpec((tm, D_POOL), row),
            pl.BlockSpec((tm, Q_LORA), row),
            _prompt_rows(tm, KV_LORA),
            _sample_rows(tm, KV_LORA),
            _prompt_rows(tm, QK_ROPE),
            _sample_rows(tm, QK_ROPE),
            _prompt_rows(tm, KV_LORA),
            pl.BlockSpec((cpb, D_QK, KV_CHUNK), lambda i: (jnp.minimum(i, last), 0, 0)),
        ],
        out_shape=[
            jax.ShapeDtypeStruct((N_TOK, D_POOL), F32),
            jax.ShapeDtypeStruct((N_TOK, Q_LORA), BF16),
            jax.ShapeDtypeStruct((N_PROMPT, KV_LORA), F32),
            jax.ShapeDtypeStruct((N_SAMPLE, KV_LORA), F32),
            jax.ShapeDtypeStruct((N_PROMPT, QK_ROPE), F32),
            jax.ShapeDtypeStruct((N_SAMPLE, QK_ROPE), F32),
            jax.ShapeDtypeStruct((N_PROMPT, KV_LORA), BF16),
            jax.ShapeDtypeStruct((N_PROMPT // KV_CHUNK, D_QK, KV_CHUNK), BF16),
        ],
        compiler_params=_cparams(("arbitrary",)),
        name="in_even",
    )(xp, xs, w_all, qg, kvg, cs)


def _q_proj_kernel(cq_ref, wqn_ref, wqr_ref, wqrr_ref, wukt_ref, cos_ref, sin_ref, qp_ref, qs_ref, *, nsub):
    i = pl.program_id(0)
    is_prompt = i < N_PROMPT // (nsub * Q_BLOCK)
    cq = cq_ref[...]
    qn = _dot(cq, wqn_ref[...])
    cos = jnp.concatenate([cos_ref[...]] * (N_HEADS // 2), axis=1)
    sin = jnp.concatenate([sin_ref[...]] * (N_HEADS // 2), axis=1)
    qpe = (_dot(cq, wqr_ref[...]) * cos + _dot(cq, wqrr_ref[...]) * sin) * ATTN_SCALE

    def emit(q_ref):
        for h in range(N_HEADS):
            ql = _dot(qn[:, h * QK_NOPE:(h + 1) * QK_NOPE].astype(BF16), wukt_ref[h]) * ATTN_SCALE
            for j in range(nsub):
                rows = slice(j * Q_BLOCK, (j + 1) * Q_BLOCK)
                q_ref[j, h, :, :KV_LORA] = ql[rows].astype(q_ref.dtype)
                q_ref[j, h, :, KV_LORA:] = qpe[rows, h * QK_ROPE:(h + 1) * QK_ROPE].astype(q_ref.dtype)

    @pl.when(is_prompt)
    def _():
        emit(qp_ref)

    @pl.when(jnp.logical_not(is_prompt))
    def _():
        emit(qs_ref)


def _q_proj(cq, wqn, wqr, wqrr, wukt, cos, sin, tm=256):
    nsub = tm // Q_BLOCK
    nbp = N_PROMPT // tm
    row = lambda i: (i, 0)
    blk = (nsub, N_HEADS, Q_BLOCK, D_QK)
    return pl.pallas_call(
        functools.partial(_q_proj_kernel, nsub=nsub),
        grid=(N_TOK // tm,),
        in_specs=[
            pl.BlockSpec((tm, Q_LORA), row),
            _resident((Q_LORA, N_HEADS * QK_NOPE)),
            _resident((Q_LORA, N_HEADS * QK_ROPE)),
            _resident((Q_LORA, N_HEADS * QK_ROPE)),
            _resident((N_HEADS, QK_NOPE, KV_LORA)),
            pl.BlockSpec((tm, 2 * QK_ROPE), row),
            pl.BlockSpec((tm, 2 * QK_ROPE), row),
        ],
        out_specs=[
            pl.BlockSpec(blk, lambda i: (jnp.minimum(i, nbp - 1), 0, 0, 0)),
            pl.BlockSpec(blk, lambda i: (jnp.maximum(i - nbp, 0), 0, 0, 0)),
        ],
        out_shape=[
            jax.ShapeDtypeStruct((N_PROMPT // Q_BLOCK, N_HEADS, Q_BLOCK, D_QK), BF16),
            jax.ShapeDtypeStruct((N_SAMPLE // Q_BLOCK, N_HEADS, Q_BLOCK, D_QK), F32),
        ],
        compiler_params=_cparams(("arbitrary",)),
        name="q_proj",
    )(cq, wqn, wqr, wqrr, wukt, cos, sin)


def _pool_groups(read, pos, w_ref, scale_ref, write):
    for g, w in enumerate(POOL_WINDOWS):
        lanes = slice(g * POOL_GROUP, (g + 1) * POOL_GROUP)
        cur = read(0, lanes)
        win = cur
        for j in range(1, w):
            win = win + read(j, lanes)
        cnt = jnp.minimum(pos + 1, w).astype(F32)
        diff = (win / cnt - cur).astype(BF16)
        write(lanes, _dot(diff, w_ref[g]) * scale_ref[:, lanes])


def _pool_prompt_kernel(u_ref, w_ref, scale_ref, y_ref, xp_ref, *, T):
    t = pl.program_id(1)

    @pl.when(t == 0)
    def _():
        xp_ref[0:POOL_HIST, :] = jnp.zeros((POOL_HIST, D_POOL), F32)

    xp_ref[POOL_HIST:POOL_HIST + T, :] = u_ref[...]
    pos = t * T + lax.broadcasted_iota(jnp.int32, (T, 1), 0)

    def read(j, lanes):
        return xp_ref[POOL_HIST - j:POOL_HIST - j + T, lanes]

    def write(lanes, y):
        y_ref[:, lanes] = y.astype(y_ref.dtype)

    _pool_groups(read, pos, w_ref, scale_ref, write)
    xp_ref[0:POOL_HIST, :] = xp_ref[T:T + POOL_HIST, :]


def _pool_prompt(u, pool_w, pool_scale, T=512):
    nt = SEQ // T
    return pl.pallas_call(
        functools.partial(_pool_prompt_kernel, T=T),
        grid=(BATCH, nt),
        in_specs=[
            pl.BlockSpec((T, D_POOL), lambda b, t: (b * nt + t, 0)),
            _resident((N_POOL_GROUPS, POOL_GROUP, POOL_GROUP)),
            _resident((1, D_POOL)),
        ],
        out_specs=pl.BlockSpec((T, D_POOL), lambda b, t: (b * nt + t, 0)),
        out_shape=jax.ShapeDtypeStruct((N_PROMPT, D_POOL), BF16),
        scratch_shapes=[pltpu.VMEM((T + POOL_HIST, D_POOL), F32)],
        compiler_params=_cparams(("arbitrary", "arbitrary")),
        name="pool_prompt",
    )(u, pool_w, pool_scale)


def _pool_sample_kernel(u_ref, hist_ref, w_ref, scale_ref, y_ref, xp_ref, *, S):
    xp_ref[:, 0:POOL_HIST, :] = hist_ref[...]
    xp_ref[:, POOL_HIST:, :] = u_ref[...]
    pos = PAST_LEN + lax.broadcasted_iota(jnp.int32, (S, DEC_SEQ, 1), 1)
    pos = pos.reshape(S * DEC_SEQ, 1)

    def read(j, lanes):
        return xp_ref[:, POOL_HIST - j:POOL_HIST - j + DEC_SEQ, lanes].reshape(S * DEC_SEQ, POOL_GROUP)

    def write(lanes, y):
        y_ref[:, lanes] = y.astype(y_ref.dtype)

    _pool_groups(read, pos, w_ref, scale_ref, write)


def _pool_sample(u3, hist, pool_w, pool_scale, S=32):
    rows = S * DEC_SEQ
    return pl.pallas_call(
        functools.partial(_pool_sample_kernel, S=S),
        grid=(DEC_BATCH // S,),
        in_specs=[
            pl.BlockSpec((S, DEC_SEQ, D_POOL), lambda i: (i, 0, 0)),
            pl.BlockSpec((S, POOL_HIST, D_POOL), lambda i: (i, 0, 0)),
            _resident((N_POOL_GROUPS, POOL_GROUP, POOL_GROUP)),
            _resident((1, D_POOL)),
        ],
        out_specs=pl.BlockSpec((rows, D_POOL), lambda i: (i, 0)),
        out_shape=jax.ShapeDtypeStruct((N_SAMPLE, D_POOL), BF16),
        scratch_shapes=[pltpu.VMEM((S, POOL_HIST + DEC_SEQ, D_POOL), F32)],
        compiler_params=_cparams(("arbitrary",)),
        name="pool_sample",
    )(u3, hist, pool_w, pool_scale)


def _prompt_attn_kernel(q_ref, kt_ref, v_ref, wuv_ref, o_ref, m_ref, l_ref, acc_ref, sa_ref, sb_ref):
    i = pl.program_id(1)
    rows_all = N_HEADS * Q_BLOCK
    m_ref[...] = jnp.full((rows_all, 1), -jnp.inf, F32)
    l_ref[...] = jnp.zeros((rows_all, 1), F32)
    acc_ref[...] = jnp.zeros((rows_all, KV_LORA), F32)
    q = q_ref[0].reshape(rows_all, D_QK)
    q_pos = i * Q_BLOCK + (lax.broadcasted_iota(jnp.int32, (rows_all, KV_CHUNK), 0) & (Q_BLOCK - 1))
    key = lax.broadcasted_iota(jnp.int32, (rows_all, KV_CHUNK), 1)

    def scores(j, s_ref):
        s = _dot(q, kt_ref[j])
        s_ref[...] = jnp.where(j * KV_CHUNK + key <= q_pos, s, -jnp.inf)

    def attend(j, s_ref):
        v = v_ref[pl.ds(pl.multiple_of(j * KV_CHUNK, KV_CHUNK), KV_CHUNK), :]
        m_new, l_new, acc_new = _softmax_update(m_ref[...], l_ref[...], acc_ref[...], s_ref[...], v)
        m_ref[...] = m_new
        l_ref[...] = l_new
        acc_ref[...] = acc_new

    n_chunks = (i * Q_BLOCK + Q_BLOCK + KV_CHUNK - 1) // KV_CHUNK
    n_pairs = (n_chunks - 1) // 2
    scores(0, sa_ref)

    def body(t, c):
        scores(2 * t + 1, sb_ref)
        attend(2 * t, sa_ref)
        scores(2 * t + 2, sa_ref)
        attend(2 * t + 1, sb_ref)
        return c

    lax.fori_loop(0, n_pairs, body, 0)
    done = 2 * n_pairs

    @pl.when(done + 2 == n_chunks)
    def _():
        scores(done + 1, sb_ref)
        attend(done, sa_ref)
        attend(done + 1, sb_ref)

    @pl.when(done + 1 == n_chunks)
    def _():
        attend(done, sa_ref)

    for h in range(N_HEADS):
        rows = slice(h * Q_BLOCK, (h + 1) * Q_BLOCK)
        o_lat = (acc_ref[rows] * (1.0 / l_ref[rows])).astype(BF16)
        o_ref[:, h * V_HEAD:(h + 1) * V_HEAD] = _dot(o_lat, wuv_ref[h]).astype(o_ref.dtype)


def _prompt_attn(qp, kt, vb, wuv):
    nqb = SEQ // Q_BLOCK
    cps = SEQ // KV_CHUNK
    rows = N_HEADS * Q_BLOCK
    return pl.pallas_call(
        _prompt_attn_kernel,
        grid=(BATCH, nqb),
        in_specs=[
            pl.BlockSpec((1, N_HEADS, Q_BLOCK, D_QK), lambda b, i: (b * nqb + i, 0, 0, 0)),
            pl.BlockSpec((cps, D_QK, KV_CHUNK), lambda b, i: (b, 0, 0)),
            pl.BlockSpec((SEQ, KV_LORA), lambda b, i: (b, 0)),
            _resident((N_HEADS, KV_LORA, V_HEAD)),
        ],
        out_specs=pl.BlockSpec((Q_BLOCK, D_MLA_OUT), lambda b, i: (b * nqb + i, 0)),
        out_shape=jax.ShapeDtypeStruct((N_PROMPT, D_MLA_OUT), BF16),
        scratch_shapes=[
            pltpu.VMEM((rows, 1), F32),
            pltpu.VMEM((rows, 1), F32),
            pltpu.VMEM((rows, KV_LORA), F32),
            pltpu.VMEM((rows, KV_CHUNK), F32),
            pltpu.VMEM((rows, KV_CHUNK), F32),
        ],
        compiler_params=_cparams(("arbitrary", "arbitrary")),
        name="prompt_attn",
    )(qp, kt, vb, wuv)


PAGES_PER_CHUNK = 16
CHUNK_KEYS = PAGES_PER_CHUNK * PAGE_SIZE
CHUNKS_PER_SEQ = N_PAGES // PAGES_PER_CHUNK
DEC_SLOTS = 4
DEC_AHEAD = DEC_SLOTS - 1
NEW_KEYS_PAD = 16
assert CHUNKS_PER_SEQ % DEC_SLOTS == 0


def _decode_attn_kernel(pt_ref, q_ref, ckvn_ref, kpen_ref, cckv_hbm, ckpet_hbm, o_ref,
                        ckv_buf, kpet_buf, kvb_a, kvb_b, sc_a, sc_b, sem):
    s = pl.program_id(0)
    nseq = pl.num_programs(0)
    rows = N_HEADS * DEC_SEQ

    def page_copies(seq, c, p):
        slot = c % DEC_SLOTS
        page = pt_ref[(seq * CHUNKS_PER_SEQ + c) * PAGES_PER_CHUNK + p]
        keys = pl.ds(p * PAGE_SIZE, PAGE_SIZE)
        return (pltpu.make_async_copy(cckv_hbm.at[page], ckv_buf.at[slot, keys, :], sem.at[0, slot]),
                pltpu.make_async_copy(ckpet_hbm.at[page], kpet_buf.at[slot, :, keys], sem.at[1, slot]))

    def start_chunk(seq, c):
        for p in range(PAGES_PER_CHUNK):
            for cp in page_copies(seq, c, p):
                cp.start()

    def wait_chunk(seq, c):
        for p in range(PAGES_PER_CHUNK):
            for cp in page_copies(seq, c, p):
                cp.wait()

    @pl.when(s == 0)
    def _():
        for c in range(DEC_AHEAD):
            start_chunk(0, c)

    q = q_ref[0].reshape(rows, D_QK)
    q_lat = q[:, :KV_LORA].astype(BF16)
    q_pe = q[:, KV_LORA:].astype(BF16)

    def scores(c, kvb_ref, sc_ref):
        nxt = c + DEC_AHEAD
        if nxt < CHUNKS_PER_SEQ:
            start_chunk(s, nxt)
        else:
            @pl.when(s + 1 < nseq)
            def _():
                start_chunk(s + 1, nxt - CHUNKS_PER_SEQ)
        wait_chunk(s, c)
        slot = c % DEC_SLOTS
        kvb_ref[...] = ckv_buf[slot].astype(BF16)
        sc_ref[...] = _dot_nt(q_lat, kvb_ref[...]) + _dot(q_pe, kpet_buf[slot].astype(BF16))

    def attend(carry, kvb_ref, sc_ref):
        return _softmax_update(*carry, sc_ref[...], kvb_ref[...])

    bufs = ((kvb_a, sc_a), (kvb_b, sc_b))
    carry = (jnp.full((rows, 1), -jnp.inf, F32), jnp.zeros((rows, 1), F32), jnp.zeros((rows, KV_LORA), F32))
    scores(0, *bufs[0])
    for c in range(1, CHUNKS_PER_SEQ):
        scores(c, *bufs[c % 2])
        carry = attend(carry, *bufs[(c - 1) % 2])
    carry = attend(carry, *bufs[(CHUNKS_PER_SEQ - 1) % 2])

    pad = NEW_KEYS_PAD - DEC_SEQ
    kvn = jnp.concatenate([ckvn_ref[...], jnp.zeros((pad, KV_LORA), F32)], axis=0).astype(BF16)
    kpn = jnp.concatenate([kpen_ref[...], jnp.zeros((pad, QK_ROPE), F32)], axis=0).astype(BF16)
    sc = _dot_nt(q_lat, kvn) + _dot_nt(q_pe, kpn)
    tok = lax.broadcasted_iota(jnp.int32, (rows, NEW_KEYS_PAD), 0) & (DEC_SEQ - 1)
    key = lax.broadcasted_iota(jnp.int32, (rows, NEW_KEYS_PAD), 1)
    sc = jnp.where(key <= tok, sc, -jnp.inf)
    _, l_fin, acc = _softmax_update(*carry, sc, kvn)
    o_ref[0] = (acc * (1.0 / l_fin)).reshape(N_HEADS, DEC_SEQ, KV_LORA)


def _decode_attn(page_table, qs, ckv_s, kpe_s, cache_ckv, cache_kpet):
    seq_per_blk = Q_BLOCK // DEC_SEQ
    grid_spec = pltpu.PrefetchScalarGridSpec(
        num_scalar_prefetch=1,
        grid=(DEC_BATCH,),
        in_specs=[
            pl.BlockSpec((1, N_HEADS, DEC_SEQ, D_QK), lambda s, pt: (s // seq_per_blk, 0, s % seq_per_blk, 0)),
            pl.BlockSpec((DEC_SEQ, KV_LORA), lambda s, pt: (s, 0)),
            pl.BlockSpec((DEC_SEQ, QK_ROPE), lambda s, pt: (s, 0)),
            pl.BlockSpec(memory_space=pl.ANY),
            pl.BlockSpec(memory_space=pl.ANY),
        ],
        out_specs=pl.BlockSpec((1, N_HEADS, DEC_SEQ, KV_LORA),
                               lambda s, pt: (s // seq_per_blk, 0, s % seq_per_blk, 0)),
        scratch_shapes=[
            pltpu.VMEM((DEC_SLOTS, CHUNK_KEYS, KV_LORA), F32),
            pltpu.VMEM((DEC_SLOTS, QK_ROPE, CHUNK_KEYS), F32),
            pltpu.VMEM((CHUNK_KEYS, KV_LORA), BF16),
            pltpu.VMEM((CHUNK_KEYS, KV_LORA), BF16),
            pltpu.VMEM((N_HEADS * DEC_SEQ, CHUNK_KEYS), F32),
            pltpu.VMEM((N_HEADS * DEC_SEQ, CHUNK_KEYS), F32),
            pltpu.SemaphoreType.DMA((2, DEC_SLOTS)),
        ],
    )
    return pl.pallas_call(
        _decode_attn_kernel,
        grid_spec=grid_spec,
        out_shape=jax.ShapeDtypeStruct((N_SAMPLE // Q_BLOCK, N_HEADS, Q_BLOCK, KV_LORA), F32),
        compiler_params=_cparams(("arbitrary",)),
        name="decode_attn",
    )(page_table.reshape(-1), qs, ckv_s, kpe_s, cache_ckv, cache_kpet)


def _uv_proj_kernel(ol_ref, wuv_ref, o_ref, *, nsub):
    for j in range(nsub):
        for h in range(N_HEADS):
            o = _dot(ol_ref[j, h].astype(BF16), wuv_ref[h])
            o_ref[j * Q_BLOCK:(j + 1) * Q_BLOCK, h * V_HEAD:(h + 1) * V_HEAD] = o.astype(o_ref.dtype)


def _uv_proj_sample(o_lat, wuv, nsub=4):
    nb = o_lat.shape[0]
    rows = nsub * Q_BLOCK
    return pl.pallas_call(
        functools.partial(_uv_proj_kernel, nsub=nsub),
        grid=(nb // nsub,),
        in_specs=[
            pl.BlockSpec((nsub, N_HEADS, Q_BLOCK, KV_LORA), lambda i: (i, 0, 0, 0)),
            _resident((N_HEADS, KV_LORA, V_HEAD)),
        ],
        out_specs=pl.BlockSpec((rows, D_MLA_OUT), lambda i: (i, 0)),
        out_shape=jax.ShapeDtypeStruct((nb * Q_BLOCK, D_MLA_OUT), BF16),
        compiler_params=_cparams(("arbitrary",)),
        name="uv_proj_sample",
    )(o_lat, wuv)


def _mm_ln_kernel(*refs, lhs_split, split_resid, split_out, tm):
    is_prompt = pl.program_id(0) < N_PROMPT // tm
    refs = list(refs)

    def take(split):
        if split:
            p_ref, s_ref = refs.pop(0), refs.pop(0)
            return lambda rows: jnp.where(is_prompt, p_ref[rows, :], s_ref[rows, :])
        ref = refs.pop(0)
        return lambda rows: ref[rows, :]

    lhs = [take(split) for split in lhs_split]
    w_ref = refs.pop(0)
    resid = take(split_resid)
    g_ref, b_ref = refs.pop(0), refs.pop(0)
    out_refs = refs

    sub = min(tm, FFN_SUB_ROWS)
    for r in range(0, tm, sub):
        rows = slice(r, r + sub)
        acc = None
        off = 0
        for piece in lhs:
            a = piece(rows)
            part = _dot(a, w_ref[off:off + a.shape[1], :])
            acc = part if acc is None else acc + part
            off += a.shape[1]
        y = _layer_norm(DEEPNORM_ALPHA * resid(rows) + acc, g_ref[...], b_ref[...])
        if split_out:
            @pl.when(is_prompt)
            def _():
                out_refs[0][rows, :] = y

            @pl.when(jnp.logical_not(is_prompt))
            def _():
                out_refs[1][rows, :] = y
        else:
            out_refs[0][rows, :] = y
            out_refs[1][rows, :] = y.astype(BF16)


def _mm_ln(lhs_list, w, resid, g, b, *, tm, split_out=False, layer=None):
    operands, in_specs = [], []

    def add_rows(v):
        if isinstance(v, tuple):
            operands.extend(v)
            in_specs.extend([_prompt_rows(tm, v[0].shape[1]), _sample_rows(tm, v[0].shape[1])])
        else:
            operands.append(v)
            in_specs.append(pl.BlockSpec((tm, v.shape[1]), lambda i: (i, 0)))
        return isinstance(v, tuple)

    lhs_split = tuple(add_rows(v) for v in lhs_list)
    operands.append(w)
    if layer is None:
        in_specs.append(_resident(w.shape))
    else:
        in_specs.append(pl.BlockSpec((None,) + w.shape[1:], lambda i: (layer, 0, 0), pipeline_mode=pl.Buffered(1)))
    split_resid = add_rows(resid)
    operands.extend([g, b])
    in_specs.extend([_resident((1, D_MODEL)), _resident((1, D_MODEL))])
    if split_out:
        out_specs = [_prompt_rows(tm, D_MODEL), _sample_rows(tm, D_MODEL)]
        out_shape = [jax.ShapeDtypeStruct((N_PROMPT, D_MODEL), F32), jax.ShapeDtypeStruct((N_SAMPLE, D_MODEL), F32)]
    else:
        row = pl.BlockSpec((tm, D_MODEL), lambda i: (i, 0))
        out_specs = [row, row]
        out_shape = [jax.ShapeDtypeStruct((N_TOK, D_MODEL), F32), jax.ShapeDtypeStruct((N_TOK, D_MODEL), BF16)]
    return pl.pallas_call(
        functools.partial(_mm_ln_kernel, lhs_split=lhs_split, split_resid=split_resid, split_out=split_out, tm=tm),
        grid=(N_TOK // tm,),
        in_specs=in_specs,
        out_specs=out_specs,
        out_shape=out_shape,
        compiler_params=_cparams(("arbitrary",)),
        name="mm_ln",
    )(*operands)


def _ffn_up_kernel(x_ref, wg_ref, wu_ref, h_ref, wgb_ref, wub_ref):
    @pl.when(pl.program_id(1) == 0)
    def _():
        wgb_ref[...] = wg_ref[...].astype(BF16)
        wub_ref[...] = wu_ref[...].astype(BF16)

    for r in range(0, x_ref.shape[0], FFN_SUB_ROWS):
        x = x_ref[r:r + FFN_SUB_ROWS, :]
        h = jax.nn.silu(_dot(x, wgb_ref[...])) * _dot(x, wub_ref[...])
        h_ref[r:r + FFN_SUB_ROWS, :] = h.astype(h_ref.dtype)


def _ffn_up(xb, wg, wu, layer, tm=N_TOK // 4, tn=512):
    wblk = pl.BlockSpec((None, D_MODEL, tn), lambda j, i: (layer, 0, j))
    return pl.pallas_call(
        _ffn_up_kernel,
        grid=(D_FF // tn, N_TOK // tm),
        in_specs=[pl.BlockSpec((tm, D_MODEL), lambda j, i: (i, 0)), wblk, wblk],
        out_specs=pl.BlockSpec((tm, tn), lambda j, i: (i, j)),
        out_shape=jax.ShapeDtypeStruct((N_TOK, D_FF), BF16),
        scratch_shapes=[pltpu.VMEM((D_MODEL, tn), BF16), pltpu.VMEM((D_MODEL, tn), BF16)],
        compiler_params=_cparams(("arbitrary", "arbitrary")),
        name="ffn_up",
    )(xb, wg, wu)


def _conv_taps(read, cw_ref):
    conv = read(CONV_W - 1) * cw_ref[0:1, :]
    for j in range(1, CONV_W):
        conv = conv + read(CONV_W - 1 - j) * cw_ref[j:j + 1, :]
    return conv


def _in_odd_cast_weights(wv_ref, wb_ref, wc_ref, wvb_ref, wbb_ref, wcb_ref):
    @pl.when(pl.program_id(1) == 0)
    def _():
        wvb_ref[...] = wv_ref[...].astype(BF16)
        wbb_ref[...] = wb_ref[...].astype(BF16)
        wcb_ref[...] = wc_ref[...].astype(BF16)


def _in_odd_prompt_kernel(x_ref, wv_ref, wb_ref, wc_ref, cw_ref, g_ref, ztail_ref,
                          wvb_ref, wbb_ref, wcb_ref, zp_ref, *, tm):
    i = pl.program_id(1)
    tiles_per_seq = SEQ // tm
    _in_odd_cast_weights(wv_ref, wb_ref, wc_ref, wvb_ref, wbb_ref, wcb_ref)

    @pl.when(i % tiles_per_seq == 0)
    def _():
        zp_ref[0:CONV_HIST, :] = jnp.zeros((CONV_HIST, zp_ref.shape[1]), F32)

    for r in range(0, tm, FFN_SUB_ROWS):
        x = x_ref[r:r + FFN_SUB_ROWS, :]
        new = slice(CONV_HIST + r, CONV_HIST + r + FFN_SUB_ROWS)
        zp_ref[new, :] = _dot(x, wcb_ref[...]) * _dot(x, wvb_ref[...])
        conv = _conv_taps(lambda back: zp_ref[new.start - back:new.stop - back, :], cw_ref)
        g_ref[r:r + FFN_SUB_ROWS, :] = (_dot(x, wbb_ref[...]) * conv).astype(g_ref.dtype)
    zp_ref[0:CONV_HIST, :] = zp_ref[tm:tm + CONV_HIST, :]

    @pl.when(i % tiles_per_seq == tiles_per_seq - 1)
    def _():
        ztail_ref[...] = zp_ref[tm:tm + CONV_HIST, :]


def _in_odd_sample_kernel(x_ref, wv_ref, wb_ref, wc_ref, zs_ref, gbs_ref, wvb_ref, wbb_ref, wcb_ref):
    _in_odd_cast_weights(wv_ref, wb_ref, wc_ref, wvb_ref, wbb_ref, wcb_ref)
    x = x_ref[...]
    zs_ref[...] = _dot(x, wcb_ref[...]) * _dot(x, wvb_ref[...])
    gbs_ref[...] = _dot(x, wbb_ref[...])


def _in_odd(xb, w_in, conv_w, tm=1024, tn=512):
    nb = D_MODEL // tn
    nbp = N_PROMPT // tm
    tiles_per_seq = SEQ // tm
    wblks = [pl.BlockSpec((D_MODEL, tn), lambda j, i, part=part: (0, part * nb + j)) for part in range(3)]
    wscratch = [pltpu.VMEM((D_MODEL, tn), BF16)] * 3
    g_prompt, z_tail = pl.pallas_call(
        functools.partial(_in_odd_prompt_kernel, tm=tm),
        grid=(nb, nbp),
        in_specs=[pl.BlockSpec((tm, D_MODEL), lambda j, i: (i, 0))] + wblks
        + [pl.BlockSpec((CONV_W, tn), lambda j, i: (0, j))],
        out_specs=[
            pl.BlockSpec((tm, tn), lambda j, i: (i, j)),
            pl.BlockSpec((CONV_HIST, tn), lambda j, i: (i // tiles_per_seq, j)),
        ],
        out_shape=[
            jax.ShapeDtypeStruct((N_PROMPT, D_MODEL), BF16),
            jax.ShapeDtypeStruct((BATCH * CONV_HIST, D_MODEL), F32),
        ],
        scratch_shapes=wscratch + [pltpu.VMEM((tm + CONV_HIST, tn), F32)],
        compiler_params=_cparams(("arbitrary", "arbitrary")),
        name="in_odd_prompt",
    )(xb, w_in, w_in, w_in, conv_w)
    sample_blk = pl.BlockSpec((tm, tn), lambda j, i: (i, j))
    z_s, gb_s = pl.pallas_call(
        _in_odd_sample_kernel,
        grid=(nb, N_SAMPLE // tm),
        in_specs=[pl.BlockSpec((tm, D_MODEL), lambda j, i: (nbp + i, 0))] + wblks,
        out_specs=[sample_blk, sample_blk],
        out_shape=[jax.ShapeDtypeStruct((N_SAMPLE, D_MODEL), F32)] * 2,
        scratch_shapes=wscratch,
        compiler_params=_cparams(("arbitrary", "arbitrary")),
        name="in_odd_sample",
    )(xb, w_in, w_in, w_in)
    return g_prompt, z_tail, z_s, gb_s


def _conv_sample_kernel(z_ref, hist_ref, gb_ref, cw_ref, g_ref, zp_ref, *, S):
    zp_ref[:, 0:CONV_HIST, :] = hist_ref[...]
    zp_ref[:, CONV_HIST:, :] = z_ref[...]
    conv = _conv_taps(lambda back: zp_ref[:, CONV_HIST - back:CONV_HIST - back + DEC_SEQ, :], cw_ref)
    g_ref[...] = (gb_ref[...] * conv).reshape(S * DEC_SEQ, D_MODEL).astype(g_ref.dtype)


def _conv_sample(z3, hist, gb3, conv_w, S=32):
    blk = lambda i: (i, 0, 0)
    rows = S * DEC_SEQ
    return pl.pallas_call(
        functools.partial(_conv_sample_kernel, S=S),
        grid=(DEC_BATCH // S,),
        in_specs=[
            pl.BlockSpec((S, DEC_SEQ, D_MODEL), blk),
            pl.BlockSpec((S, CONV_HIST, D_MODEL), blk),
            pl.BlockSpec((S, DEC_SEQ, D_MODEL), blk),
            _resident((CONV_W, D_MODEL)),
        ],
        out_specs=pl.BlockSpec((rows, D_MODEL), lambda i: (i, 0)),
        out_shape=jax.ShapeDtypeStruct((N_SAMPLE, D_MODEL), BF16),
        scratch_shapes=[pltpu.VMEM((S, CONV_HIST + DEC_SEQ, D_MODEL), F32)],
        compiler_params=_cparams(("arbitrary",)),
        name="conv_sample",
    )(z3, hist, gb3, conv_w)


def _rope_tables():
    pos = jnp.concatenate([jnp.tile(jnp.arange(SEQ), BATCH),
                           jnp.tile(PAST_LEN + jnp.arange(DEC_SEQ), DEC_BATCH)])
    inv = ROPE_THETA ** (-jnp.arange(0, QK_ROPE, 2, dtype=F32) / QK_ROPE)
    ang = pos.astype(F32)[:, None] * inv[None, :]
    cos, sin = jnp.cos(ang), jnp.sin(ang)
    cos2 = jnp.concatenate([cos, cos], axis=1)
    sin2 = jnp.concatenate([-sin, sin], axis=1)
    return jnp.tile(cos2, (1, 2)), jnp.tile(sin2, (1, 2)), jnp.concatenate([cos2, sin2], axis=1)


def _swap_halves(w):
    half = QK_ROPE // 2
    return jnp.concatenate([w[..., half:], w[..., :half]], axis=-1)


def kernel(x_prompt, x_sample, cache_ckv, cache_kpe, state_pool, state_conv, page_table, w_in_even, pool_w, pool_scale, q_norm_g, w_q_b, kv_norm_g, w_uk, w_uv, w_out_even, w_in_odd, conv_w, w_out_odd, ffn_w_gate, ffn_w_up, ffn_w_down, ln1_g, ln1_b, ln2_g, ln2_b):
    assert DEPTH == 2
    n_pool_pages = cache_ckv.shape[1]
    cos, sin, cos_sin = _rope_tables()
    xp = x_prompt.reshape(N_PROMPT, D_MODEL)
    xs = x_sample.reshape(N_SAMPLE, D_MODEL)

    def row(v):
        return v.reshape(1, -1)

    w_down = ffn_w_down.astype(BF16)

    def ffn(layer, xf, xb, last):
        hmid = _ffn_up(xb, ffn_w_gate, ffn_w_up, layer)
        return _mm_ln([hmid], w_down, xf, row(ln2_g[layer]), row(ln2_b[layer]),
                      tm=256, split_out=last, layer=layer)

    e = 0
    w_in = w_in_even[e]
    w_all = jnp.concatenate([w_in, _swap_halves(w_in[:, D_POOL + Q_LORA + KV_LORA:])], axis=1).astype(BF16)
    u, cq, ckv_p, ckv_s, kpe_p, kpe_s, vb, kt = _in_even(
        xp, xs, w_all, row(q_norm_g[e]), row(kv_norm_g[e]), cos_sin)

    wq = w_q_b[e]
    wq_rope = wq[:, :, QK_NOPE:]
    qp, qs = _q_proj(
        cq,
        wq[:, :, :QK_NOPE].reshape(Q_LORA, N_HEADS * QK_NOPE).astype(BF16),
        wq_rope.reshape(Q_LORA, N_HEADS * QK_ROPE).astype(BF16),
        _swap_halves(wq_rope).reshape(Q_LORA, N_HEADS * QK_ROPE).astype(BF16),
        jnp.transpose(w_uk[e], (1, 2, 0)).astype(BF16), cos, sin)

    pw = pool_w[e].astype(BF16)
    ps = row(pool_scale[e])
    u_sample = u[N_PROMPT:].reshape(DEC_BATCH, DEC_SEQ, D_POOL)
    pool_hist = jnp.pad(state_pool[e], ((0, 0), (POOL_HIST - POOL_BUF, 0), (0, 0)))
    y_pool = (_pool_prompt(u, pw, ps), _pool_sample(u_sample, pool_hist, pw, ps))

    wuv = jnp.transpose(w_uv[e], (1, 0, 2)).astype(BF16)
    o_prompt = _prompt_attn(qp, kt, vb, wuv)
    o_lat_s = _decode_attn(page_table, qs, ckv_s, kpe_s,
                           cache_ckv[e].reshape(n_pool_pages, PAGE_SIZE, KV_LORA),
                           jnp.swapaxes(cache_kpe[e], 1, 2))
    o_sample = _uv_proj_sample(o_lat_s, wuv)
    xf, xb = _mm_ln([y_pool, (o_prompt, o_sample)], w_out_even[e].astype(BF16), (xp, xs),
                    row(ln1_g[0]), row(ln1_b[0]), tm=512)
    xf, xb = ffn(0, xf, xb, False)

    od = 0
    g_prompt, z_tail, z_s, gb_s = _in_odd(xb, w_in_odd[od], conv_w[od])
    z_sample = z_s.reshape(DEC_BATCH, DEC_SEQ, D_MODEL)
    gb_sample = gb_s.reshape(DEC_BATCH, DEC_SEQ, D_MODEL)
    conv_hist = jnp.pad(state_conv[od], ((0, 0), (CONV_HIST - CONV_BUF, 0), (0, 0)))
    g = (g_prompt, _conv_sample(z_sample, conv_hist, gb_sample, conv_w[od]))
    xf, xb = _mm_ln([g], w_out_odd[od].astype(BF16), xf, row(ln1_g[1]), row(ln1_b[1]), tm=512)
    y_p, y_s = ffn(1, xf, xb, True)

    pool_prompt = jnp.stack([u[(b + 1) * SEQ - POOL_BUF:(b + 1) * SEQ] for b in range(BATCH)])
    pool_sample = jnp.concatenate([state_pool[e][:, DEC_SEQ:], u_sample], axis=1)
    conv_prompt = z_tail.reshape(BATCH, CONV_HIST, D_MODEL)[:, CONV_HIST - CONV_BUF:]
    conv_sample = z_sample[:, DEC_SEQ - CONV_BUF:]
    return (y_p.reshape(BATCH, SEQ, D_MODEL), y_s.reshape(DEC_BATCH, DEC_SEQ, D_MODEL),
            ckv_p.reshape(1, BATCH, SEQ, KV_LORA), kpe_p.reshape(1, BATCH, SEQ, QK_ROPE),
            pool_prompt[None], conv_prompt[None],
            ckv_s.reshape(1, DEC_BATCH, DEC_SEQ, KV_LORA), kpe_s.reshape(1, DEC_BATCH, DEC_SEQ, QK_ROPE),
            pool_sample[None], conv_sample[None])
```

```python
import functools

import jax
import jax.numpy as jnp
from jax import lax
from jax.experimental import pallas as pl
from jax.experimental.pallas import tpu as pltpu

F32 = jnp.float32
BF16 = jnp.bfloat16

D_MODEL = 2048
BATCH = 4
SEQ = 2048
DEPTH = 2
DEC_BATCH = 128
DEC_SEQ = 8
PAST_LEN = 16384
PAGE_SIZE = 128
N_PAGES = PAST_LEN // PAGE_SIZE
POOL_WINDOWS = (2, 4, 8, 16)
N_POOL_GROUPS = len(POOL_WINDOWS)
D_POOL = D_MODEL // 2
POOL_GROUP = D_POOL // N_POOL_GROUPS
POOL_BUF = max(POOL_WINDOWS) - 1
N_HEADS = 8
QK_NOPE = 128
QK_ROPE = 64
V_HEAD = 128
Q_LORA = 512
KV_LORA = 512
ROPE_THETA = 10000.0
ATTN_SCALE = (QK_NOPE + QK_ROPE) ** -0.5
Q_BLOCK = 128
D_MLA_OUT = N_HEADS * V_HEAD
CONV_W = 3
CONV_BUF = CONV_W - 1
D_FF = ((8 * D_MODEL + 3 * 256 - 1) // (3 * 256)) * 256
DEEPNORM_ALPHA = (2 * DEPTH) ** 0.25

N_PROMPT = BATCH * SEQ
N_SAMPLE = DEC_BATCH * DEC_SEQ
N_TOK = N_PROMPT + N_SAMPLE
D_QK = KV_LORA + QK_ROPE
POOL_HIST = 16
CONV_HIST = 8
KV_CHUNK = 512
FFN_SUB_ROWS = 256
VMEM_LIMIT_MB = 56


def _cparams(dims, vmem_mb=VMEM_LIMIT_MB):
    return pltpu.CompilerParams(dimension_semantics=dims, vmem_limit_bytes=vmem_mb * 1024 * 1024)


def _resident(shape):
    zeros = (0,) * len(shape)
    return pl.BlockSpec(shape, lambda *_: zeros, pipeline_mode=pl.Buffered(1))


def _prompt_rows(tm, width):
    last = N_PROMPT // tm - 1
    return pl.BlockSpec((tm, width), lambda i, *_: (jnp.minimum(i, last), 0))


def _sample_rows(tm, width):
    first = N_PROMPT // tm
    return pl.BlockSpec((tm, width), lambda i, *_: (jnp.maximum(i - first, 0), 0))


def _dot(a, b):
    return jnp.dot(a, b, preferred_element_type=F32)


def _dot_nt(a, b):
    return lax.dot_general(a, b, (((1,), (1,)), ((), ())), preferred_element_type=F32)


def _rms(h, g, eps=1e-6):
    return h * lax.rsqrt(jnp.mean(h * h, axis=-1, keepdims=True) + eps) * g


def _layer_norm(v, g, b, eps=1e-5):
    mu = jnp.mean(v, axis=-1, keepdims=True)
    vc = v - mu
    var = jnp.mean(vc * vc, axis=-1, keepdims=True)
    return vc * lax.rsqrt(var + eps) * g + b


def _softmax_update(m_old, l_old, acc, s, v):
    m_new = jnp.maximum(m_old, jnp.max(s, axis=-1, keepdims=True))
    alpha = jnp.exp(m_old - m_new)
    p = jnp.exp(s - m_new)
    l_new = alpha * l_old + jnp.sum(p, axis=-1, keepdims=True)
    return m_new, l_new, alpha * acc + _dot(p.astype(BF16), v)


D_IN_EVEN = D_POOL + Q_LORA + KV_LORA + 2 * QK_ROPE


def _in_even_kernel(xp_ref, xs_ref, w_ref, qg_ref, kvg_ref, cs_ref,
                    u_ref, cq_ref, ckvp_ref, ckvs_ref, kpep_ref, kpes_ref, vb_ref, kt_ref, *, tm):
    i = pl.program_id(0)
    is_prompt = i < N_PROMPT // tm
    xb = jnp.where(is_prompt, xp_ref[...], xs_ref[...]).astype(BF16)
    h = _dot(xb, w_ref[...])
    o1, o2 = D_POOL + Q_LORA, D_POOL + Q_LORA + KV_LORA
    u_ref[...] = h[:, :D_POOL]
    cq_ref[...] = _rms(h[:, D_POOL:o1], qg_ref[...]).astype(BF16)
    ckv = _rms(h[:, o1:o2], kvg_ref[...])
    prod = h[:, o2:] * cs_ref[...]
    kpe2 = prod + pltpu.roll(prod, QK_ROPE, 1)
    kpe = kpe2[:, :QK_ROPE]

    @pl.when(is_prompt)
    def _():
        ckvp_ref[...] = ckv
        kpep_ref[...] = kpe
        vb_ref[...] = ckv.astype(BF16)
        for c in range(tm // KV_CHUNK):
            rows = slice(c * KV_CHUNK, (c + 1) * KV_CHUNK)
            kt_ref[c, :KV_LORA, :] = ckv[rows, :].T.astype(BF16)
            kt_ref[c, KV_LORA:, :] = kpe2[rows, :].T[:QK_ROPE, :].astype(BF16)

    @pl.when(jnp.logical_not(is_prompt))
    def _():
        ckvs_ref[...] = ckv
        kpes_ref[...] = kpe


def _in_even(xp, xs, w_all, qg, kvg, cs, tm=512):
    row = lambda i: (i, 0)
    cpb = tm // KV_CHUNK
    last = N_PROMPT // tm - 1
    return pl.pallas_call(
        functools.partial(_in_even_kernel, tm=tm),
        grid=(N_TOK // tm,),
        in_specs=[
            _prompt_rows(tm, D_MODEL),
            _sample_rows(tm, D_MODEL),
            _resident((D_MODEL, D_IN_EVEN)),
            _resident((1, Q_LORA)),
            _resident((1, KV_LORA)),
            pl.BlockSpec((tm, 2 * QK_ROPE), row),
        ],
        out_specs=[
            pl.BlockSpec((tm, D_POOL), row),
            pl.BlockSpec((tm, Q_LORA), row),
            _prompt_rows(tm, KV_LORA),
            _sample_rows(tm, KV_LORA),
            _prompt_rows(tm, QK_ROPE),
            _sample_rows(tm, QK_ROPE),
            _prompt_rows(tm, KV_LORA),
            pl.BlockSpec((cpb, D_QK, KV_CHUNK), lambda i: (jnp.minimum(i, last), 0, 0)),
        ],
        out_shape=[
            jax.ShapeDtypeStruct((N_TOK, D_POOL), F32),
            jax.ShapeDtypeStruct((N_TOK, Q_LORA), BF16),
            jax.ShapeDtypeStruct((N_PROMPT, KV_LORA), F32),
            jax.ShapeDtypeStruct((N_SAMPLE, KV_LORA), F32),
            jax.ShapeDtypeStruct((N_PROMPT, QK_ROPE), F32),
            jax.ShapeDtypeStruct((N_SAMPLE, QK_ROPE), F32),
            jax.ShapeDtypeStruct((N_PROMPT, KV_LORA), BF16),
            jax.ShapeDtypeStruct((N_PROMPT // KV_CHUNK, D_QK, KV_CHUNK), BF16),
        ],
        compiler_params=_cparams(("arbitrary",)),
        name="in_even",
    )(xp, xs, w_all, qg, kvg, cs)


def _q_proj_kernel(cq_ref, wqn_ref, wqr_ref, wqrr_ref, wukt_ref, cos_ref, sin_ref, qp_ref, qs_ref, *, nsub):
    i = pl.program_id(0)
    is_prompt = i < N_PROMPT // (nsub * Q_BLOCK)
    cq = cq_ref[...]
    qn = _dot(cq, wqn_ref[...])
    cos = jnp.concatenate([cos_ref[...]] * (N_HEADS // 2), axis=1)
    sin = jnp.concatenate([sin_ref[...]] * (N_HEADS // 2), axis=1)
    qpe = (_dot(cq, wqr_ref[...]) * cos + _dot(cq, wqrr_ref[...]) * sin) * ATTN_SCALE

    def emit(q_ref):
        for h in range(N_HEADS):
            ql = _dot(qn[:, h * QK_NOPE:(h + 1) * QK_NOPE].astype(BF16), wukt_ref[h]) * ATTN_SCALE
            for j in range(nsub):
                rows = slice(j * Q_BLOCK, (j + 1) * Q_BLOCK)
                q_ref[j, h, :, :KV_LORA] = ql[rows].astype(q_ref.dtype)
                q_ref[j, h, :, KV_LORA:] = qpe[rows, h * QK_ROPE:(h + 1) * QK_ROPE].astype(q_ref.dtype)

    @pl.when(is_prompt)
    def _():
        emit(qp_ref)

    @pl.when(jnp.logical_not(is_prompt))
    def _():
        emit(qs_ref)


def _q_proj(cq, wqn, wqr, wqrr, wukt, cos, sin, tm=256):
    nsub = tm // Q_BLOCK
    nbp = N_PROMPT // tm
    row = lambda i: (i, 0)
    blk = (nsub, N_HEADS, Q_BLOCK, D_QK)
    return pl.pallas_call(
        functools.partial(_q_proj_kernel, nsub=nsub),
        grid=(N_TOK // tm,),
        in_specs=[
            pl.BlockSpec((tm, Q_LORA), row),
            _resident((Q_LORA, N_HEADS * QK_NOPE)),
            _resident((Q_LORA, N_HEADS * QK_ROPE)),
            _resident((Q_LORA, N_HEADS * QK_ROPE)),
            _resident((N_HEADS, QK_NOPE, KV_LORA)),
            pl.BlockSpec((tm, 2 * QK_ROPE), row),
            pl.BlockSpec((tm, 2 * QK_ROPE), row),
        ],
        out_specs=[
            pl.BlockSpec(blk, lambda i: (jnp.minimum(i, nbp - 1), 0, 0, 0)),
            pl.BlockSpec(blk, lambda i: (jnp.maximum(i - nbp, 0), 0, 0, 0)),
        ],
        out_shape=[
            jax.ShapeDtypeStruct((N_PROMPT // Q_BLOCK, N_HEADS, Q_BLOCK, D_QK), BF16),
            jax.ShapeDtypeStruct((N_SAMPLE // Q_BLOCK, N_HEADS, Q_BLOCK, D_QK), F32),
        ],
        compiler_params=_cparams(("arbitrary",)),
        name="q_proj",
    )(cq, wqn, wqr, wqrr, wukt, cos, sin)


def _pool_groups(read, pos, w_ref, scale_ref, write):
    for g, w in enumerate(POOL_WINDOWS):
        lanes = slice(g * POOL_GROUP, (g + 1) * POOL_GROUP)
        cur = read(0, lanes)
        win = cur
        for j in range(1, w):
            win = win + read(j, lanes)
        cnt = jnp.minimum(pos + 1, w).astype(F32)
        diff = (win / cnt - cur).astype(BF16)
        write(lanes, _dot(diff, w_ref[g]) * scale_ref[:, lanes])


def _pool_prompt_kernel(u_ref, w_ref, scale_ref, y_ref, xp_ref, *, T):
    t = pl.program_id(1)

    @pl.when(t == 0)
    def _():
        xp_ref[0:POOL_HIST, :] = jnp.zeros((POOL_HIST, D_POOL), F32)

    xp_ref[POOL_HIST:POOL_HIST + T, :] = u_ref[...]
    pos = t * T + lax.broadcasted_iota(jnp.int32, (T, 1), 0)

    def read(j, lanes):
        return xp_ref[POOL_HIST - j:POOL_HIST - j + T, lanes]

    def write(lanes, y):
        y_ref[:, lanes] = y.astype(y_ref.dtype)

    _pool_groups(read, pos, w_ref, scale_ref, write)
    xp_ref[0:POOL_HIST, :] = xp_ref[T:T + POOL_HIST, :]


def _pool_prompt(u, pool_w, pool_scale, T=512):
    nt = SEQ // T
    return pl.pallas_call(
        functools.partial(_pool_prompt_kernel, T=T),
        grid=(BATCH, nt),
        in_specs=[
            pl.BlockSpec((T, D_POOL), lambda b, t: (b * nt + t, 0)),
            _resident((N_POOL_GROUPS, POOL_GROUP, POOL_GROUP)),
            _resident((1, D_POOL)),
        ],
        out_specs=pl.BlockSpec((T, D_POOL), lambda b, t: (b * nt + t, 0)),
        out_shape=jax.ShapeDtypeStruct((N_PROMPT, D_POOL), BF16),
        scratch_shapes=[pltpu.VMEM((T + POOL_HIST, D_POOL), F32)],
        compiler_params=_cparams(("arbitrary", "arbitrary")),
        name="pool_prompt",
    )(u, pool_w, pool_scale)


def _pool_sample_kernel(u_ref, hist_ref, w_ref, scale_ref, y_ref, xp_ref, *, S):
    xp_ref[:, 0:POOL_HIST, :] = hist_ref[...]
    xp_ref[:, POOL_HIST:, :] = u_ref[...]
    pos = PAST_LEN + lax.broadcasted_iota(jnp.int32, (S, DEC_SEQ, 1), 1)
    pos = pos.reshape(S * DEC_SEQ, 1)

    def read(j, lanes):
        return xp_ref[:, POOL_HIST - j:POOL_HIST - j + DEC_SEQ, lanes].reshape(S * DEC_SEQ, POOL_GROUP)

    def write(lanes, y):
        y_ref[:, lanes] = y.astype(y_ref.dtype)

    _pool_groups(read, pos, w_ref, scale_ref, write)


def _pool_sample(u3, hist, pool_w, pool_scale, S=32):
    rows = S * DEC_SEQ
    return pl.pallas_call(
        functools.partial(_pool_sample_kernel, S=S),
        grid=(DEC_BATCH // S,),
        in_specs=[
            pl.BlockSpec((S, DEC_SEQ, D_POOL), lambda i: (i, 0, 0)),
            pl.BlockSpec((S, POOL_HIST, D_POOL), lambda i: (i, 0, 0)),
            _resident((N_POOL_GROUPS, POOL_GROUP, POOL_GROUP)),
            _resident((1, D_POOL)),
        ],
        out_specs=pl.BlockSpec((rows, D_POOL), lambda i: (i, 0)),
        out_shape=jax.ShapeDtypeStruct((N_SAMPLE, D_POOL), BF16),
        scratch_shapes=[pltpu.VMEM((S, POOL_HIST + DEC_SEQ, D_POOL), F32)],
        compiler_params=_cparams(("arbitrary",)),
        name="pool_sample",
    )(u3, hist, pool_w, pool_scale)


def _prompt_attn_block(i, q_ref, kt_ref, v_ref, wuv_ref, o_ref, m_ref, l_ref, acc_ref, sa_ref, sb_ref, heads):
    rows_all = heads * Q_BLOCK
    m_ref[...] = jnp.full((rows_all, 1), -jnp.inf, F32)
    l_ref[...] = jnp.zeros((rows_all, 1), F32)
    acc_ref[...] = jnp.zeros((rows_all, KV_LORA), F32)
    q = q_ref[0].reshape(rows_all, D_QK)
    q_pos = i * Q_BLOCK + (lax.broadcasted_iota(jnp.int32, (rows_all, KV_CHUNK), 0) & (Q_BLOCK - 1))
    key = lax.broadcasted_iota(jnp.int32, (rows_all, KV_CHUNK), 1)

    def scores(j, s_ref):
        s = _dot(q, kt_ref[j])
        s_ref[...] = jnp.where(j * KV_CHUNK + key <= q_pos, s, -jnp.inf)

    def attend(j, s_ref):
        v = v_ref[pl.ds(pl.multiple_of(j * KV_CHUNK, KV_CHUNK), KV_CHUNK), :]
        m_new, l_new, acc_new = _softmax_update(m_ref[...], l_ref[...], acc_ref[...], s_ref[...], v)
        m_ref[...] = m_new
        l_ref[...] = l_new
        acc_ref[...] = acc_new

    n_chunks = (i * Q_BLOCK + Q_BLOCK + KV_CHUNK - 1) // KV_CHUNK
    n_pairs = (n_chunks - 1) // 2
    scores(0, sa_ref)

    def body(t, c):
        scores(2 * t + 1, sb_ref)
        attend(2 * t, sa_ref)
        scores(2 * t + 2, sa_ref)
        attend(2 * t + 1, sb_ref)
        return c

    lax.fori_loop(0, n_pairs, body, 0)
    done = 2 * n_pairs

    @pl.when(done + 2 == n_chunks)
    def _():
        scores(done + 1, sb_ref)
        attend(done, sa_ref)
        attend(done + 1, sb_ref)

    @pl.when(done + 1 == n_chunks)
    def _():
        attend(done, sa_ref)

    for h in range(heads):
        rows = slice(h * Q_BLOCK, (h + 1) * Q_BLOCK)
        o_lat = (acc_ref[rows] * (1.0 / l_ref[rows])).astype(BF16)
        o_ref[:, h * V_HEAD:(h + 1) * V_HEAD] = _dot(o_lat, wuv_ref[h]).astype(o_ref.dtype)


PAGES_PER_CHUNK = 16
CHUNK_KEYS = PAGES_PER_CHUNK * PAGE_SIZE
CHUNKS_PER_SEQ = N_PAGES // PAGES_PER_CHUNK
DEC_SLOTS = 4
DEC_AHEAD = DEC_SLOTS - 1
NEW_KEYS_PAD = 16
ATTN_HEADS_PER_STEP = N_HEADS // 2
BLOCKS_PER_SEQ = SEQ // Q_BLOCK
assert CHUNKS_PER_SEQ % DEC_SLOTS == 0
assert DEC_BATCH * ATTN_HEADS_PER_STEP == BATCH * BLOCKS_PER_SEQ * N_HEADS


def _prompt_block(k):
    b, r = k // BLOCKS_PER_SEQ, k % BLOCKS_PER_SEQ
    return b, jnp.where(r % 2 == 0, r // 2, BLOCKS_PER_SEQ - 1 - r // 2)


def _attn_kernel(pt_ref, q_ref, ckvn_ref, kpen_ref, cckv_hbm, ckpet_hbm, qp_ref, kt_ref, v_ref, wuv_ref,
                 o_ref, op_ref, ckv_buf, kpet_buf, kvb_a, kvb_b, sc_a, sc_b, sem,
                 m_ref, l_ref, acc_ref, sa_ref, sb_ref):
    s = pl.program_id(0)
    nseq = pl.num_programs(0)
    rows = N_HEADS * DEC_SEQ

    def page_copies(seq, c, p):
        slot = c % DEC_SLOTS
        page = pt_ref[(seq * CHUNKS_PER_SEQ + c) * PAGES_PER_CHUNK + p]
        keys = pl.ds(p * PAGE_SIZE, PAGE_SIZE)
        return (pltpu.make_async_copy(cckv_hbm.at[page], ckv_buf.at[slot, keys, :], sem.at[0, slot]),
                pltpu.make_async_copy(ckpet_hbm.at[page], kpet_buf.at[slot, :, keys], sem.at[1, slot]))

    def start_chunk(seq, c):
        for p in range(PAGES_PER_CHUNK):
            for cp in page_copies(seq, c, p):
                cp.start()

    def wait_chunk(seq, c):
        for p in range(PAGES_PER_CHUNK):
            for cp in page_copies(seq, c, p):
                cp.wait()

    @pl.when(s == 0)
    def _():
        for c in range(DEC_AHEAD):
            start_chunk(0, c)

    q = q_ref[0].reshape(rows, D_QK)
    q_lat = q[:, :KV_LORA].astype(BF16)
    q_pe = q[:, KV_LORA:].astype(BF16)

    def scores(c, kvb_ref, sc_ref):
        nxt = c + DEC_AHEAD
        if nxt < CHUNKS_PER_SEQ:
            start_chunk(s, nxt)
        else:
            @pl.when(s + 1 < nseq)
            def _():
                start_chunk(s + 1, nxt - CHUNKS_PER_SEQ)
        wait_chunk(s, c)
        slot = c % DEC_SLOTS
        kvb_ref[...] = ckv_buf[slot].astype(BF16)
        sc_ref[...] = _dot_nt(q_lat, kvb_ref[...]) + _dot(q_pe, kpet_buf[slot].astype(BF16))

    def attend(carry, kvb_ref, sc_ref):
        return _softmax_update(*carry, sc_ref[...], kvb_ref[...])

    bufs = ((kvb_a, sc_a), (kvb_b, sc_b))
    carry = (jnp.full((rows, 1), -jnp.inf, F32), jnp.zeros((rows, 1), F32), jnp.zeros((rows, KV_LORA), F32))
    scores(0, *bufs[0])
    for c in range(1, CHUNKS_PER_SEQ):
        scores(c, *bufs[c % 2])
        carry = attend(carry, *bufs[(c - 1) % 2])
        if c == CHUNKS_PER_SEQ // 2:
            _, blk = _prompt_block(s // 2)
            _prompt_attn_block(blk, qp_ref, kt_ref, v_ref, wuv_ref, op_ref, m_ref, l_ref, acc_ref,
                               sa_ref, sb_ref, ATTN_HEADS_PER_STEP)
    carry = attend(carry, *bufs[(CHUNKS_PER_SEQ - 1) % 2])

    pad = NEW_KEYS_PAD - DEC_SEQ
    kvn = jnp.concatenate([ckvn_ref[...], jnp.zeros((pad, KV_LORA), F32)], axis=0).astype(BF16)
    kpn = jnp.concatenate([kpen_ref[...], jnp.zeros((pad, QK_ROPE), F32)], axis=0).astype(BF16)
    sc = _dot_nt(q_lat, kvn) + _dot_nt(q_pe, kpn)
    tok = lax.broadcasted_iota(jnp.int32, (rows, NEW_KEYS_PAD), 0) & (DEC_SEQ - 1)
    key = lax.broadcasted_iota(jnp.int32, (rows, NEW_KEYS_PAD), 1)
    sc = jnp.where(key <= tok, sc, -jnp.inf)
    _, l_fin, acc = _softmax_update(*carry, sc, kvn)
    o_ref[0] = (acc * (1.0 / l_fin)).reshape(N_HEADS, DEC_SEQ, KV_LORA)


def _attention(page_table, qs, ckv_s, kpe_s, cache_ckv, cache_kpet, qp, kt, vb, wuv):
    seq_per_blk = Q_BLOCK // DEC_SEQ
    hps = ATTN_HEADS_PER_STEP
    halves = N_HEADS // hps
    cps = SEQ // KV_CHUNK
    rows_p = hps * Q_BLOCK

    def qblock(s):
        b, i = _prompt_block(s // halves)
        return b * BLOCKS_PER_SEQ + i

    grid_spec = pltpu.PrefetchScalarGridSpec(
        num_scalar_prefetch=1,
        grid=(DEC_BATCH,),
        in_specs=[
            pl.BlockSpec((1, N_HEADS, DEC_SEQ, D_QK), lambda s, pt: (s // seq_per_blk, 0, s % seq_per_blk, 0)),
            pl.BlockSpec((DEC_SEQ, KV_LORA), lambda s, pt: (s, 0)),
            pl.BlockSpec((DEC_SEQ, QK_ROPE), lambda s, pt: (s, 0)),
            pl.BlockSpec(memory_space=pl.ANY),
            pl.BlockSpec(memory_space=pl.ANY),
            pl.BlockSpec((1, hps, Q_BLOCK, D_QK), lambda s, pt: (qblock(s), s % halves, 0, 0)),
            pl.BlockSpec((cps, D_QK, KV_CHUNK), lambda s, pt: (_prompt_block(s // halves)[0], 0, 0)),
            pl.BlockSpec((SEQ, KV_LORA), lambda s, pt: (_prompt_block(s // halves)[0], 0)),
            pl.BlockSpec((hps, KV_LORA, V_HEAD), lambda s, pt: (s % halves, 0, 0)),
        ],
        out_specs=[
            pl.BlockSpec((1, N_HEADS, DEC_SEQ, KV_LORA), lambda s, pt: (s // seq_per_blk, 0, s % seq_per_blk, 0)),
            pl.BlockSpec((Q_BLOCK, hps * V_HEAD), lambda s, pt: (qblock(s), s % halves)),
        ],
        scratch_shapes=[
            pltpu.VMEM((DEC_SLOTS, CHUNK_KEYS, KV_LORA), F32),
            pltpu.VMEM((DEC_SLOTS, QK_ROPE, CHUNK_KEYS), F32),
            pltpu.VMEM((CHUNK_KEYS, KV_LORA), BF16),
            pltpu.VMEM((CHUNK_KEYS, KV_LORA), BF16),
            pltpu.VMEM((N_HEADS * DEC_SEQ, CHUNK_KEYS), F32),
            pltpu.VMEM((N_HEADS * DEC_SEQ, CHUNK_KEYS), F32),
            pltpu.SemaphoreType.DMA((2, DEC_SLOTS)),
            pltpu.VMEM((rows_p, 1), F32),
            pltpu.VMEM((rows_p, 1), F32),
            pltpu.VMEM((rows_p, KV_LORA), F32),
            pltpu.VMEM((rows_p, KV_CHUNK), F32),
            pltpu.VMEM((rows_p, KV_CHUNK), F32),
        ],
    )
    return pl.pallas_call(
        _attn_kernel,
        grid_spec=grid_spec,
        out_shape=[
            jax.ShapeDtypeStruct((N_SAMPLE // Q_BLOCK, N_HEADS, Q_BLOCK, KV_LORA), F32),
            jax.ShapeDtypeStruct((N_PROMPT, D_MLA_OUT), BF16),
        ],
        compiler_params=_cparams(("arbitrary",)),
        name="attention",
    )(page_table.reshape(-1), qs, ckv_s, kpe_s, cache_ckv, cache_kpet, qp, kt, vb, wuv)


def _uv_proj_kernel(ol_ref, wuv_ref, o_ref, *, nsub):
    for j in range(nsub):
        for h in range(N_HEADS):
            o = _dot(ol_ref[j, h].astype(BF16), wuv_ref[h])
            o_ref[j * Q_BLOCK:(j + 1) * Q_BLOCK, h * V_HEAD:(h + 1) * V_HEAD] = o.astype(o_ref.dtype)


def _uv_proj_sample(o_lat, wuv, nsub=4):
    nb = o_lat.shape[0]
    rows = nsub * Q_BLOCK
    return pl.pallas_call(
        functools.partial(_uv_proj_kernel, nsub=nsub),
        grid=(nb // nsub,),
        in_specs=[
            pl.BlockSpec((nsub, N_HEADS, Q_BLOCK, KV_LORA), lambda i: (i, 0, 0, 0)),
            _resident((N_HEADS, KV_LORA, V_HEAD)),
        ],
        out_specs=pl.BlockSpec((rows, D_MLA_OUT), lambda i: (i, 0)),
        out_shape=jax.ShapeDtypeStruct((nb * Q_BLOCK, D_MLA_OUT), BF16),
        compiler_params=_cparams(("arbitrary",)),
        name="uv_proj_sample",
    )(o_lat, wuv)


def _mm_ln_kernel(*refs, lhs_split, split_resid, split_out, tm):
    is_prompt = pl.program_id(0) < N_PROMPT // tm
    refs = list(refs)

    def take(split):
        if split:
            p_ref, s_ref = refs.pop(0), refs.pop(0)
            return lambda rows: jnp.where(is_prompt, p_ref[rows, :], s_ref[rows, :])
        ref = refs.pop(0)
        return lambda rows: ref[rows, :]

    lhs = [take(split) for split in lhs_split]
    w_ref = refs.pop(0)
    resid = take(split_resid)
    g_ref, b_ref = refs.pop(0), refs.pop(0)
    out_refs = refs

    sub = min(tm, FFN_SUB_ROWS)
    for r in range(0, tm, sub):
        rows = slice(r, r + sub)
        acc = None
        off = 0
        for piece in lhs:
            a = piece(rows)
            part = _dot(a, w_ref[off:off + a.shape[1], :])
            acc = part if acc is None else acc + part
            off += a.shape[1]
        y = _layer_norm(DEEPNORM_ALPHA * resid(rows) + acc, g_ref[...], b_ref[...])
        if split_out:
            @pl.when(is_prompt)
            def _():
                out_refs[0][rows, :] = y

            @pl.when(jnp.logical_not(is_prompt))
            def _():
                out_refs[1][rows, :] = y
        else:
            out_refs[0][rows, :] = y
            out_refs[1][rows, :] = y.astype(BF16)


def _mm_ln(lhs_list, w, resid, g, b, *, tm, split_out=False, layer=None):
    operands, in_specs = [], []

    def add_rows(v):
        if isinstance(v, tuple):
            operands.extend(v)
            in_specs.extend([_prompt_rows(tm, v[0].shape[1]), _sample_rows(tm, v[0].shape[1])])
        else:
            operands.append(v)
            in_specs.append(pl.BlockSpec((tm, v.shape[1]), lambda i: (i, 0)))
        return isinstance(v, tuple)

    lhs_split = tuple(add_rows(v) for v in lhs_list)
    operands.append(w)
    if layer is None:
        in_specs.append(_resident(w.shape))
    else:
        in_specs.append(pl.BlockSpec((None,) + w.shape[1:], lambda i: (layer, 0, 0), pipeline_mode=pl.Buffered(1)))
    split_resid = add_rows(resid)
    operands.extend([g, b])
    in_specs.extend([_resident((1, D_MODEL)), _resident((1, D_MODEL))])
    if split_out:
        out_specs = [_prompt_rows(tm, D_MODEL), _sample_rows(tm, D_MODEL)]
        out_shape = [jax.ShapeDtypeStruct((N_PROMPT, D_MODEL), F32), jax.ShapeDtypeStruct((N_SAMPLE, D_MODEL), F32)]
    else:
        row = pl.BlockSpec((tm, D_MODEL), lambda i: (i, 0))
        out_specs = [row, row]
        out_shape = [jax.ShapeDtypeStruct((N_TOK, D_MODEL), F32), jax.ShapeDtypeStruct((N_TOK, D_MODEL), BF16)]
    return pl.pallas_call(
        functools.partial(_mm_ln_kernel, lhs_split=lhs_split, split_resid=split_resid, split_out=split_out, tm=tm),
        grid=(N_TOK // tm,),
        in_specs=in_specs,
        out_specs=out_specs,
        out_shape=out_shape,
        compiler_params=_cparams(("arbitrary",)),
        name="mm_ln",
    )(*operands)


def _ffn_up_kernel(x_ref, wg_ref, wu_ref, h_ref, wgb_ref, wub_ref):
    @pl.when(pl.program_id(1) == 0)
    def _():
        wgb_ref[...] = wg_ref[...].astype(BF16)
        wub_ref[...] = wu_ref[...].astype(BF16)

    for r in range(0, x_ref.shape[0], FFN_SUB_ROWS):
        x = x_ref[r:r + FFN_SUB_ROWS, :]
        h = jax.nn.silu(_dot(x, wgb_ref[...])) * _dot(x, wub_ref[...])
        h_ref[r:r + FFN_SUB_ROWS, :] = h.astype(h_ref.dtype)


def _ffn_up(xb, wg, wu, layer, tm=N_TOK // 4, tn=512):
    wblk = pl.BlockSpec((None, D_MODEL, tn), lambda j, i: (layer, 0, j))
    return pl.pallas_call(
        _ffn_up_kernel,
        grid=(D_FF // tn, N_TOK // tm),
        in_specs=[pl.BlockSpec((tm, D_MODEL), lambda j, i: (i, 0)), wblk, wblk],
        out_specs=pl.BlockSpec((tm, tn), lambda j, i: (i, j)),
        out_shape=jax.ShapeDtypeStruct((N_TOK, D_FF), BF16),
        scratch_shapes=[pltpu.VMEM((D_MODEL, tn), BF16), pltpu.VMEM((D_MODEL, tn), BF16)],
        compiler_params=_cparams(("arbitrary", "arbitrary")),
        name="ffn_up",
    )(xb, wg, wu)


def _conv_taps(read, cw_ref):
    conv = read(CONV_W - 1) * cw_ref[0:1, :]
    for j in range(1, CONV_W):
        conv = conv + read(CONV_W - 1 - j) * cw_ref[j:j + 1, :]
    return conv


def _in_odd_cast_weights(wv_ref, wb_ref, wc_ref, wvb_ref, wbb_ref, wcb_ref):
    @pl.when(pl.program_id(1) == 0)
    def _():
        wvb_ref[...] = wv_ref[...].astype(BF16)
        wbb_ref[...] = wb_ref[...].astype(BF16)
        wcb_ref[...] = wc_ref[...].astype(BF16)


def _in_odd_prompt_kernel(x_ref, wv_ref, wb_ref, wc_ref, cw_ref, g_ref, ztail_ref,
                          wvb_ref, wbb_ref, wcb_ref, zp_ref, *, tm):
    i = pl.program_id(1)
    tiles_per_seq = SEQ // tm
    _in_odd_cast_weights(wv_ref, wb_ref, wc_ref, wvb_ref, wbb_ref, wcb_ref)

    @pl.when(i % tiles_per_seq == 0)
    def _():
        zp_ref[0:CONV_HIST, :] = jnp.zeros((CONV_HIST, zp_ref.shape[1]), F32)

    for r in range(0, tm, FFN_SUB_ROWS):
        x = x_ref[r:r + FFN_SUB_ROWS, :]
        new = slice(CONV_HIST + r, CONV_HIST + r + FFN_SUB_ROWS)
        zp_ref[new, :] = _dot(x, wcb_ref[...]) * _dot(x, wvb_ref[...])
        conv = _conv_taps(lambda back: zp_ref[new.start - back:new.stop - back, :], cw_ref)
        g_ref[r:r + FFN_SUB_ROWS, :] = (_dot(x, wbb_ref[...]) * conv).astype(g_ref.dtype)
    zp_ref[0:CONV_HIST, :] = zp_ref[tm:tm + CONV_HIST, :]

    @pl.when(i % tiles_per_seq == tiles_per_seq - 1)
    def _():
        ztail_ref[...] = zp_ref[tm:tm + CONV_HIST, :]


def _in_odd_sample_kernel(x_ref, wv_ref, wb_ref, wc_ref, zs_ref, gbs_ref, wvb_ref, wbb_ref, wcb_ref):
    _in_odd_cast_weights(wv_ref, wb_ref, wc_ref, wvb_ref, wbb_ref, wcb_ref)
    x = x_ref[...]
    zs_ref[...] = _dot(x, wcb_ref[...]) * _dot(x, wvb_ref[...])
    gbs_ref[...] = _dot(x, wbb_ref[...])


def _in_odd(xb, w_in, conv_w, tm=1024, tn=512):
    nb = D_MODEL // tn
    nbp = N_PROMPT // tm
    tiles_per_seq = SEQ // tm
    wblks = [pl.BlockSpec((D_MODEL, tn), lambda j, i, part=part: (0, part * nb + j)) for part in range(3)]
    wscratch = [pltpu.VMEM((D_MODEL, tn), BF16)] * 3
    g_prompt, z_tail = pl.pallas_call(
        functools.partial(_in_odd_prompt_kernel, tm=tm),
        grid=(nb, nbp),
        in_specs=[pl.BlockSpec((tm, D_MODEL), lambda j, i: (i, 0))] + wblks
        + [pl.BlockSpec((CONV_W, tn), lambda j, i: (0, j))],
        out_specs=[
            pl.BlockSpec((tm, tn), lambda j, i: (i, j)),
            pl.BlockSpec((CONV_HIST, tn), lambda j, i: (i // tiles_per_seq, j)),
        ],
        out_shape=[
            jax.ShapeDtypeStruct((N_PROMPT, D_MODEL), BF16),
            jax.ShapeDtypeStruct((BATCH * CONV_HIST, D_MODEL), F32),
        ],
        scratch_shapes=wscratch + [pltpu.VMEM((tm + CONV_HIST, tn), F32)],
        compiler_params=_cparams(("arbitrary", "arbitrary")),
        name="in_odd_prompt",
    )(xb, w_in, w_in, w_in, conv_w)
    sample_blk = pl.BlockSpec((tm, tn), lambda j, i: (i, j))
    z_s, gb_s = pl.pallas_call(
        _in_odd_sample_kernel,
        grid=(nb, N_SAMPLE // tm),
        in_specs=[pl.BlockSpec((tm, D_MODEL), lambda j, i: (nbp + i, 0))] + wblks,
        out_specs=[sample_blk, sample_blk],
        out_shape=[jax.ShapeDtypeStruct((N_SAMPLE, D_MODEL), F32)] * 2,
        scratch_shapes=wscratch,
        compiler_params=_cparams(("arbitrary", "arbitrary")),
        name="in_odd_sample",
    )(xb, w_in, w_in, w_in)
    return g_prompt, z_tail, z_s, gb_s


def _conv_sample_kernel(z_ref, hist_ref, gb_ref, cw_ref, g_ref, zp_ref, *, S):
    zp_ref[:, 0:CONV_HIST, :] = hist_ref[...]
    zp_ref[:, CONV_HIST:, :] = z_ref[...]
    conv = _conv_taps(lambda back: zp_ref[:, CONV_HIST - back:CONV_HIST - back + DEC_SEQ, :], cw_ref)
    g_ref[...] = (gb_ref[...] * conv).reshape(S * DEC_SEQ, D_MODEL).astype(g_ref.dtype)


def _conv_sample(z3, hist, gb3, conv_w, S=32):
    blk = lambda i: (i, 0, 0)
    rows = S * DEC_SEQ
    return pl.pallas_call(
        functools.partial(_conv_sample_kernel, S=S),
        grid=(DEC_BATCH // S,),
        in_specs=[
            pl.BlockSpec((S, DEC_SEQ, D_MODEL), blk),
            pl.BlockSpec((S, CONV_HIST, D_MODEL), blk),
            pl.BlockSpec((S, DEC_SEQ, D_MODEL), blk),
            _resident((CONV_W, D_MODEL)),
        ],
        out_specs=pl.BlockSpec((rows, D_MODEL), lambda i: (i, 0)),
        out_shape=jax.ShapeDtypeStruct((N_SAMPLE, D_MODEL), BF16),
        scratch_shapes=[pltpu.VMEM((S, CONV_HIST + DEC_SEQ, D_MODEL), F32)],
        compiler_params=_cparams(("arbitrary",)),
        name="conv_sample",
    )(z3, hist, gb3, conv_w)


def _rope_tables():
    pos = jnp.concatenate([jnp.tile(jnp.arange(SEQ), BATCH),
                           jnp.tile(PAST_LEN + jnp.arange(DEC_SEQ), DEC_BATCH)])
    inv = ROPE_THETA ** (-jnp.arange(0, QK_ROPE, 2, dtype=F32) / QK_ROPE)
    ang = pos.astype(F32)[:, None] * inv[None, :]
    cos, sin = jnp.cos(ang), jnp.sin(ang)
    cos2 = jnp.concatenate([cos, cos], axis=1)
    sin2 = jnp.concatenate([-sin, sin], axis=1)
    return jnp.tile(cos2, (1, 2)), jnp.tile(sin2, (1, 2)), jnp.concatenate([cos2, sin2], axis=1)


def _swap_halves(w):
    half = QK_ROPE // 2
    return jnp.concatenate([w[..., half:], w[..., :half]], axis=-1)


def kernel(x_prompt, x_sample, cache_ckv, cache_kpe, state_pool, state_conv, page_table, w_in_even, pool_w, pool_scale, q_norm_g, w_q_b, kv_norm_g, w_uk, w_uv, w_out_even, w_in_odd, conv_w, w_out_odd, ffn_w_gate, ffn_w_up, ffn_w_down, ln1_g, ln1_b, ln2_g, ln2_b):
    assert DEPTH == 2
    n_pool_pages = cache_ckv.shape[1]
    cos, sin, cos_sin = _rope_tables()
    xp = x_prompt.reshape(N_PROMPT, D_MODEL)
    xs = x_sample.reshape(N_SAMPLE, D_MODEL)

    def row(v):
        return v.reshape(1, -1)

    w_down = ffn_w_down.astype(BF16)

    def ffn(layer, xf, xb, last):
        hmid = _ffn_up(xb, ffn_w_gate, ffn_w_up, layer)
        return _mm_ln([hmid], w_down, xf, row(ln2_g[layer]), row(ln2_b[layer]),
                      tm=256, split_out=last, layer=layer)

    e = 0
    w_in = w_in_even[e]
    w_all = jnp.concatenate([w_in, _swap_halves(w_in[:, D_POOL + Q_LORA + KV_LORA:])], axis=1).astype(BF16)
    u, cq, ckv_p, ckv_s, kpe_p, kpe_s, vb, kt = _in_even(
        xp, xs, w_all, row(q_norm_g[e]), row(kv_norm_g[e]), cos_sin)

    wq = w_q_b[e]
    wq_rope = wq[:, :, QK_NOPE:]
    qp, qs = _q_proj(
        cq,
        wq[:, :, :QK_NOPE].reshape(Q_LORA, N_HEADS * QK_NOPE).astype(BF16),
        wq_rope.reshape(Q_LORA, N_HEADS * QK_ROPE).astype(BF16),
        _swap_halves(wq_rope).reshape(Q_LORA, N_HEADS * QK_ROPE).astype(BF16),
        jnp.transpose(w_uk[e], (1, 2, 0)).astype(BF16), cos, sin)

    pw = pool_w[e].astype(BF16)
    ps = row(pool_scale[e])
    u_sample = u[N_PROMPT:].reshape(DEC_BATCH, DEC_SEQ, D_POOL)
    pool_hist = jnp.pad(state_pool[e], ((0, 0), (POOL_HIST - POOL_BUF, 0), (0, 0)))
    y_pool = (_pool_prompt(u, pw, ps), _pool_sample(u_sample, pool_hist, pw, ps))

    wuv = jnp.transpose(w_uv[e], (1, 0, 2)).astype(BF16)
    o_lat_s, o_prompt = _attention(page_table, qs, ckv_s, kpe_s,
                                   cache_ckv[e].reshape(n_pool_pages, PAGE_SIZE, KV_LORA),
                                   jnp.swapaxes(cache_kpe[e], 1, 2), qp, kt, vb, wuv)
    o_sample = _uv_proj_sample(o_lat_s, wuv)
    xf, xb = _mm_ln([y_pool, (o_prompt, o_sample)], w_out_even[e].astype(BF16), (xp, xs),
                    row(ln1_g[0]), row(ln1_b[0]), tm=512)
    xf, xb = ffn(0, xf, xb, False)

    od = 0
    g_prompt, z_tail, z_s, gb_s = _in_odd(xb, w_in_odd[od], conv_w[od])
    z_sample = z_s.reshape(DEC_BATCH, DEC_SEQ, D_MODEL)
    gb_sample = gb_s.reshape(DEC_BATCH, DEC_SEQ, D_MODEL)
    conv_hist = jnp.pad(state_conv[od], ((0, 0), (CONV_HIST - CONV_BUF, 0), (0, 0)))
    g = (g_prompt, _conv_sample(z_sample, conv_hist, gb_sample, conv_w[od]))
    xf, xb = _mm_ln([g], w_out_odd[od].astype(BF16), xf, row(ln1_g[1]), row(ln1_b[1]), tm=512)
    y_p, y_s = ffn(1, xf, xb, True)

    pool_prompt = jnp.stack([u[(b + 1) * SEQ - POOL_BUF:(b + 1) * SEQ] for b in range(BATCH)])
    pool_sample = jnp.concatenate([state_pool[e][:, DEC_SEQ:], u_sample], axis=1)
    conv_prompt = z_tail.reshape(BATCH, CONV_HIST, D_MODEL)[:, CONV_HIST - CONV_BUF:]
    conv_sample = z_sample[:, DEC_SEQ - CONV_BUF:]
    return (y_p.reshape(BATCH, SEQ, D_MODEL), y_s.reshape(DEC_BATCH, DEC_SEQ, D_MODEL),
            ckv_p.reshape(1, BATCH, SEQ, KV_LORA), kpe_p.reshape(1, BATCH, SEQ, QK_ROPE),
            pool_prompt[None], conv_prompt[None],
            ckv_s.reshape(1, DEC_BATCH, DEC_SEQ, KV_LORA), kpe_s.reshape(1, DEC_BATCH, DEC_SEQ, QK_ROPE),
            pool_sample[None], conv_sample[None])
```

```python
import functools

import jax
import jax.numpy as jnp
from jax import lax
from jax.experimental import pallas as pl
from jax.experimental.pallas import tpu as pltpu

F32 = jnp.float32
BF16 = jnp.bfloat16

D_MODEL = 2048
BATCH = 4
SEQ = 2048
DEPTH = 2
DEC_BATCH = 128
DEC_SEQ = 8
PAST_LEN = 16384
PAGE_SIZE = 128
N_PAGES = PAST_LEN // PAGE_SIZE
POOL_WINDOWS = (2, 4, 8, 16)
N_POOL_GROUPS = len(POOL_WINDOWS)
D_POOL = D_MODEL // 2
POOL_GROUP = D_POOL // N_POOL_GROUPS
POOL_BUF = max(POOL_WINDOWS) - 1
N_HEADS = 8
QK_NOPE = 128
QK_ROPE = 64
V_HEAD = 128
Q_LORA = 512
KV_LORA = 512
ROPE_THETA = 10000.0
ATTN_SCALE = (QK_NOPE + QK_ROPE) ** -0.5
Q_BLOCK = 128
D_MLA_OUT = N_HEADS * V_HEAD
CONV_W = 3
CONV_BUF = CONV_W - 1
D_FF = ((8 * D_MODEL + 3 * 256 - 1) // (3 * 256)) * 256
DEEPNORM_ALPHA = (2 * DEPTH) ** 0.25

N_PROMPT = BATCH * SEQ
N_SAMPLE = DEC_BATCH * DEC_SEQ
N_TOK = N_PROMPT + N_SAMPLE
D_QK = KV_LORA + QK_ROPE
POOL_HIST = 16
CONV_HIST = 8
KV_CHUNK = 512
FFN_SUB_ROWS = 256
VMEM_LIMIT_MB = 56


def _cparams(dims, vmem_mb=VMEM_LIMIT_MB):
    return pltpu.CompilerParams(dimension_semantics=dims, vmem_limit_bytes=vmem_mb * 1024 * 1024)


def _resident(shape):
    zeros = (0,) * len(shape)
    return pl.BlockSpec(shape, lambda *_: zeros, pipeline_mode=pl.Buffered(1))


def _prompt_rows(tm, width):
    last = N_PROMPT // tm - 1
    return pl.BlockSpec((tm, width), lambda i, *_: (jnp.minimum(i, last), 0))


def _sample_rows(tm, width):
    first = N_PROMPT // tm
    return pl.BlockSpec((tm, width), lambda i, *_: (jnp.maximum(i - first, 0), 0))


def _dot(a, b):
    return jnp.dot(a, b, preferred_element_type=F32)


def _dot_nt(a, b):
    return lax.dot_general(a, b, (((1,), (1,)), ((), ())), preferred_element_type=F32)


def _rms(h, g, eps=1e-6):
    return h * lax.rsqrt(jnp.mean(h * h, axis=-1, keepdims=True) + eps) * g


def _layer_norm(v, g, b, eps=1e-5):
    mu = jnp.mean(v, axis=-1, keepdims=True)
    vc = v - mu
    var = jnp.mean(vc * vc, axis=-1, keepdims=True)
    return vc * lax.rsqrt(var + eps) * g + b


def _softmax_update(m_old, l_old, acc, s, v):
    m_new = jnp.maximum(m_old, jnp.max(s, axis=-1, keepdims=True))
    alpha = jnp.exp(m_old - m_new)
    p = jnp.exp(s - m_new)
    l_new = alpha * l_old + jnp.sum(p, axis=-1, keepdims=True)
    return m_new, l_new, alpha * acc + _dot(p.astype(BF16), v)


D_IN_EVEN = D_POOL + Q_LORA + KV_LORA + 2 * QK_ROPE


def _in_even_kernel(xp_ref, xs_ref, w_ref, qg_ref, kvg_ref, cs_ref,
                    u_ref, cq_ref, ckvp_ref, ckvs_ref, kpep_ref, kpes_ref, vb_ref, kt_ref, *, tm):
    i = pl.program_id(0)
    is_prompt = i < N_PROMPT // tm
    xb = jnp.where(is_prompt, xp_ref[...], xs_ref[...]).astype(BF16)
    h = _dot(xb, w_ref[...])
    o1, o2 = D_POOL + Q_LORA, D_POOL + Q_LORA + KV_LORA
    u_ref[...] = h[:, :D_POOL]
    cq_ref[...] = _rms(h[:, D_POOL:o1], qg_ref[...]).astype(BF16)
    ckv = _rms(h[:, o1:o2], kvg_ref[...])
    prod = h[:, o2:] * cs_ref[...]
    kpe2 = prod + pltpu.roll(prod, QK_ROPE, 1)
    kpe = kpe2[:, :QK_ROPE]

    @pl.when(is_prompt)
    def _():
        ckvp_ref[...] = ckv
        kpep_ref[...] = kpe
        vb_ref[...] = ckv.astype(BF16)
        for c in range(tm // KV_CHUNK):
            rows = slice(c * KV_CHUNK, (c + 1) * KV_CHUNK)
            kt_ref[c, :KV_LORA, :] = ckv[rows, :].T.astype(BF16)
            kt_ref[c, KV_LORA:, :] = kpe2[rows, :].T[:QK_ROPE, :].astype(BF16)

    @pl.when(jnp.logical_not(is_prompt))
    def _():
        ckvs_ref[...] = ckv
        kpes_ref[...] = kpe


def _in_even(xp, xs, w_all, qg, kvg, cs, tm=512):
    row = lambda i: (i, 0)
    cpb = tm // KV_CHUNK
    last = N_PROMPT // tm - 1
    return pl.pallas_call(
        functools.partial(_in_even_kernel, tm=tm),
        grid=(N_TOK // tm,),
        in_specs=[
            _prompt_rows(tm, D_MODEL),
            _sample_rows(tm, D_MODEL),
            _resident((D_MODEL, D_IN_EVEN)),
            _resident((1, Q_LORA)),
            _resident((1, KV_LORA)),
            pl.BlockSpec((tm, 2 * QK_ROPE), row),
        ],
        out_specs=[
            pl.BlockSpec((tm, D_POOL), row),
            pl.BlockSpec((tm, Q_LORA), row),
            _prompt_rows(tm, KV_LORA),
            _sample_rows(tm, KV_LORA),
            _prompt_rows(tm, QK_ROPE),
            _sample_rows(tm, QK_ROPE),
            _prompt_rows(tm, KV_LORA),
            pl.BlockSpec((cpb, D_QK, KV_CHUNK), lambda i: (jnp.minimum(i, last), 0, 0)),
        ],
        out_shape=[
            jax.ShapeDtypeStruct((N_TOK, D_POOL), F32),
            jax.ShapeDtypeStruct((N_TOK, Q_LORA), BF16),
            jax.ShapeDtypeStruct((N_PROMPT, KV_LORA), F32),
            jax.ShapeDtypeStruct((N_SAMPLE, KV_LORA), F32),
            jax.ShapeDtypeStruct((N_PROMPT, QK_ROPE), F32),
            jax.ShapeDtypeStruct((N_SAMPLE, QK_ROPE), F32),
            jax.ShapeDtypeStruct((N_PROMPT, KV_LORA), BF16),
            jax.ShapeDtypeStruct((N_PROMPT // KV_CHUNK, D_QK, KV_CHUNK), BF16),
        ],
        compiler_params=_cparams(("arbitrary",)),
        name="in_even",
    )(xp, xs, w_all, qg, kvg, cs)


def _q_proj_kernel(cq_ref, wqn_ref, wqr_ref, wqrr_ref, wukt_ref, cos_ref, sin_ref, qp_ref, qs_ref, *, nsub):
    i = pl.program_id(0)
    is_prompt = i < N_PROMPT // (nsub * Q_BLOCK)
    cq = cq_ref[...]
    qn = _dot(cq, wqn_ref[...])
    cos = jnp.concatenate([cos_ref[...]] * (N_HEADS // 2), axis=1)
    sin = jnp.concatenate([sin_ref[...]] * (N_HEADS // 2), axis=1)
    qpe = (_dot(cq, wqr_ref[...]) * cos + _dot(cq, wqrr_ref[...]) * sin) * ATTN_SCALE

    def emit(q_ref):
        for h in range(N_HEADS):
            ql = _dot(qn[:, h * QK_NOPE:(h + 1) * QK_NOPE].astype(BF16), wukt_ref[h]) * ATTN_SCALE
            for j in range(nsub):
                rows = slice(j * Q_BLOCK, (j + 1) * Q_BLOCK)
                q_ref[j, h, :, :KV_LORA] = ql[rows].astype(q_ref.dtype)
                q_ref[j, h, :, KV_LORA:] = qpe[rows, h * QK_ROPE:(h + 1) * QK_ROPE].astype(q_ref.dtype)

    @pl.when(is_prompt)
    def _():
        emit(qp_ref)

    @pl.when(jnp.logical_not(is_prompt))
    def _():
        emit(qs_ref)


def _q_proj(cq, wqn, wqr, wqrr, wukt, cos, sin, tm=256):
    nsub = tm // Q_BLOCK
    nbp = N_PROMPT // tm
    row = lambda i: (i, 0)
    blk = (nsub, N_HEADS, Q_BLOCK, D_QK)
    return pl.pallas_call(
        functools.partial(_q_proj_kernel, nsub=nsub),
        grid=(N_TOK // tm,),
        in_specs=[
            pl.BlockSpec((tm, Q_LORA), row),
            _resident((Q_LORA, N_HEADS * QK_NOPE)),
            _resident((Q_LORA, N_HEADS * QK_ROPE)),
            _resident((Q_LORA, N_HEADS * QK_ROPE)),
            _resident((N_HEADS, QK_NOPE, KV_LORA)),
            pl.BlockSpec((tm, 2 * QK_ROPE), row),
            pl.BlockSpec((tm, 2 * QK_ROPE), row),
        ],
        out_specs=[
            pl.BlockSpec(blk, lambda i: (jnp.minimum(i, nbp - 1), 0, 0, 0)),
            pl.BlockSpec(blk, lambda i: (jnp.maximum(i - nbp, 0), 0, 0, 0)),
        ],
        out_shape=[
            jax.ShapeDtypeStruct((N_PROMPT // Q_BLOCK, N_HEADS, Q_BLOCK, D_QK), BF16),
            jax.ShapeDtypeStruct((N_SAMPLE // Q_BLOCK, N_HEADS, Q_BLOCK, D_QK), F32),
        ],
        compiler_params=_cparams(("arbitrary",)),
        name="q_proj",
    )(cq, wqn, wqr, wqrr, wukt, cos, sin)


def _pool_groups(read, pos, w_ref, scale_ref, write):
    for g, w in enumerate(POOL_WINDOWS):
        lanes = slice(g * POOL_GROUP, (g + 1) * POOL_GROUP)
        cur = read(0, lanes)
        win = cur
        for j in range(1, w):
            win = win + read(j, lanes)
        cnt = jnp.minimum(pos + 1, w).astype(F32)
        diff = (win / cnt - cur).astype(BF16)
        write(lanes, _dot(diff, w_ref[g]) * scale_ref[:, lanes])


def _pool_prompt_kernel(u_ref, w_ref, scale_ref, y_ref, xp_ref, *, T):
    t = pl.program_id(1)

    @pl.when(t == 0)
    def _():
        xp_ref[0:POOL_HIST, :] = jnp.zeros((POOL_HIST, D_POOL), F32)

    xp_ref[POOL_HIST:POOL_HIST + T, :] = u_ref[...]
    pos = t * T + lax.broadcasted_iota(jnp.int32, (T, 1), 0)

    def read(j, lanes):
        return xp_ref[POOL_HIST - j:POOL_HIST - j + T, lanes]

    def write(lanes, y):
        y_ref[:, lanes] = y.astype(y_ref.dtype)

    _pool_groups(read, pos, w_ref, scale_ref, write)
    xp_ref[0:POOL_HIST, :] = xp_ref[T:T + POOL_HIST, :]


def _pool_prompt(u, pool_w, pool_scale, T=512):
    nt = SEQ // T
    return pl.pallas_call(
        functools.partial(_pool_prompt_kernel, T=T),
        grid=(BATCH, nt),
        in_specs=[
            pl.BlockSpec((T, D_POOL), lambda b, t: (b * nt + t, 0)),
            _resident((N_POOL_GROUPS, POOL_GROUP, POOL_GROUP)),
            _resident((1, D_POOL)),
        ],
        out_specs=pl.BlockSpec((T, D_POOL), lambda b, t: (b * nt + t, 0)),
        out_shape=jax.ShapeDtypeStruct((N_PROMPT, D_POOL), BF16),
        scratch_shapes=[pltpu.VMEM((T + POOL_HIST, D_POOL), F32)],
        compiler_params=_cparams(("arbitrary", "arbitrary")),
        name="pool_prompt",
    )(u, pool_w, pool_scale)


def _pool_sample_kernel(u_ref, hist_ref, w_ref, scale_ref, y_ref, xp_ref, *, S):
    xp_ref[:, 0:POOL_HIST, :] = hist_ref[...]
    xp_ref[:, POOL_HIST:, :] = u_ref[...]
    pos = PAST_LEN + lax.broadcasted_iota(jnp.int32, (S, DEC_SEQ, 1), 1)
    pos = pos.reshape(S * DEC_SEQ, 1)

    def read(j, lanes):
        return xp_ref[:, POOL_HIST - j:POOL_HIST - j + DEC_SEQ, lanes].reshape(S * DEC_SEQ, POOL_GROUP)

    def write(lanes, y):
        y_ref[:, lanes] = y.astype(y_ref.dtype)

    _pool_groups(read, pos, w_ref, scale_ref, write)


def _pool_sample(u3, hist, pool_w, pool_scale, S=32):
    rows = S * DEC_SEQ
    return pl.pallas_call(
        functools.partial(_pool_sample_kernel, S=S),
        grid=(DEC_BATCH // S,),
        in_specs=[
            pl.BlockSpec((S, DEC_SEQ, D_POOL), lambda i: (i, 0, 0)),
            pl.BlockSpec((S, POOL_HIST, D_POOL), lambda i: (i, 0, 0)),
            _resident((N_POOL_GROUPS, POOL_GROUP, POOL_GROUP)),
            _resident((1, D_POOL)),
        ],
        out_specs=pl.BlockSpec((rows, D_POOL), lambda i: (i, 0)),
        out_shape=jax.ShapeDtypeStruct((N_SAMPLE, D_POOL), BF16),
        scratch_shapes=[pltpu.VMEM((S, POOL_HIST + DEC_SEQ, D_POOL), F32)],
        compiler_params=_cparams(("arbitrary",)),
        name="pool_sample",
    )(u3, hist, pool_w, pool_scale)


def _prompt_attn_block(i, q_ref, kt_ref, v_ref, wuv_ref, o_ref, m_ref, l_ref, acc_ref, sa_ref, sb_ref, heads):
    rows_all = heads * Q_BLOCK
    m_ref[...] = jnp.full((rows_all, 1), -jnp.inf, F32)
    l_ref[...] = jnp.zeros((rows_all, 1), F32)
    acc_ref[...] = jnp.zeros((rows_all, KV_LORA), F32)
    q = q_ref[0].reshape(rows_all, D_QK)
    q_pos = i * Q_BLOCK + (lax.broadcasted_iota(jnp.int32, (rows_all, KV_CHUNK), 0) & (Q_BLOCK - 1))
    key = lax.broadcasted_iota(jnp.int32, (rows_all, KV_CHUNK), 1)

    def scores(j, s_ref):
        s = _dot(q, kt_ref[j])
        s_ref[...] = jnp.where(j * KV_CHUNK + key <= q_pos, s, -jnp.inf)

    def attend(j, s_ref):
        v = v_ref[pl.ds(pl.multiple_of(j * KV_CHUNK, KV_CHUNK), KV_CHUNK), :]
        m_new, l_new, acc_new = _softmax_update(m_ref[...], l_ref[...], acc_ref[...], s_ref[...], v)
        m_ref[...] = m_new
        l_ref[...] = l_new
        acc_ref[...] = acc_new

    n_chunks = (i * Q_BLOCK + Q_BLOCK + KV_CHUNK - 1) // KV_CHUNK
    n_pairs = (n_chunks - 1) // 2
    scores(0, sa_ref)

    def body(t, c):
        scores(2 * t + 1, sb_ref)
        attend(2 * t, sa_ref)
        scores(2 * t + 2, sa_ref)
        attend(2 * t + 1, sb_ref)
        return c

    lax.fori_loop(0, n_pairs, body, 0)
    done = 2 * n_pairs

    @pl.when(done + 2 == n_chunks)
    def _():
        scores(done + 1, sb_ref)
        attend(done, sa_ref)
        attend(done + 1, sb_ref)

    @pl.when(done + 1 == n_chunks)
    def _():
        attend(done, sa_ref)

    for h in range(heads):
        rows = slice(h * Q_BLOCK, (h + 1) * Q_BLOCK)
        o_lat = (acc_ref[rows] * (1.0 / l_ref[rows])).astype(BF16)
        o_ref[:, h * V_HEAD:(h + 1) * V_HEAD] = _dot(o_lat, wuv_ref[h]).astype(o_ref.dtype)


PAGES_PER_CHUNK = 32
CHUNK_KEYS = PAGES_PER_CHUNK * PAGE_SIZE
CHUNKS_PER_SEQ = N_PAGES // PAGES_PER_CHUNK
PAGES_PER_GROUP = KV_LORA // PAGE_SIZE
GROUP_KEYS = PAGES_PER_GROUP * PAGE_SIZE
ROPE_ROWS = CHUNK_KEYS // GROUP_KEYS * QK_ROPE
DEC_SLOTS = 2
DEC_AHEAD = DEC_SLOTS - 1
NEW_KEYS_PAD = 16
ATTN_HEADS_PER_STEP = N_HEADS // 2
BLOCKS_PER_SEQ = SEQ // Q_BLOCK
assert CHUNKS_PER_SEQ % DEC_SLOTS == 0
assert DEC_BATCH * ATTN_HEADS_PER_STEP == BATCH * BLOCKS_PER_SEQ * N_HEADS


def _prompt_block(k):
    b, r = k // BLOCKS_PER_SEQ, k % BLOCKS_PER_SEQ
    return b, jnp.where(r % 2 == 0, r // 2, BLOCKS_PER_SEQ - 1 - r // 2)


def _attn_kernel(pt_ref, q_ref, ckvn_ref, kpen_ref, cckv_hbm, ckpet_hbm, qp_ref, kt_ref, v_ref, wuv_ref,
                 o_ref, op_ref, kv_buf, kvb_a, kvb_b, sc_a, sc_b, sem,
                 m_ref, l_ref, acc_ref, sa_ref, sb_ref):
    s = pl.program_id(0)
    nseq = pl.num_programs(0)
    rows = N_HEADS * DEC_SEQ

    def rope_rows(group):
        return pl.ds(CHUNK_KEYS + group * QK_ROPE, QK_ROPE)

    def start_chunk(seq, c):
        slot = c % DEC_SLOTS
        for p in range(PAGES_PER_CHUNK):
            page = pt_ref[(seq * CHUNKS_PER_SEQ + c) * PAGES_PER_CHUNK + p]
            lanes = pl.ds(p % PAGES_PER_GROUP * PAGE_SIZE, PAGE_SIZE)
            pltpu.make_async_copy(cckv_hbm.at[page], kv_buf.at[slot, pl.ds(p * PAGE_SIZE, PAGE_SIZE), :],
                                  sem.at[slot]).start()
            pltpu.make_async_copy(ckpet_hbm.at[page], kv_buf.at[slot, rope_rows(p // PAGES_PER_GROUP), lanes],
                                  sem.at[slot]).start()

    def wait_chunk(c):
        slot = c % DEC_SLOTS
        pltpu.make_async_copy(kv_buf.at[slot], kv_buf.at[slot], sem.at[slot]).wait()

    @pl.when(s == 0)
    def _():
        for c in range(DEC_AHEAD):
            start_chunk(0, c)

    q = q_ref[0].reshape(rows, D_QK)
    q_lat = q[:, :KV_LORA].astype(BF16)
    q_pe = q[:, KV_LORA:].astype(BF16)

    def scores(c, kvb_ref, sc_ref):
        nxt = c + DEC_AHEAD
        if nxt < CHUNKS_PER_SEQ:
            start_chunk(s, nxt)
        else:
            @pl.when(s + 1 < nseq)
            def _():
                start_chunk(s + 1, nxt - CHUNKS_PER_SEQ)
        wait_chunk(c)
        slot = c % DEC_SLOTS
        kvb_ref[...] = kv_buf[slot, :CHUNK_KEYS, :].astype(BF16)
        lat = _dot_nt(q_lat, kvb_ref[...])
        for g in range(CHUNK_KEYS // GROUP_KEYS):
            keys = slice(g * GROUP_KEYS, (g + 1) * GROUP_KEYS)
            sc_ref[:, keys] = lat[:, keys] + _dot(q_pe, kv_buf[slot, rope_rows(g), :].astype(BF16))

    def attend(carry, kvb_ref, sc_ref):
        return _softmax_update(*carry, sc_ref[...], kvb_ref[...])

    bufs = ((kvb_a, sc_a), (kvb_b, sc_b))
    carry = (jnp.full((rows, 1), -jnp.inf, F32), jnp.zeros((rows, 1), F32), jnp.zeros((rows, KV_LORA), F32))
    scores(0, *bufs[0])
    for c in range(1, CHUNKS_PER_SEQ):
        scores(c, *bufs[c % 2])
        carry = attend(carry, *bufs[(c - 1) % 2])
        if c == CHUNKS_PER_SEQ // 2:
            _, blk = _prompt_block(s // 2)
            _prompt_attn_block(blk, qp_ref, kt_ref, v_ref, wuv_ref, op_ref, m_ref, l_ref, acc_ref,
                               sa_ref, sb_ref, ATTN_HEADS_PER_STEP)
    carry = attend(carry, *bufs[(CHUNKS_PER_SEQ - 1) % 2])

    pad = NEW_KEYS_PAD - DEC_SEQ
    kvn = jnp.concatenate([ckvn_ref[...], jnp.zeros((pad, KV_LORA), F32)], axis=0).astype(BF16)
    kpn = jnp.concatenate([kpen_ref[...], jnp.zeros((pad, QK_ROPE), F32)], axis=0).astype(BF16)
    sc = _dot_nt(q_lat, kvn) + _dot_nt(q_pe, kpn)
    tok = lax.broadcasted_iota(jnp.int32, (rows, NEW_KEYS_PAD), 0) & (DEC_SEQ - 1)
    key = lax.broadcasted_iota(jnp.int32, (rows, NEW_KEYS_PAD), 1)
    sc = jnp.where(key <= tok, sc, -jnp.inf)
    _, l_fin, acc = _softmax_update(*carry, sc, kvn)
    o_ref[0] = (acc * (1.0 / l_fin)).reshape(N_HEADS, DEC_SEQ, KV_LORA)


def _attention(page_table, qs, ckv_s, kpe_s, cache_ckv, cache_kpet, qp, kt, vb, wuv):
    seq_per_blk = Q_BLOCK // DEC_SEQ
    hps = ATTN_HEADS_PER_STEP
    halves = N_HEADS // hps
    cps = SEQ // KV_CHUNK
    rows_p = hps * Q_BLOCK

    def qblock(s):
        b, i = _prompt_block(s // halves)
        return b * BLOCKS_PER_SEQ + i

    grid_spec = pltpu.PrefetchScalarGridSpec(
        num_scalar_prefetch=1,
        grid=(DEC_BATCH,),
        in_specs=[
            pl.BlockSpec((1, N_HEADS, DEC_SEQ, D_QK), lambda s, pt: (s // seq_per_blk, 0, s % seq_per_blk, 0)),
            pl.BlockSpec((DEC_SEQ, KV_LORA), lambda s, pt: (s, 0)),
            pl.BlockSpec((DEC_SEQ, QK_ROPE), lambda s, pt: (s, 0)),
            pl.BlockSpec(memory_space=pl.ANY),
            pl.BlockSpec(memory_space=pl.ANY),
            pl.BlockSpec((1, hps, Q_BLOCK, D_QK), lambda s, pt: (qblock(s), s % halves, 0, 0)),
            pl.BlockSpec((cps, D_QK, KV_CHUNK), lambda s, pt: (_prompt_block(s // halves)[0], 0, 0)),
            pl.BlockSpec((SEQ, KV_LORA), lambda s, pt: (_prompt_block(s // halves)[0], 0)),
            pl.BlockSpec((hps, KV_LORA, V_HEAD), lambda s, pt: (s % halves, 0, 0)),
        ],
        out_specs=[
            pl.BlockSpec((1, N_HEADS, DEC_SEQ, KV_LORA), lambda s, pt: (s // seq_per_blk, 0, s % seq_per_blk, 0)),
            pl.BlockSpec((Q_BLOCK, hps * V_HEAD), lambda s, pt: (qblock(s), s % halves)),
        ],
        scratch_shapes=[
            pltpu.VMEM((DEC_SLOTS, CHUNK_KEYS + ROPE_ROWS, KV_LORA), F32),
            pltpu.VMEM((CHUNK_KEYS, KV_LORA), BF16),
            pltpu.VMEM((CHUNK_KEYS, KV_LORA), BF16),
            pltpu.VMEM((N_HEADS * DEC_SEQ, CHUNK_KEYS), F32),
            pltpu.VMEM((N_HEADS * DEC_SEQ, CHUNK_KEYS), F32),
            pltpu.SemaphoreType.DMA((DEC_SLOTS,)),
            pltpu.VMEM((rows_p, 1), F32),
            pltpu.VMEM((rows_p, 1), F32),
            pltpu.VMEM((rows_p, KV_LORA), F32),
            pltpu.VMEM((rows_p, KV_CHUNK), F32),
            pltpu.VMEM((rows_p, KV_CHUNK), F32),
        ],
    )
    return pl.pallas_call(
        _attn_kernel,
        grid_spec=grid_spec,
        out_shape=[
            jax.ShapeDtypeStruct((N_SAMPLE // Q_BLOCK, N_HEADS, Q_BLOCK, KV_LORA), F32),
            jax.ShapeDtypeStruct((N_PROMPT, D_MLA_OUT), BF16),
        ],
        compiler_params=_cparams(("arbitrary",)),
        name="attention",
    )(page_table.reshape(-1), qs, ckv_s, kpe_s, cache_ckv, cache_kpet, qp, kt, vb, wuv)


def _uv_proj_kernel(ol_ref, wuv_ref, o_ref, *, nsub):
    for j in range(nsub):
        for h in range(N_HEADS):
            o = _dot(ol_ref[j, h].astype(BF16), wuv_ref[h])
            o_ref[j * Q_BLOCK:(j + 1) * Q_BLOCK, h * V_HEAD:(h + 1) * V_HEAD] = o.astype(o_ref.dtype)


def _uv_proj_sample(o_lat, wuv, nsub=4):
    nb = o_lat.shape[0]
    rows = nsub * Q_BLOCK
    return pl.pallas_call(
        functools.partial(_uv_proj_kernel, nsub=nsub),
        grid=(nb // nsub,),
        in_specs=[
            pl.BlockSpec((nsub, N_HEADS, Q_BLOCK, KV_LORA), lambda i: (i, 0, 0, 0)),
            _resident((N_HEADS, KV_LORA, V_HEAD)),
        ],
        out_specs=pl.BlockSpec((rows, D_MLA_OUT), lambda i: (i, 0)),
        out_shape=jax.ShapeDtypeStruct((nb * Q_BLOCK, D_MLA_OUT), BF16),
        compiler_params=_cparams(("arbitrary",)),
        name="uv_proj_sample",
    )(o_lat, wuv)


def _mm_ln_kernel(*refs, lhs_split, split_resid, split_out, tm):
    is_prompt = pl.program_id(0) < N_PROMPT // tm
    refs = list(refs)

    def take(split):
        if split:
            p_ref, s_ref = refs.pop(0), refs.pop(0)
            return lambda rows: jnp.where(is_prompt, p_ref[rows, :], s_ref[rows, :])
        ref = refs.pop(0)
        return lambda rows: ref[rows, :]

    lhs = [take(split) for split in lhs_split]
    w_ref = refs.pop(0)
    resid = take(split_resid)
    g_ref, b_ref = refs.pop(0), refs.pop(0)
    out_refs = refs

    sub = min(tm, FFN_SUB_ROWS)
    for r in range(0, tm, sub):
        rows = slice(r, r + sub)
        acc = None
        off = 0
        for piece in lhs:
            a = piece(rows)
            part = _dot(a, w_ref[off:off + a.shape[1], :])
            acc = part if acc is None else acc + part
            off += a.shape[1]
        y = _layer_norm(DEEPNORM_ALPHA * resid(rows) + acc, g_ref[...], b_ref[...])
        if split_out:
            @pl.when(is_prompt)
            def _():
                out_refs[0][rows, :] = y

            @pl.when(jnp.logical_not(is_prompt))
            def _():
                out_refs[1][rows, :] = y
        else:
            out_refs[0][rows, :] = y
            out_refs[1][rows, :] = y.astype(BF16)


def _mm_ln(lhs_list, w, resid, g, b, *, tm, split_out=False, layer=None):
    operands, in_specs = [], []

    def add_rows(v):
        if isinstance(v, tuple):
            operands.extend(v)
            in_specs.extend([_prompt_rows(tm, v[0].shape[1]), _sample_rows(tm, v[0].shape[1])])
        else:
            operands.append(v)
            in_specs.append(pl.BlockSpec((tm, v.shape[1]), lambda i: (i, 0)))
        return isinstance(v, tuple)

    lhs_split = tuple(add_rows(v) for v in lhs_list)
    operands.append(w)
    if layer is None:
        in_specs.append(_resident(w.shape))
    else:
        in_specs.append(pl.BlockSpec((None,) + w.shape[1:], lambda i: (layer, 0, 0), pipeline_mode=pl.Buffered(1)))
    split_resid = add_rows(resid)
    operands.extend([g, b])
    in_specs.extend([_resident((1, D_MODEL)), _resident((1, D_MODEL))])
    if split_out:
        out_specs = [_prompt_rows(tm, D_MODEL), _sample_rows(tm, D_MODEL)]
        out_shape = [jax.ShapeDtypeStruct((N_PROMPT, D_MODEL), F32), jax.ShapeDtypeStruct((N_SAMPLE, D_MODEL), F32)]
    else:
        row = pl.BlockSpec((tm, D_MODEL), lambda i: (i, 0))
        out_specs = [row, row]
        out_shape = [jax.ShapeDtypeStruct((N_TOK, D_MODEL), F32), jax.ShapeDtypeStruct((N_TOK, D_MODEL), BF16)]
    return pl.pallas_call(
        functools.partial(_mm_ln_kernel, lhs_split=lhs_split, split_resid=split_resid, split_out=split_out, tm=tm),
        grid=(N_TOK // tm,),
        in_specs=in_specs,
        out_specs=out_specs,
        out_shape=out_shape,
        compiler_params=_cparams(("arbitrary",)),
        name="mm_ln",
    )(*operands)


def _ffn_up_kernel(x_ref, wg_ref, wu_ref, h_ref, wgb_ref, wub_ref):
    @pl.when(pl.program_id(1) == 0)
    def _():
        wgb_ref[...] = wg_ref[...].astype(BF16)
        wub_ref[...] = wu_ref[...].astype(BF16)

    for r in range(0, x_ref.shape[0], FFN_SUB_ROWS):
        x = x_ref[r:r + FFN_SUB_ROWS, :]
        h = jax.nn.silu(_dot(x, wgb_ref[...])) * _dot(x, wub_ref[...])
        h_ref[r:r + FFN_SUB_ROWS, :] = h.astype(h_ref.dtype)


def _ffn_up(xb, wg, wu, layer, tm=N_TOK // 4, tn=512):
    wblk = pl.BlockSpec((None, D_MODEL, tn), lambda j, i: (layer, 0, j))
    return pl.pallas_call(
        _ffn_up_kernel,
        grid=(D_FF // tn, N_TOK // tm),
        in_specs=[pl.BlockSpec((tm, D_MODEL), lambda j, i: (i, 0)), wblk, wblk],
        out_specs=pl.BlockSpec((tm, tn), lambda j, i: (i, j)),
        out_shape=jax.ShapeDtypeStruct((N_TOK, D_FF), BF16),
        scratch_shapes=[pltpu.VMEM((D_MODEL, tn), BF16), pltpu.VMEM((D_MODEL, tn), BF16)],
        compiler_params=_cparams(("arbitrary", "arbitrary")),
        name="ffn_up",
    )(xb, wg, wu)


def _conv_taps(read, cw_ref):
    conv = read(CONV_W - 1) * cw_ref[0:1, :]
    for j in range(1, CONV_W):
        conv = conv + read(CONV_W - 1 - j) * cw_ref[j:j + 1, :]
    return conv


def _in_odd_cast_weights(wv_ref, wb_ref, wc_ref, wvb_ref, wbb_ref, wcb_ref):
    @pl.when(pl.program_id(1) == 0)
    def _():
        wvb_ref[...] = wv_ref[...].astype(BF16)
        wbb_ref[...] = wb_ref[...].astype(BF16)
        wcb_ref[...] = wc_ref[...].astype(BF16)


def _in_odd_prompt_kernel(x_ref, wv_ref, wb_ref, wc_ref, cw_ref, g_ref, ztail_ref,
                          wvb_ref, wbb_ref, wcb_ref, zp_ref, *, tm):
    i = pl.program_id(1)
    tiles_per_seq = SEQ // tm
    _in_odd_cast_weights(wv_ref, wb_ref, wc_ref, wvb_ref, wbb_ref, wcb_ref)

    @pl.when(i % tiles_per_seq == 0)
    def _():
        zp_ref[0:CONV_HIST, :] = jnp.zeros((CONV_HIST, zp_ref.shape[1]), F32)

    for r in range(0, tm, FFN_SUB_ROWS):
        x = x_ref[r:r + FFN_SUB_ROWS, :]
        new = slice(CONV_HIST + r, CONV_HIST + r + FFN_SUB_ROWS)
        zp_ref[new, :] = _dot(x, wcb_ref[...]) * _dot(x, wvb_ref[...])
        conv = _conv_taps(lambda back: zp_ref[new.start - back:new.stop - back, :], cw_ref)
        g_ref[r:r + FFN_SUB_ROWS, :] = (_dot(x, wbb_ref[...]) * conv).astype(g_ref.dtype)
    zp_ref[0:CONV_HIST, :] = zp_ref[tm:tm + CONV_HIST, :]

    @pl.when(i % tiles_per_seq == tiles_per_seq - 1)
    def _():
        ztail_ref[...] = zp_ref[tm:tm + CONV_HIST, :]


def _in_odd_sample_kernel(x_ref, wv_ref, wb_ref, wc_ref, zs_ref, gbs_ref, wvb_ref, wbb_ref, wcb_ref):
    _in_odd_cast_weights(wv_ref, wb_ref, wc_ref, wvb_ref, wbb_ref, wcb_ref)
    x = x_ref[...]
    zs_ref[...] = _dot(x, wcb_ref[...]) * _dot(x, wvb_ref[...])
    gbs_ref[...] = _dot(x, wbb_ref[...])


def _in_odd(xb, w_in, conv_w, tm=1024, tn=512):
    nb = D_MODEL // tn
    nbp = N_PROMPT // tm
    tiles_per_seq = SEQ // tm
    wblks = [pl.BlockSpec((D_MODEL, tn), lambda j, i, part=part: (0, part * nb + j)) for part in range(3)]
    wscratch = [pltpu.VMEM((D_MODEL, tn), BF16)] * 3
    g_prompt, z_tail = pl.pallas_call(
        functools.partial(_in_odd_prompt_kernel, tm=tm),
        grid=(nb, nbp),
        in_specs=[pl.BlockSpec((tm, D_MODEL), lambda j, i: (i, 0))] + wblks
        + [pl.BlockSpec((CONV_W, tn), lambda j, i: (0, j))],
        out_specs=[
            pl.BlockSpec((tm, tn), lambda j, i: (i, j)),
            pl.BlockSpec((CONV_HIST, tn), lambda j, i: (i // tiles_per_seq, j)),
        ],
        out_shape=[
            jax.ShapeDtypeStruct((N_PROMPT, D_MODEL), BF16),
            jax.ShapeDtypeStruct((BATCH * CONV_HIST, D_MODEL), F32),
        ],
        scratch_shapes=wscratch + [pltpu.VMEM((tm + CONV_HIST, tn), F32)],
        compiler_params=_cparams(("arbitrary", "arbitrary")),
        name="in_odd_prompt",
    )(xb, w_in, w_in, w_in, conv_w)
    sample_blk = pl.BlockSpec((tm, tn), lambda j, i: (i, j))
    z_s, gb_s = pl.pallas_call(
        _in_odd_sample_kernel,
        grid=(nb, N_SAMPLE // tm),
        in_specs=[pl.BlockSpec((tm, D_MODEL), lambda j, i: (nbp + i, 0))] + wblks,
        out_specs=[sample_blk, sample_blk],
        out_shape=[jax.ShapeDtypeStruct((N_SAMPLE, D_MODEL), F32)] * 2,
        scratch_shapes=wscratch,
        compiler_params=_cparams(("arbitrary", "arbitrary")),
        name="in_odd_sample",
    )(xb, w_in, w_in, w_in)
    return g_prompt, z_tail, z_s, gb_s


def _conv_sample_kernel(z_ref, hist_ref, gb_ref, cw_ref, g_ref, zp_ref, *, S):
    zp_ref[:, 0:CONV_HIST, :] = hist_ref[...]
    zp_ref[:, CONV_HIST:, :] = z_ref[...]
    conv = _conv_taps(lambda back: zp_ref[:, CONV_HIST - back:CONV_HIST - back + DEC_SEQ, :], cw_ref)
    g_ref[...] = (gb_ref[...] * conv).reshape(S * DEC_SEQ, D_MODEL).astype(g_ref.dtype)


def _conv_sample(z3, hist, gb3, conv_w, S=32):
    blk = lambda i: (i, 0, 0)
    rows = S * DEC_SEQ
    return pl.pallas_call(
        functools.partial(_conv_sample_kernel, S=S),
        grid=(DEC_BATCH // S,),
        in_specs=[
            pl.BlockSpec((S, DEC_SEQ, D_MODEL), blk),
            pl.BlockSpec((S, CONV_HIST, D_MODEL), blk),
            pl.BlockSpec((S, DEC_SEQ, D_MODEL), blk),
            _resident((CONV_W, D_MODEL)),
        ],
        out_specs=pl.BlockSpec((rows, D_MODEL), lambda i: (i, 0)),
        out_shape=jax.ShapeDtypeStruct((N_SAMPLE, D_MODEL), BF16),
        scratch_shapes=[pltpu.VMEM((S, CONV_HIST + DEC_SEQ, D_MODEL), F32)],
        compiler_params=_cparams(("arbitrary",)),
        name="conv_sample",
    )(z3, hist, gb3, conv_w)


def _rope_tables():
    pos = jnp.concatenate([jnp.tile(jnp.arange(SEQ), BATCH),
                           jnp.tile(PAST_LEN + jnp.arange(DEC_SEQ), DEC_BATCH)])
    inv = ROPE_THETA ** (-jnp.arange(0, QK_ROPE, 2, dtype=F32) / QK_ROPE)
    ang = pos.astype(F32)[:, None] * inv[None, :]
    cos, sin = jnp.cos(ang), jnp.sin(ang)
    cos2 = jnp.concatenate([cos, cos], axis=1)
    sin2 = jnp.concatenate([-sin, sin], axis=1)
    return jnp.tile(cos2, (1, 2)), jnp.tile(sin2, (1, 2)), jnp.concatenate([cos2, sin2], axis=1)


def _swap_halves(w):
    half = QK_ROPE // 2
    return jnp.concatenate([w[..., half:], w[..., :half]], axis=-1)


def kernel(x_prompt, x_sample, cache_ckv, cache_kpe, state_pool, state_conv, page_table, w_in_even, pool_w, pool_scale, q_norm_g, w_q_b, kv_norm_g, w_uk, w_uv, w_out_even, w_in_odd, conv_w, w_out_odd, ffn_w_gate, ffn_w_up, ffn_w_down, ln1_g, ln1_b, ln2_g, ln2_b):
    assert DEPTH == 2
    n_pool_pages = cache_ckv.shape[1]
    cos, sin, cos_sin = _rope_tables()
    xp = x_prompt.reshape(N_PROMPT, D_MODEL)
    xs = x_sample.reshape(N_SAMPLE, D_MODEL)

    def row(v):
        return v.reshape(1, -1)

    w_down = ffn_w_down.astype(BF16)

    def ffn(layer, xf, xb, last):
        hmid = _ffn_up(xb, ffn_w_gate, ffn_w_up, layer)
        return _mm_ln([hmid], w_down, xf, row(ln2_g[layer]), row(ln2_b[layer]),
                      tm=256, split_out=last, layer=layer)

    e = 0
    w_in = w_in_even[e]
    w_all = jnp.concatenate([w_in, _swap_halves(w_in[:, D_POOL + Q_LORA + KV_LORA:])], axis=1).astype(BF16)
    u, cq, ckv_p, ckv_s, kpe_p, kpe_s, vb, kt = _in_even(
        xp, xs, w_all, row(q_norm_g[e]), row(kv_norm_g[e]), cos_sin)

    wq = w_q_b[e]
    wq_rope = wq[:, :, QK_NOPE:]
    qp, qs = _q_proj(
        cq,
        wq[:, :, :QK_NOPE].reshape(Q_LORA, N_HEADS * QK_NOPE).astype(BF16),
        wq_rope.reshape(Q_LORA, N_HEADS * QK_ROPE).astype(BF16),
        _swap_halves(wq_rope).reshape(Q_LORA, N_HEADS * QK_ROPE).astype(BF16),
        jnp.transpose(w_uk[e], (1, 2, 0)).astype(BF16), cos, sin)

    pw = pool_w[e].astype(BF16)
    ps = row(pool_scale[e])
    u_sample = u[N_PROMPT:].reshape(DEC_BATCH, DEC_SEQ, D_POOL)
    pool_hist = jnp.pad(state_pool[e], ((0, 0), (POOL_HIST - POOL_BUF, 0), (0, 0)))
    y_pool = (_pool_prompt(u, pw, ps), _pool_sample(u_sample, pool_hist, pw, ps))

    wuv = jnp.transpose(w_uv[e], (1, 0, 2)).astype(BF16)
    o_lat_s, o_prompt = _attention(page_table, qs, ckv_s, kpe_s,
                                   cache_ckv[e].reshape(n_pool_pages, PAGE_SIZE, KV_LORA),
                                   jnp.swapaxes(cache_kpe[e], 1, 2), qp, kt, vb, wuv)
    o_sample = _uv_proj_sample(o_lat_s, wuv)
    xf, xb = _mm_ln([y_pool, (o_prompt, o_sample)], w_out_even[e].astype(BF16), (xp, xs),
                    row(ln1_g[0]), row(ln1_b[0]), tm=512)
    xf, xb = ffn(0, xf, xb, False)

    od = 0
    g_prompt, z_tail, z_s, gb_s = _in_odd(xb, w_in_odd[od], conv_w[od])
    z_sample = z_s.reshape(DEC_BATCH, DEC_SEQ, D_MODEL)
    gb_sample = gb_s.reshape(DEC_BATCH, DEC_SEQ, D_MODEL)
    conv_hist = jnp.pad(state_conv[od], ((0, 0), (CONV_HIST - CONV_BUF, 0), (0, 0)))
    g = (g_prompt, _conv_sample(z_sample, conv_hist, gb_sample, conv_w[od]))
    xf, xb = _mm_ln([g], w_out_odd[od].astype(BF16), xf, row(ln1_g[1]), row(ln1_b[1]), tm=512)
    y_p, y_s = ffn(1, xf, xb, True)

    pool_prompt = jnp.stack([u[(b + 1) * SEQ - POOL_BUF:(b + 1) * SEQ] for b in range(BATCH)])
    pool_sample = jnp.concatenate([state_pool[e][:, DEC_SEQ:], u_sample], axis=1)
    conv_prompt = z_tail.reshape(BATCH, CONV_HIST, D_MODEL)[:, CONV_HIST - CONV_BUF:]
    conv_sample = z_sample[:, DEC_SEQ - CONV_BUF:]
    return (y_p.reshape(BATCH, SEQ, D_MODEL), y_s.reshape(DEC_BATCH, DEC_SEQ, D_MODEL),
            ckv_p.reshape(1, BATCH, SEQ, KV_LORA), kpe_p.reshape(1, BATCH, SEQ, QK_ROPE),
            pool_prompt[None], conv_prompt[None],
            ckv_s.reshape(1, DEC_BATCH, DEC_SEQ, KV_LORA), kpe_s.reshape(1, DEC_BATCH, DEC_SEQ, QK_ROPE),
            pool_sample[None], conv_sample[None])
```

```python
import functools

import jax
import jax.numpy as jnp
from jax import lax
from jax.experimental import pallas as pl
from jax.experimental.pallas import tpu as pltpu

F32 = jnp.float32
BF16 = jnp.bfloat16

D_MODEL = 2048
BATCH = 4
SEQ = 2048
DEPTH = 2
DEC_BATCH = 128
DEC_SEQ = 8
PAST_LEN = 16384
PAGE_SIZE = 128
N_PAGES = PAST_LEN // PAGE_SIZE
POOL_WINDOWS = (2, 4, 8, 16)
N_POOL_GROUPS = len(POOL_WINDOWS)
D_POOL = D_MODEL // 2
POOL_GROUP = D_POOL // N_POOL_GROUPS
POOL_BUF = max(POOL_WINDOWS) - 1
N_HEADS = 8
QK_NOPE = 128
QK_ROPE = 64
V_HEAD = 128
Q_LORA = 512
KV_LORA = 512
ROPE_THETA = 10000.0
ATTN_SCALE = (QK_NOPE + QK_ROPE) ** -0.5
Q_BLOCK = 128
D_MLA_OUT = N_HEADS * V_HEAD
CONV_W = 3
CONV_BUF = CONV_W - 1
D_FF = ((8 * D_MODEL + 3 * 256 - 1) // (3 * 256)) * 256
DEEPNORM_ALPHA = (2 * DEPTH) ** 0.25

N_PROMPT = BATCH * SEQ
N_SAMPLE = DEC_BATCH * DEC_SEQ
N_TOK = N_PROMPT + N_SAMPLE
D_QK = KV_LORA + QK_ROPE
POOL_HIST = 16
CONV_HIST = 8
KV_CHUNK = 512
FFN_SUB_ROWS = 256
VMEM_LIMIT_MB = 56


def _cparams(dims, vmem_mb=VMEM_LIMIT_MB):
    return pltpu.CompilerParams(dimension_semantics=dims, vmem_limit_bytes=vmem_mb * 1024 * 1024)


def _resident(shape):
    zeros = (0,) * len(shape)
    return pl.BlockSpec(shape, lambda *_: zeros, pipeline_mode=pl.Buffered(1))


def _prompt_rows(tm, width):
    last = N_PROMPT // tm - 1
    return pl.BlockSpec((tm, width), lambda i, *_: (jnp.minimum(i, last), 0))


def _sample_rows(tm, width):
    first = N_PROMPT // tm
    return pl.BlockSpec((tm, width), lambda i, *_: (jnp.maximum(i - first, 0), 0))


def _dot(a, b):
    return jnp.dot(a, b, preferred_element_type=F32)


def _dot_nt(a, b):
    return lax.dot_general(a, b, (((1,), (1,)), ((), ())), preferred_element_type=F32)


def _rms(h, g, eps=1e-6):
    return h * lax.rsqrt(jnp.mean(h * h, axis=-1, keepdims=True) + eps) * g


def _layer_norm(v, g, b, eps=1e-5):
    mu = jnp.mean(v, axis=-1, keepdims=True)
    vc = v - mu
    var = jnp.mean(vc * vc, axis=-1, keepdims=True)
    return vc * lax.rsqrt(var + eps) * g + b


def _softmax_update(m_old, l_old, acc, s, v):
    m_new = jnp.maximum(m_old, jnp.max(s, axis=-1, keepdims=True))
    alpha = jnp.exp(m_old - m_new)
    p = jnp.exp(s - m_new)
    l_new = alpha * l_old + jnp.sum(p, axis=-1, keepdims=True)
    return m_new, l_new, alpha * acc + _dot(p.astype(BF16), v)


D_IN_EVEN = D_POOL + Q_LORA + KV_LORA + 2 * QK_ROPE


def _in_even_kernel(xp_ref, xs_ref, w_ref, qg_ref, kvg_ref, cs_ref,
                    u_ref, cq_ref, ckvp_ref, ckvs_ref, kpep_ref, kpes_ref, vb_ref, kt_ref, *, tm):
    i = pl.program_id(0)
    is_prompt = i < N_PROMPT // tm
    xb = jnp.where(is_prompt, xp_ref[...], xs_ref[...]).astype(BF16)
    h = _dot(xb, w_ref[...])
    o1, o2 = D_POOL + Q_LORA, D_POOL + Q_LORA + KV_LORA
    u_ref[...] = h[:, :D_POOL]
    cq_ref[...] = _rms(h[:, D_POOL:o1], qg_ref[...]).astype(BF16)
    ckv = _rms(h[:, o1:o2], kvg_ref[...])
    prod = h[:, o2:] * cs_ref[...]
    kpe2 = prod + pltpu.roll(prod, QK_ROPE, 1)
    kpe = kpe2[:, :QK_ROPE]

    @pl.when(is_prompt)
    def _():
        ckvp_ref[...] = ckv
        kpep_ref[...] = kpe
        vb_ref[...] = ckv.astype(BF16)
        for c in range(tm // KV_CHUNK):
            rows = slice(c * KV_CHUNK, (c + 1) * KV_CHUNK)
            kt_ref[c, :KV_LORA, :] = ckv[rows, :].T.astype(BF16)
            kt_ref[c, KV_LORA:, :] = kpe2[rows, :].T[:QK_ROPE, :].astype(BF16)

    @pl.when(jnp.logical_not(is_prompt))
    def _():
        ckvs_ref[...] = ckv
        kpes_ref[...] = kpe


def _in_even(xp, xs, w_all, qg, kvg, cs, tm=512):
    row = lambda i: (i, 0)
    cpb = tm // KV_CHUNK
    last = N_PROMPT // tm - 1
    return pl.pallas_call(
        functools.partial(_in_even_kernel, tm=tm),
        grid=(N_TOK // tm,),
        in_specs=[
            _prompt_rows(tm, D_MODEL),
            _sample_rows(tm, D_MODEL),
            _resident((D_MODEL, D_IN_EVEN)),
            _resident((1, Q_LORA)),
            _resident((1, KV_LORA)),
            pl.BlockSpec((tm, 2 * QK_ROPE), row),
        ],
        out_specs=[
            pl.BlockSpec((tm, D_POOL), row),
            pl.BlockSpec((tm, Q_LORA), row),
            _prompt_rows(tm, KV_LORA),
            _sample_rows(tm, KV_LORA),
            _prompt_rows(tm, QK_ROPE),
            _sample_rows(tm, QK_ROPE),
            _prompt_rows(tm, KV_LORA),
            pl.BlockSpec((cpb, D_QK, KV_CHUNK), lambda i: (jnp.minimum(i, last), 0, 0)),
        ],
        out_shape=[
            jax.ShapeDtypeStruct((N_TOK, D_POOL), F32),
            jax.ShapeDtypeStruct((N_TOK, Q_LORA), BF16),
            jax.ShapeDtypeStruct((N_PROMPT, KV_LORA), F32),
            jax.ShapeDtypeStruct((N_SAMPLE, KV_LORA), F32),
            jax.ShapeDtypeStruct((N_PROMPT, QK_ROPE), F32),
            jax.ShapeDtypeStruct((N_SAMPLE, QK_ROPE), F32),
            jax.ShapeDtypeStruct((N_PROMPT, KV_LORA), BF16),
            jax.ShapeDtypeStruct((N_PROMPT // KV_CHUNK, D_QK, KV_CHUNK), BF16),
        ],
        compiler_params=_cparams(("arbitrary",)),
        name="in_even",
    )(xp, xs, w_all, qg, kvg, cs)


def _q_proj_kernel(cq_ref, wqn_ref, wqr_ref, wqrr_ref, wukt_ref, cos_ref, sin_ref, qp_ref, qs_ref, *, nsub):
    i = pl.program_id(0)
    is_prompt = i < N_PROMPT // (nsub * Q_BLOCK)
    cq = cq_ref[...]
    qn = _dot(cq, wqn_ref[...])
    cos = jnp.concatenate([cos_ref[...]] * (N_HEADS // 2), axis=1)
    sin = jnp.concatenate([sin_ref[...]] * (N_HEADS // 2), axis=1)
    qpe = (_dot(cq, wqr_ref[...]) * cos + _dot(cq, wqrr_ref[...]) * sin) * ATTN_SCALE

    def emit(q_ref):
        for h in range(N_HEADS):
            ql = _dot(qn[:, h * QK_NOPE:(h + 1) * QK_NOPE].astype(BF16), wukt_ref[h]) * ATTN_SCALE
            for j in range(nsub):
                rows = slice(j * Q_BLOCK, (j + 1) * Q_BLOCK)
                q_ref[j, h, :, :KV_LORA] = ql[rows].astype(q_ref.dtype)
                q_ref[j, h, :, KV_LORA:] = qpe[rows, h * QK_ROPE:(h + 1) * QK_ROPE].astype(q_ref.dtype)

    @pl.when(is_prompt)
    def _():
        emit(qp_ref)

    @pl.when(jnp.logical_not(is_prompt))
    def _():
        emit(qs_ref)


def _q_proj(cq, wqn, wqr, wqrr, wukt, cos, sin, tm=256):
    nsub = tm // Q_BLOCK
    nbp = N_PROMPT // tm
    row = lambda i: (i, 0)
    blk = (nsub, N_HEADS, Q_BLOCK, D_QK)
    return pl.pallas_call(
        functools.partial(_q_proj_kernel, nsub=nsub),
        grid=(N_TOK // tm,),
        in_specs=[
            pl.BlockSpec((tm, Q_LORA), row),
            _resident((Q_LORA, N_HEADS * QK_NOPE)),
            _resident((Q_LORA, N_HEADS * QK_ROPE)),
            _resident((Q_LORA, N_HEADS * QK_ROPE)),
            _resident((N_HEADS, QK_NOPE, KV_LORA)),
            pl.BlockSpec((tm, 2 * QK_ROPE), row),
            pl.BlockSpec((tm, 2 * QK_ROPE), row),
        ],
        out_specs=[
            pl.BlockSpec(blk, lambda i: (jnp.minimum(i, nbp - 1), 0, 0, 0)),
            pl.BlockSpec(blk, lambda i: (jnp.maximum(i - nbp, 0), 0, 0, 0)),
        ],
        out_shape=[
            jax.ShapeDtypeStruct((N_PROMPT // Q_BLOCK, N_HEADS, Q_BLOCK, D_QK), BF16),
            jax.ShapeDtypeStruct((N_SAMPLE // Q_BLOCK, N_HEADS, Q_BLOCK, D_QK), F32),
        ],
        compiler_params=_cparams(("arbitrary",)),
        name="q_proj",
    )(cq, wqn, wqr, wqrr, wukt, cos, sin)


def _pool_groups(read, pos, w_ref, scale_ref, write):
    for g, w in enumerate(POOL_WINDOWS):
        lanes = slice(g * POOL_GROUP, (g + 1) * POOL_GROUP)
        cur = read(0, lanes)
        win = cur
        for j in range(1, w):
            win = win + read(j, lanes)
        cnt = jnp.minimum(pos + 1, w).astype(F32)
        diff = (win / cnt - cur).astype(BF16)
        write(lanes, _dot(diff, w_ref[g]) * scale_ref[:, lanes])


def _pool_prompt_kernel(u_ref, w_ref, scale_ref, y_ref, xp_ref, *, T):
    t = pl.program_id(1)

    @pl.when(t == 0)
    def _():
        xp_ref[0:POOL_HIST, :] = jnp.zeros((POOL_HIST, D_POOL), F32)

    xp_ref[POOL_HIST:POOL_HIST + T, :] = u_ref[...]
    pos = t * T + lax.broadcasted_iota(jnp.int32, (T, 1), 0)

    def read(j, lanes):
        return xp_ref[POOL_HIST - j:POOL_HIST - j + T, lanes]

    def write(lanes, y):
        y_ref[:, lanes] = y.astype(y_ref.dtype)

    _pool_groups(read, pos, w_ref, scale_ref, write)
    xp_ref[0:POOL_HIST, :] = xp_ref[T:T + POOL_HIST, :]


def _pool_prompt(u, pool_w, pool_scale, T=512):
    nt = SEQ // T
    return pl.pallas_call(
        functools.partial(_pool_prompt_kernel, T=T),
        grid=(BATCH, nt),
        in_specs=[
            pl.BlockSpec((T, D_POOL), lambda b, t: (b * nt + t, 0)),
            _resident((N_POOL_GROUPS, POOL_GROUP, POOL_GROUP)),
            _resident((1, D_POOL)),
        ],
        out_specs=pl.BlockSpec((T, D_POOL), lambda b, t: (b * nt + t, 0)),
        out_shape=jax.ShapeDtypeStruct((N_PROMPT, D_POOL), BF16),
        scratch_shapes=[pltpu.VMEM((T + POOL_HIST, D_POOL), F32)],
        compiler_params=_cparams(("arbitrary", "arbitrary")),
        name="pool_prompt",
    )(u, pool_w, pool_scale)


def _pool_sample_kernel(u_ref, hist_ref, w_ref, scale_ref, y_ref, xp_ref, *, S):
    xp_ref[:, 0:POOL_HIST, :] = hist_ref[...]
    xp_ref[:, POOL_HIST:, :] = u_ref[...]
    pos = PAST_LEN + lax.broadcasted_iota(jnp.int32, (S, DEC_SEQ, 1), 1)
    pos = pos.reshape(S * DEC_SEQ, 1)

    def read(j, lanes):
        return xp_ref[:, POOL_HIST - j:POOL_HIST - j + DEC_SEQ, lanes].reshape(S * DEC_SEQ, POOL_GROUP)

    def write(lanes, y):
        y_ref[:, lanes] = y.astype(y_ref.dtype)

    _pool_groups(read, pos, w_ref, scale_ref, write)


def _pool_sample(u3, hist, pool_w, pool_scale, S=32):
    rows = S * DEC_SEQ
    return pl.pallas_call(
        functools.partial(_pool_sample_kernel, S=S),
        grid=(DEC_BATCH // S,),
        in_specs=[
            pl.BlockSpec((S, DEC_SEQ, D_POOL), lambda i: (i, 0, 0)),
            pl.BlockSpec((S, POOL_HIST, D_POOL), lambda i: (i, 0, 0)),
            _resident((N_POOL_GROUPS, POOL_GROUP, POOL_GROUP)),
            _resident((1, D_POOL)),
        ],
        out_specs=pl.BlockSpec((rows, D_POOL), lambda i: (i, 0)),
        out_shape=jax.ShapeDtypeStruct((N_SAMPLE, D_POOL), BF16),
        scratch_shapes=[pltpu.VMEM((S, POOL_HIST + DEC_SEQ, D_POOL), F32)],
        compiler_params=_cparams(("arbitrary",)),
        name="pool_sample",
    )(u3, hist, pool_w, pool_scale)


def _prompt_attn_block(i, q_ref, kt_ref, v_ref, wuv_ref, o_ref, m_ref, l_ref, acc_ref, sa_ref, sb_ref, heads):
    rows_all = heads * Q_BLOCK
    m_ref[...] = jnp.full((rows_all, 1), -jnp.inf, F32)
    l_ref[...] = jnp.zeros((rows_all, 1), F32)
    acc_ref[...] = jnp.zeros((rows_all, KV_LORA), F32)
    q = q_ref[0].reshape(rows_all, D_QK)
    q_pos = i * Q_BLOCK + (lax.broadcasted_iota(jnp.int32, (rows_all, KV_CHUNK), 0) & (Q_BLOCK - 1))
    key = lax.broadcasted_iota(jnp.int32, (rows_all, KV_CHUNK), 1)

    def scores(j, s_ref):
        s = _dot(q, kt_ref[j])
        s_ref[...] = jnp.where(j * KV_CHUNK + key <= q_pos, s, -jnp.inf)

    def attend(j, s_ref):
        v = v_ref[pl.ds(pl.multiple_of(j * KV_CHUNK, KV_CHUNK), KV_CHUNK), :]
        m_new, l_new, acc_new = _softmax_update(m_ref[...], l_ref[...], acc_ref[...], s_ref[...], v)
        m_ref[...] = m_new
        l_ref[...] = l_new
        acc_ref[...] = acc_new

    n_chunks = (i * Q_BLOCK + Q_BLOCK + KV_CHUNK - 1) // KV_CHUNK
    n_pairs = (n_chunks - 1) // 2
    scores(0, sa_ref)

    def body(t, c):
        scores(2 * t + 1, sb_ref)
        attend(2 * t, sa_ref)
        scores(2 * t + 2, sa_ref)
        attend(2 * t + 1, sb_ref)
        return c

    lax.fori_loop(0, n_pairs, body, 0)
    done = 2 * n_pairs

    @pl.when(done + 2 == n_chunks)
    def _():
        scores(done + 1, sb_ref)
        attend(done, sa_ref)
        attend(done + 1, sb_ref)

    @pl.when(done + 1 == n_chunks)
    def _():
        attend(done, sa_ref)

    for h in range(heads):
        rows = slice(h * Q_BLOCK, (h + 1) * Q_BLOCK)
        o_lat = (acc_ref[rows] * (1.0 / l_ref[rows])).astype(BF16)
        o_ref[:, h * V_HEAD:(h + 1) * V_HEAD] = _dot(o_lat, wuv_ref[h]).astype(o_ref.dtype)


PAGES_PER_CHUNK = 16
CHUNK_KEYS = PAGES_PER_CHUNK * PAGE_SIZE
CHUNKS_PER_SEQ = N_PAGES // PAGES_PER_CHUNK
PAGES_PER_GROUP = KV_LORA // PAGE_SIZE
GROUP_KEYS = PAGES_PER_GROUP * PAGE_SIZE
ROPE_ROWS = CHUNK_KEYS // GROUP_KEYS * QK_ROPE
DEC_SLOTS = 4
DEC_AHEAD = DEC_SLOTS - 1
NEW_KEYS_PAD = 16
ATTN_HEADS_PER_STEP = N_HEADS // 2
BLOCKS_PER_SEQ = SEQ // Q_BLOCK
assert CHUNKS_PER_SEQ % DEC_SLOTS == 0
assert DEC_BATCH * ATTN_HEADS_PER_STEP == BATCH * BLOCKS_PER_SEQ * N_HEADS


def _prompt_block(k):
    b, r = k // BLOCKS_PER_SEQ, k % BLOCKS_PER_SEQ
    return b, jnp.where(r % 2 == 0, r // 2, BLOCKS_PER_SEQ - 1 - r // 2)


def _attn_kernel(pt_ref, q_ref, ckvn_ref, kpen_ref, cckv_hbm, ckpet_hbm, qp_ref, kt_ref, v_ref, wuv_ref,
                 o_ref, op_ref, kv_buf, kvb_a, kvb_b, sc_a, sc_b, sem,
                 m_ref, l_ref, acc_ref, sa_ref, sb_ref):
    s = pl.program_id(0)
    nseq = pl.num_programs(0)
    rows = N_HEADS * DEC_SEQ

    def rope_rows(group):
        return pl.ds(CHUNK_KEYS + group * QK_ROPE, QK_ROPE)

    def start_chunk(seq, c):
        slot = c % DEC_SLOTS
        for p in range(PAGES_PER_CHUNK):
            page = pt_ref[(seq * CHUNKS_PER_SEQ + c) * PAGES_PER_CHUNK + p]
            lanes = pl.ds(p % PAGES_PER_GROUP * PAGE_SIZE, PAGE_SIZE)
            pltpu.make_async_copy(cckv_hbm.at[page], kv_buf.at[slot, pl.ds(p * PAGE_SIZE, PAGE_SIZE), :],
                                  sem.at[slot]).start()
            pltpu.make_async_copy(ckpet_hbm.at[page], kv_buf.at[slot, rope_rows(p // PAGES_PER_GROUP), lanes],
                                  sem.at[slot]).start()

    def wait_chunk(c):
        slot = c % DEC_SLOTS
        pltpu.make_async_copy(kv_buf.at[slot], kv_buf.at[slot], sem.at[slot]).wait()

    @pl.when(s == 0)
    def _():
        for c in range(DEC_AHEAD):
            start_chunk(0, c)

    q = q_ref[0].reshape(rows, D_QK)
    q_lat = q[:, :KV_LORA].astype(BF16)
    q_pe = q[:, KV_LORA:].astype(BF16)

    def scores(c, kvb_ref, sc_ref):
        nxt = c + DEC_AHEAD
        if nxt < CHUNKS_PER_SEQ:
            start_chunk(s, nxt)
        else:
            @pl.when(s + 1 < nseq)
            def _():
                start_chunk(s + 1, nxt - CHUNKS_PER_SEQ)
        wait_chunk(c)
        slot = c % DEC_SLOTS
        kvb_ref[...] = kv_buf[slot, :CHUNK_KEYS, :].astype(BF16)
        lat = _dot_nt(q_lat, kvb_ref[...])
        for g in range(CHUNK_KEYS // GROUP_KEYS):
            keys = slice(g * GROUP_KEYS, (g + 1) * GROUP_KEYS)
            sc_ref[:, keys] = lat[:, keys] + _dot(q_pe, kv_buf[slot, rope_rows(g), :].astype(BF16))

    def attend(carry, kvb_ref, sc_ref):
        return _softmax_update(*carry, sc_ref[...], kvb_ref[...])

    bufs = ((kvb_a, sc_a), (kvb_b, sc_b))
    carry = (jnp.full((rows, 1), -jnp.inf, F32), jnp.zeros((rows, 1), F32), jnp.zeros((rows, KV_LORA), F32))
    scores(0, *bufs[0])
    for c in range(1, CHUNKS_PER_SEQ):
        scores(c, *bufs[c % 2])
        carry = attend(carry, *bufs[(c - 1) % 2])
        if c == CHUNKS_PER_SEQ // 2:
            _, blk = _prompt_block(s // 2)
            _prompt_attn_block(blk, qp_ref, kt_ref, v_ref, wuv_ref, op_ref, m_ref, l_ref, acc_ref,
                               sa_ref, sb_ref, ATTN_HEADS_PER_STEP)
    carry = attend(carry, *bufs[(CHUNKS_PER_SEQ - 1) % 2])

    pad = NEW_KEYS_PAD - DEC_SEQ
    kvn = jnp.concatenate([ckvn_ref[...], jnp.zeros((pad, KV_LORA), F32)], axis=0).astype(BF16)
    kpn = jnp.concatenate([kpen_ref[...], jnp.zeros((pad, QK_ROPE), F32)], axis=0).astype(BF16)
    sc = _dot_nt(q_lat, kvn) + _dot_nt(q_pe, kpn)
    tok = lax.broadcasted_iota(jnp.int32, (rows, NEW_KEYS_PAD), 0) & (DEC_SEQ - 1)
    key = lax.broadcasted_iota(jnp.int32, (rows, NEW_KEYS_PAD), 1)
    sc = jnp.where(key <= tok, sc, -jnp.inf)
    _, l_fin, acc = _softmax_update(*carry, sc, kvn)
    o_ref[0] = (acc * (1.0 / l_fin)).reshape(N_HEADS, DEC_SEQ, KV_LORA)


def _attention(page_table, qs, ckv_s, kpe_s, cache_ckv, cache_kpet, qp, kt, vb, wuv):
    seq_per_blk = Q_BLOCK // DEC_SEQ
    hps = ATTN_HEADS_PER_STEP
    halves = N_HEADS // hps
    cps = SEQ // KV_CHUNK
    rows_p = hps * Q_BLOCK

    def qblock(s):
        b, i = _prompt_block(s // halves)
        return b * BLOCKS_PER_SEQ + i

    grid_spec = pltpu.PrefetchScalarGridSpec(
        num_scalar_prefetch=1,
        grid=(DEC_BATCH,),
        in_specs=[
            pl.BlockSpec((1, N_HEADS, DEC_SEQ, D_QK), lambda s, pt: (s // seq_per_blk, 0, s % seq_per_blk, 0)),
            pl.BlockSpec((DEC_SEQ, KV_LORA), lambda s, pt: (s, 0)),
            pl.BlockSpec((DEC_SEQ, QK_ROPE), lambda s, pt: (s, 0)),
            pl.BlockSpec(memory_space=pl.ANY),
            pl.BlockSpec(memory_space=pl.ANY),
            pl.BlockSpec((1, hps, Q_BLOCK, D_QK), lambda s, pt: (qblock(s), s % halves, 0, 0)),
            pl.BlockSpec((cps, D_QK, KV_CHUNK), lambda s, pt: (_prompt_block(s // halves)[0], 0, 0)),
            pl.BlockSpec((SEQ, KV_LORA), lambda s, pt: (_prompt_block(s // halves)[0], 0)),
            pl.BlockSpec((hps, KV_LORA, V_HEAD), lambda s, pt: (s % halves, 0, 0)),
        ],
        out_specs=[
            pl.BlockSpec((1, N_HEADS, DEC_SEQ, KV_LORA), lambda s, pt: (s // seq_per_blk, 0, s % seq_per_blk, 0)),
            pl.BlockSpec((Q_BLOCK, hps * V_HEAD), lambda s, pt: (qblock(s), s % halves)),
        ],
        scratch_shapes=[
            pltpu.VMEM((DEC_SLOTS, CHUNK_KEYS + ROPE_ROWS, KV_LORA), F32),
            pltpu.VMEM((CHUNK_KEYS, KV_LORA), BF16),
            pltpu.VMEM((CHUNK_KEYS, KV_LORA), BF16),
            pltpu.VMEM((N_HEADS * DEC_SEQ, CHUNK_KEYS), F32),
            pltpu.VMEM((N_HEADS * DEC_SEQ, CHUNK_KEYS), F32),
            pltpu.SemaphoreType.DMA((DEC_SLOTS,)),
            pltpu.VMEM((rows_p, 1), F32),
            pltpu.VMEM((rows_p, 1), F32),
            pltpu.VMEM((rows_p, KV_LORA), F32),
            pltpu.VMEM((rows_p, KV_CHUNK), F32),
            pltpu.VMEM((rows_p, KV_CHUNK), F32),
        ],
    )
    return pl.pallas_call(
        _attn_kernel,
        grid_spec=grid_spec,
        out_shape=[
            jax.ShapeDtypeStruct((N_SAMPLE // Q_BLOCK, N_HEADS, Q_BLOCK, KV_LORA), F32),
            jax.ShapeDtypeStruct((N_PROMPT, D_MLA_OUT), BF16),
        ],
        compiler_params=_cparams(("arbitrary",)),
        name="attention",
    )(page_table.reshape(-1), qs, ckv_s, kpe_s, cache_ckv, cache_kpet, qp, kt, vb, wuv)


def _uv_proj_kernel(ol_ref, wuv_ref, o_ref, *, nsub):
    for j in range(nsub):
        for h in range(N_HEADS):
            o = _dot(ol_ref[j, h].astype(BF16), wuv_ref[h])
            o_ref[j * Q_BLOCK:(j + 1) * Q_BLOCK, h * V_HEAD:(h + 1) * V_HEAD] = o.astype(o_ref.dtype)


def _uv_proj_sample(o_lat, wuv, nsub=4):
    nb = o_lat.shape[0]
    rows = nsub * Q_BLOCK
    return pl.pallas_call(
        functools.partial(_uv_proj_kernel, nsub=nsub),
        grid=(nb // nsub,),
        in_specs=[
            pl.BlockSpec((nsub, N_HEADS, Q_BLOCK, KV_LORA), lambda i: (i, 0, 0, 0)),
            _resident((N_HEADS, KV_LORA, V_HEAD)),
        ],
        out_specs=pl.BlockSpec((rows, D_MLA_OUT), lambda i: (i, 0)),
        out_shape=jax.ShapeDtypeStruct((nb * Q_BLOCK, D_MLA_OUT), BF16),
        compiler_params=_cparams(("arbitrary",)),
        name="uv_proj_sample",
    )(o_lat, wuv)


def _mm_ln_kernel(*refs, lhs_split, split_resid, split_out, tm):
    is_prompt = pl.program_id(0) < N_PROMPT // tm
    refs = list(refs)

    def take(split):
        if split:
            p_ref, s_ref = refs.pop(0), refs.pop(0)
            return lambda rows: jnp.where(is_prompt, p_ref[rows, :], s_ref[rows, :])
        ref = refs.pop(0)
        return lambda rows: ref[rows, :]

    lhs = [take(split) for split in lhs_split]
    w_ref = refs.pop(0)
    resid = take(split_resid)
    g_ref, b_ref = refs.pop(0), refs.pop(0)
    out_refs = refs

    sub = min(tm, FFN_SUB_ROWS)
    for r in range(0, tm, sub):
        rows = slice(r, r + sub)
        acc = None
        off = 0
        for piece in lhs:
            a = piece(rows)
            part = _dot(a, w_ref[off:off + a.shape[1], :])
            acc = part if acc is None else acc + part
            off += a.shape[1]
        y = _layer_norm(DEEPNORM_ALPHA * resid(rows) + acc, g_ref[...], b_ref[...])
        if split_out:
            @pl.when(is_prompt)
            def _():
                out_refs[0][rows, :] = y

            @pl.when(jnp.logical_not(is_prompt))
            def _():
                out_refs[1][rows, :] = y
        else:
            out_refs[0][rows, :] = y
            out_refs[1][rows, :] = y.astype(BF16)


def _mm_ln(lhs_list, w, resid, g, b, *, tm, split_out=False, layer=None):
    operands, in_specs = [], []

    def add_rows(v):
        if isinstance(v, tuple):
            operands.extend(v)
            in_specs.extend([_prompt_rows(tm, v[0].shape[1]), _sample_rows(tm, v[0].shape[1])])
        else:
            operands.append(v)
            in_specs.append(pl.BlockSpec((tm, v.shape[1]), lambda i: (i, 0)))
        return isinstance(v, tuple)

    lhs_split = tuple(add_rows(v) for v in lhs_list)
    operands.append(w)
    if layer is None:
        in_specs.append(_resident(w.shape))
    else:
        in_specs.append(pl.BlockSpec((None,) + w.shape[1:], lambda i: (layer, 0, 0), pipeline_mode=pl.Buffered(1)))
    split_resid = add_rows(resid)
    operands.extend([g, b])
    in_specs.extend([_resident((1, D_MODEL)), _resident((1, D_MODEL))])
    if split_out:
        out_specs = [_prompt_rows(tm, D_MODEL), _sample_rows(tm, D_MODEL)]
        out_shape = [jax.ShapeDtypeStruct((N_PROMPT, D_MODEL), F32), jax.ShapeDtypeStruct((N_SAMPLE, D_MODEL), F32)]
    else:
        row = pl.BlockSpec((tm, D_MODEL), lambda i: (i, 0))
        out_specs = [row, row]
        out_shape = [jax.ShapeDtypeStruct((N_TOK, D_MODEL), F32), jax.ShapeDtypeStruct((N_TOK, D_MODEL), BF16)]
    return pl.pallas_call(
        functools.partial(_mm_ln_kernel, lhs_split=lhs_split, split_resid=split_resid, split_out=split_out, tm=tm),
        grid=(N_TOK // tm,),
        in_specs=in_specs,
        out_specs=out_specs,
        out_shape=out_shape,
        compiler_params=_cparams(("arbitrary",)),
        name="mm_ln",
    )(*operands)


def _ffn_up_kernel(x_ref, wg_ref, wu_ref, h_ref, wgb_ref, wub_ref):
    @pl.when(pl.program_id(1) == 0)
    def _():
        wgb_ref[...] = wg_ref[...].astype(BF16)
        wub_ref[...] = wu_ref[...].astype(BF16)

    for r in range(0, x_ref.shape[0], FFN_SUB_ROWS):
        x = x_ref[r:r + FFN_SUB_ROWS, :]
        h = jax.nn.silu(_dot(x, wgb_ref[...])) * _dot(x, wub_ref[...])
        h_ref[r:r + FFN_SUB_ROWS, :] = h.astype(h_ref.dtype)


def _ffn_up(xb, wg, wu, layer, tm=N_TOK // 4, tn=512):
    wblk = pl.BlockSpec((None, D_MODEL, tn), lambda j, i: (layer, 0, j))
    return pl.pallas_call(
        _ffn_up_kernel,
        grid=(D_FF // tn, N_TOK // tm),
        in_specs=[pl.BlockSpec((tm, D_MODEL), lambda j, i: (i, 0)), wblk, wblk],
        out_specs=pl.BlockSpec((tm, tn), lambda j, i: (i, j)),
        out_shape=jax.ShapeDtypeStruct((N_TOK, D_FF), BF16),
        scratch_shapes=[pltpu.VMEM((D_MODEL, tn), BF16), pltpu.VMEM((D_MODEL, tn), BF16)],
        compiler_params=_cparams(("arbitrary", "arbitrary")),
        name="ffn_up",
    )(xb, wg, wu)


def _conv_taps(read, cw_ref):
    conv = read(CONV_W - 1) * cw_ref[0:1, :]
    for j in range(1, CONV_W):
        conv = conv + read(CONV_W - 1 - j) * cw_ref[j:j + 1, :]
    return conv


def _in_odd_cast_weights(wv_ref, wb_ref, wc_ref, wvb_ref, wbb_ref, wcb_ref):
    @pl.when(pl.program_id(1) == 0)
    def _():
        wvb_ref[...] = wv_ref[...].astype(BF16)
        wbb_ref[...] = wb_ref[...].astype(BF16)
        wcb_ref[...] = wc_ref[...].astype(BF16)


def _in_odd_prompt_kernel(x_ref, wv_ref, wb_ref, wc_ref, cw_ref, g_ref, ztail_ref,
                          wvb_ref, wbb_ref, wcb_ref, zp_ref, *, tm):
    i = pl.program_id(1)
    tiles_per_seq = SEQ // tm
    _in_odd_cast_weights(wv_ref, wb_ref, wc_ref, wvb_ref, wbb_ref, wcb_ref)

    @pl.when(i % tiles_per_seq == 0)
    def _():
        zp_ref[0:CONV_HIST, :] = jnp.zeros((CONV_HIST, zp_ref.shape[1]), F32)

    for r in range(0, tm, FFN_SUB_ROWS):
        x = x_ref[r:r + FFN_SUB_ROWS, :]
        new = slice(CONV_HIST + r, CONV_HIST + r + FFN_SUB_ROWS)
        zp_ref[new, :] = _dot(x, wcb_ref[...]) * _dot(x, wvb_ref[...])
        conv = _conv_taps(lambda back: zp_ref[new.start - back:new.stop - back, :], cw_ref)
        g_ref[r:r + FFN_SUB_ROWS, :] = (_dot(x, wbb_ref[...]) * conv).astype(g_ref.dtype)
    zp_ref[0:CONV_HIST, :] = zp_ref[tm:tm + CONV_HIST, :]

    @pl.when(i % tiles_per_seq == tiles_per_seq - 1)
    def _():
        ztail_ref[...] = zp_ref[tm:tm + CONV_HIST, :]


def _in_odd_sample_kernel(x_ref, wv_ref, wb_ref, wc_ref, zs_ref, gbs_ref, wvb_ref, wbb_ref, wcb_ref):
    _in_odd_cast_weights(wv_ref, wb_ref, wc_ref, wvb_ref, wbb_ref, wcb_ref)
    x = x_ref[...]
    zs_ref[...] = _dot(x, wcb_ref[...]) * _dot(x, wvb_ref[...])
    gbs_ref[...] = _dot(x, wbb_ref[...])


def _in_odd(xb, w_in, conv_w, tm=1024, tn=512):
    nb = D_MODEL // tn
    nbp = N_PROMPT // tm
    tiles_per_seq = SEQ // tm
    wblks = [pl.BlockSpec((D_MODEL, tn), lambda j, i, part=part: (0, part * nb + j)) for part in range(3)]
    wscratch = [pltpu.VMEM((D_MODEL, tn), BF16)] * 3
    g_prompt, z_tail = pl.pallas_call(
        functools.partial(_in_odd_prompt_kernel, tm=tm),
        grid=(nb, nbp),
        in_specs=[pl.BlockSpec((tm, D_MODEL), lambda j, i: (i, 0))] + wblks
        + [pl.BlockSpec((CONV_W, tn), lambda j, i: (0, j))],
        out_specs=[
            pl.BlockSpec((tm, tn), lambda j, i: (i, j)),
            pl.BlockSpec((CONV_HIST, tn), lambda j, i: (i // tiles_per_seq, j)),
        ],
        out_shape=[
            jax.ShapeDtypeStruct((N_PROMPT, D_MODEL), BF16),
            jax.ShapeDtypeStruct((BATCH * CONV_HIST, D_MODEL), F32),
        ],
        scratch_shapes=wscratch + [pltpu.VMEM((tm + CONV_HIST, tn), F32)],
        compiler_params=_cparams(("arbitrary", "arbitrary")),
        name="in_odd_prompt",
    )(xb, w_in, w_in, w_in, conv_w)
    sample_blk = pl.BlockSpec((tm, tn), lambda j, i: (i, j))
    z_s, gb_s = pl.pallas_call(
        _in_odd_sample_kernel,
        grid=(nb, N_SAMPLE // tm),
        in_specs=[pl.BlockSpec((tm, D_MODEL), lambda j, i: (nbp + i, 0))] + wblks,
        out_specs=[sample_blk, sample_blk],
        out_shape=[jax.ShapeDtypeStruct((N_SAMPLE, D_MODEL), F32)] * 2,
        scratch_shapes=wscratch,
        compiler_params=_cparams(("arbitrary", "arbitrary")),
        name="in_odd_sample",
    )(xb, w_in, w_in, w_in)
    return g_prompt, z_tail, z_s, gb_s


def _conv_sample_kernel(z_ref, hist_ref, gb_ref, cw_ref, g_ref, zp_ref, *, S):
    zp_ref[:, 0:CONV_HIST, :] = hist_ref[...]
    zp_ref[:, CONV_HIST:, :] = z_ref[...]
    conv = _conv_taps(lambda back: zp_ref[:, CONV_HIST - back:CONV_HIST - back + DEC_SEQ, :], cw_ref)
    g_ref[...] = (gb_ref[...] * conv).reshape(S * DEC_SEQ, D_MODEL).astype(g_ref.dtype)


def _conv_sample(z3, hist, gb3, conv_w, S=32):
    blk = lambda i: (i, 0, 0)
    rows = S * DEC_SEQ
    return pl.pallas_call(
        functools.partial(_conv_sample_kernel, S=S),
        grid=(DEC_BATCH // S,),
        in_specs=[
            pl.BlockSpec((S, DEC_SEQ, D_MODEL), blk),
            pl.BlockSpec((S, CONV_HIST, D_MODEL), blk),
            pl.BlockSpec((S, DEC_SEQ, D_MODEL), blk),
            _resident((CONV_W, D_MODEL)),
        ],
        out_specs=pl.BlockSpec((rows, D_MODEL), lambda i: (i, 0)),
        out_shape=jax.ShapeDtypeStruct((N_SAMPLE, D_MODEL), BF16),
        scratch_shapes=[pltpu.VMEM((S, CONV_HIST + DEC_SEQ, D_MODEL), F32)],
        compiler_params=_cparams(("arbitrary",)),
        name="conv_sample",
    )(z3, hist, gb3, conv_w)


def _rope_tables():
    pos = jnp.concatenate([jnp.tile(jnp.arange(SEQ), BATCH),
                           jnp.tile(PAST_LEN + jnp.arange(DEC_SEQ), DEC_BATCH)])
    inv = ROPE_THETA ** (-jnp.arange(0, QK_ROPE, 2, dtype=F32) / QK_ROPE)
    ang = pos.astype(F32)[:, None] * inv[None, :]
    cos, sin = jnp.cos(ang), jnp.sin(ang)
    cos2 = jnp.concatenate([cos, cos], axis=1)
    sin2 = jnp.concatenate([-sin, sin], axis=1)
    return jnp.tile(cos2, (1, 2)), jnp.tile(sin2, (1, 2)), jnp.concatenate([cos2, sin2], axis=1)


def _swap_halves(w):
    half = QK_ROPE // 2
    return jnp.concatenate([w[..., half:], w[..., :half]], axis=-1)


def kernel(x_prompt, x_sample, cache_ckv, cache_kpe, state_pool, state_conv, page_table, w_in_even, pool_w, pool_scale, q_norm_g, w_q_b, kv_norm_g, w_uk, w_uv, w_out_even, w_in_odd, conv_w, w_out_odd, ffn_w_gate, ffn_w_up, ffn_w_down, ln1_g, ln1_b, ln2_g, ln2_b):
    assert DEPTH == 2
    n_pool_pages = cache_ckv.shape[1]
    cos, sin, cos_sin = _rope_tables()
    xp = x_prompt.reshape(N_PROMPT, D_MODEL)
    xs = x_sample.reshape(N_SAMPLE, D_MODEL)

    def row(v):
        return v.reshape(1, -1)

    w_down = ffn_w_down.astype(BF16)

    def ffn(layer, xf, xb, last):
        hmid = _ffn_up(xb, ffn_w_gate, ffn_w_up, layer)
        return _mm_ln([hmid], w_down, xf, row(ln2_g[layer]), row(ln2_b[layer]),
                      tm=256, split_out=last, layer=layer)

    e = 0
    w_in = w_in_even[e]
    w_all = jnp.concatenate([w_in, _swap_halves(w_in[:, D_POOL + Q_LORA + KV_LORA:])], axis=1).astype(BF16)
    u, cq, ckv_p, ckv_s, kpe_p, kpe_s, vb, kt = _in_even(
        xp, xs, w_all, row(q_norm_g[e]), row(kv_norm_g[e]), cos_sin)

    wq = w_q_b[e]
    wq_rope = wq[:, :, QK_NOPE:]
    qp, qs = _q_proj(
        cq,
        wq[:, :, :QK_NOPE].reshape(Q_LORA, N_HEADS * QK_NOPE).astype(BF16),
        wq_rope.reshape(Q_LORA, N_HEADS * QK_ROPE).astype(BF16),
        _swap_halves(wq_rope).reshape(Q_LORA, N_HEADS * QK_ROPE).astype(BF16),
        jnp.transpose(w_uk[e], (1, 2, 0)).astype(BF16), cos, sin)

    pw = pool_w[e].astype(BF16)
    ps = row(pool_scale[e])
    u_sample = u[N_PROMPT:].reshape(DEC_BATCH, DEC_SEQ, D_POOL)
    pool_hist = jnp.pad(state_pool[e], ((0, 0), (POOL_HIST - POOL_BUF, 0), (0, 0)))
    y_pool = (_pool_prompt(u, pw, ps), _pool_sample(u_sample, pool_hist, pw, ps))

    wuv = jnp.transpose(w_uv[e], (1, 0, 2)).astype(BF16)
    o_lat_s, o_prompt = _attention(page_table, qs, ckv_s, kpe_s,
                                   cache_ckv[e].reshape(n_pool_pages, PAGE_SIZE, KV_LORA),
                                   jnp.swapaxes(cache_kpe[e], 1, 2), qp, kt, vb, wuv)
    o_sample = _uv_proj_sample(o_lat_s, wuv)
    xf, xb = _mm_ln([y_pool, (o_prompt, o_sample)], w_out_even[e].astype(BF16), (xp, xs),
                    row(ln1_g[0]), row(ln1_b[0]), tm=512)
    xf, xb = ffn(0, xf, xb, False)

    od = 0
    g_prompt, z_tail, z_s, gb_s = _in_odd(xb, w_in_odd[od], conv_w[od])
    z_sample = z_s.reshape(DEC_BATCH, DEC_SEQ, D_MODEL)
    gb_sample = gb_s.reshape(DEC_BATCH, DEC_SEQ, D_MODEL)
    conv_hist = jnp.pad(state_conv[od], ((0, 0), (CONV_HIST - CONV_BUF, 0), (0, 0)))
    g = (g_prompt, _conv_sample(z_sample, conv_hist, gb_sample, conv_w[od]))
    xf, xb = _mm_ln([g], w_out_odd[od].astype(BF16), xf, row(ln1_g[1]), row(ln1_b[1]), tm=512)
    y_p, y_s = ffn(1, xf, xb, True)

    pool_prompt = jnp.stack([u[(b + 1) * SEQ - POOL_BUF:(b + 1) * SEQ] for b in range(BATCH)])
    pool_sample = jnp.concatenate([state_pool[e][:, DEC_SEQ:], u_sample], axis=1)
    conv_prompt = z_tail.reshape(BATCH, CONV_HIST, D_MODEL)[:, CONV_HIST - CONV_BUF:]
    conv_sample = z_sample[:, DEC_SEQ - CONV_BUF:]
    return (y_p.reshape(BATCH, SEQ, D_MODEL), y_s.reshape(DEC_BATCH, DEC_SEQ, D_MODEL),
            ckv_p.reshape(1, BATCH, SEQ, KV_LORA), kpe_p.reshape(1, BATCH, SEQ, QK_ROPE),
            pool_prompt[None], conv_prompt[None],
            ckv_s.reshape(1, DEC_BATCH, DEC_SEQ, KV_LORA), kpe_s.reshape(1, DEC_BATCH, DEC_SEQ, QK_ROPE),
            pool_sample[None], conv_sample[None])
```

```python
import functools

import jax
import jax.numpy as jnp
from jax import lax
from jax.experimental import pallas as pl
from jax.experimental.pallas import tpu as pltpu

F32 = jnp.float32
BF16 = jnp.bfloat16

D_MODEL = 2048
BATCH = 4
SEQ = 2048
DEPTH = 2
DEC_BATCH = 128
DEC_SEQ = 8
PAST_LEN = 16384
PAGE_SIZE = 128
N_PAGES = PAST_LEN // PAGE_SIZE
POOL_WINDOWS = (2, 4, 8, 16)
N_POOL_GROUPS = len(POOL_WINDOWS)
D_POOL = D_MODEL // 2
POOL_GROUP = D_POOL // N_POOL_GROUPS
POOL_BUF = max(POOL_WINDOWS) - 1
N_HEADS = 8
QK_NOPE = 128
QK_ROPE = 64
V_HEAD = 128
Q_LORA = 512
KV_LORA = 512
ROPE_THETA = 10000.0
ATTN_SCALE = (QK_NOPE + QK_ROPE) ** -0.5
Q_BLOCK = 128
D_MLA_OUT = N_HEADS * V_HEAD
CONV_W = 3
CONV_BUF = CONV_W - 1
D_FF = ((8 * D_MODEL + 3 * 256 - 1) // (3 * 256)) * 256
DEEPNORM_ALPHA = (2 * DEPTH) ** 0.25

N_PROMPT = BATCH * SEQ
N_SAMPLE = DEC_BATCH * DEC_SEQ
N_TOK = N_PROMPT + N_SAMPLE
D_QK = KV_LORA + QK_ROPE
POOL_HIST = 16
CONV_HIST = 8
KV_CHUNK = 512
FFN_SUB_ROWS = 256
VMEM_LIMIT_MB = 56


def _cparams(dims, vmem_mb=VMEM_LIMIT_MB):
    return pltpu.CompilerParams(dimension_semantics=dims, vmem_limit_bytes=vmem_mb * 1024 * 1024)


def _resident(shape):
    zeros = (0,) * len(shape)
    return pl.BlockSpec(shape, lambda *_: zeros, pipeline_mode=pl.Buffered(1))


def _prompt_rows(tm, width):
    last = N_PROMPT // tm - 1
    return pl.BlockSpec((tm, width), lambda i, *_: (jnp.minimum(i, last), 0))


def _sample_rows(tm, width):
    first = N_PROMPT // tm
    return pl.BlockSpec((tm, width), lambda i, *_: (jnp.maximum(i - first, 0), 0))


def _dot(a, b):
    return jnp.dot(a, b, preferred_element_type=F32)


def _dot_nt(a, b):
    return lax.dot_general(a, b, (((1,), (1,)), ((), ())), preferred_element_type=F32)


def _rms(h, g, eps=1e-6):
    return h * lax.rsqrt(jnp.mean(h * h, axis=-1, keepdims=True) + eps) * g


def _layer_norm(v, g, b, eps=1e-5):
    mu = jnp.mean(v, axis=-1, keepdims=True)
    vc = v - mu
    var = jnp.mean(vc * vc, axis=-1, keepdims=True)
    return vc * lax.rsqrt(var + eps) * g + b


def _softmax_update(m_old, l_old, acc, s, v):
    m_new = jnp.maximum(m_old, jnp.max(s, axis=-1, keepdims=True))
    alpha = jnp.exp(m_old - m_new)
    p = jnp.exp(s - m_new)
    l_new = alpha * l_old + jnp.sum(p, axis=-1, keepdims=True)
    return m_new, l_new, alpha * acc + _dot(p.astype(BF16), v)


D_IN_EVEN = D_POOL + Q_LORA + KV_LORA + 2 * QK_ROPE


def _in_even_kernel(xp_ref, xs_ref, w_ref, qg_ref, kvg_ref, cs_ref,
                    u_ref, cq_ref, ckvp_ref, ckvs_ref, kpep_ref, kpes_ref, vb_ref, kt_ref, *, tm):
    i = pl.program_id(0)
    is_prompt = i < N_PROMPT // tm
    xb = jnp.where(is_prompt, xp_ref[...], xs_ref[...]).astype(BF16)
    h = _dot(xb, w_ref[...])
    o1, o2 = D_POOL + Q_LORA, D_POOL + Q_LORA + KV_LORA
    u_ref[...] = h[:, :D_POOL]
    cq_ref[...] = _rms(h[:, D_POOL:o1], qg_ref[...]).astype(BF16)
    ckv = _rms(h[:, o1:o2], kvg_ref[...])
    prod = h[:, o2:] * cs_ref[...]
    kpe2 = prod + pltpu.roll(prod, QK_ROPE, 1)
    kpe = kpe2[:, :QK_ROPE]

    @pl.when(is_prompt)
    def _():
        ckvp_ref[...] = ckv
        kpep_ref[...] = kpe
        vb_ref[...] = ckv.astype(BF16)
        for c in range(tm // KV_CHUNK):
            rows = slice(c * KV_CHUNK, (c + 1) * KV_CHUNK)
            kt_ref[c, :KV_LORA, :] = ckv[rows, :].T.astype(BF16)
            kt_ref[c, KV_LORA:, :] = kpe2[rows, :].T[:QK_ROPE, :].astype(BF16)

    @pl.when(jnp.logical_not(is_prompt))
    def _():
        ckvs_ref[...] = ckv
        kpes_ref[...] = kpe


def _in_even(xp, xs, w_all, qg, kvg, cs, tm=512):
    row = lambda i: (i, 0)
    cpb = tm // KV_CHUNK
    last = N_PROMPT // tm - 1
    return pl.pallas_call(
        functools.partial(_in_even_kernel, tm=tm),
        grid=(N_TOK // tm,),
        in_specs=[
            _prompt_rows(tm, D_MODEL),
            _sample_rows(tm, D_MODEL),
            _resident((D_MODEL, D_IN_EVEN)),
            _resident((1, Q_LORA)),
            _resident((1, KV_LORA)),
            pl.BlockSpec((tm, 2 * QK_ROPE), row),
        ],
        out_specs=[
            pl.BlockSpec((tm, D_POOL), row),
            pl.BlockSpec((tm, Q_LORA), row),
            _prompt_rows(tm, KV_LORA),
            _sample_rows(tm, KV_LORA),
            _prompt_rows(tm, QK_ROPE),
            _sample_rows(tm, QK_ROPE),
            _prompt_rows(tm, KV_LORA),
            pl.BlockSpec((cpb, D_QK, KV_CHUNK), lambda i: (jnp.minimum(i, last), 0, 0)),
        ],
        out_shape=[
            jax.ShapeDtypeStruct((N_TOK, D_POOL), F32),
            jax.ShapeDtypeStruct((N_TOK, Q_LORA), BF16),
            jax.ShapeDtypeStruct((N_PROMPT, KV_LORA), F32),
            jax.ShapeDtypeStruct((N_SAMPLE, KV_LORA), F32),
            jax.ShapeDtypeStruct((N_PROMPT, QK_ROPE), F32),
            jax.ShapeDtypeStruct((N_SAMPLE, QK_ROPE), F32),
            jax.ShapeDtypeStruct((N_PROMPT, KV_LORA), BF16),
            jax.ShapeDtypeStruct((N_PROMPT // KV_CHUNK, D_QK, KV_CHUNK), BF16),
        ],
        compiler_params=_cparams(("arbitrary",)),
        name="in_even",
    )(xp, xs, w_all, qg, kvg, cs)


def _q_proj_kernel(cq_ref, wqn_ref, wqr_ref, wqrr_ref, wukt_ref, cos_ref, sin_ref, qp_ref, qs_ref, *, nsub):
    i = pl.program_id(0)
    is_prompt = i < N_PROMPT // (nsub * Q_BLOCK)
    cq = cq_ref[...]
    qn = _dot(cq, wqn_ref[...])
    cos = jnp.concatenate([cos_ref[...]] * (N_HEADS // 2), axis=1)
    sin = jnp.concatenate([sin_ref[...]] * (N_HEADS // 2), axis=1)
    qpe = (_dot(cq, wqr_ref[...]) * cos + _dot(cq, wqrr_ref[...]) * sin) * ATTN_SCALE

    def emit(q_ref):
        for h in range(N_HEADS):
            ql = _dot(qn[:, h * QK_NOPE:(h + 1) * QK_NOPE].astype(BF16), wukt_ref[h]) * ATTN_SCALE
            for j in range(nsub):
                rows = slice(j * Q_BLOCK, (j + 1) * Q_BLOCK)
                q_ref[j, h, :, :KV_LORA] = ql[rows].astype(q_ref.dtype)
                q_ref[j, h, :, KV_LORA:] = qpe[rows, h * QK_ROPE:(h + 1) * QK_ROPE].astype(q_ref.dtype)

    @pl.when(is_prompt)
    def _():
        emit(qp_ref)

    @pl.when(jnp.logical_not(is_prompt))
    def _():
        emit(qs_ref)


def _q_proj(cq, wqn, wqr, wqrr, wukt, cos, sin, tm=256):
    nsub = tm // Q_BLOCK
    nbp = N_PROMPT // tm
    row = lambda i: (i, 0)
    blk = (nsub, N_HEADS, Q_BLOCK, D_QK)
    return pl.pallas_call(
        functools.partial(_q_proj_kernel, nsub=nsub),
        grid=(N_TOK // tm,),
        in_specs=[
            pl.BlockSpec((tm, Q_LORA), row),
            _resident((Q_LORA, N_HEADS * QK_NOPE)),
            _resident((Q_LORA, N_HEADS * QK_ROPE)),
            _resident((Q_LORA, N_HEADS * QK_ROPE)),
            _resident((N_HEADS, QK_NOPE, KV_LORA)),
            pl.BlockSpec((tm, 2 * QK_ROPE), row),
            pl.BlockSpec((tm, 2 * QK_ROPE), row),
        ],
        out_specs=[
            pl.BlockSpec(blk, lambda i: (jnp.minimum(i, nbp - 1), 0, 0, 0)),
            pl.BlockSpec(blk, lambda i: (jnp.maximum(i - nbp, 0), 0, 0, 0)),
        ],
        out_shape=[
            jax.ShapeDtypeStruct((N_PROMPT // Q_BLOCK, N_HEADS, Q_BLOCK, D_QK), BF16),
            jax.ShapeDtypeStruct((N_SAMPLE // Q_BLOCK, N_HEADS, Q_BLOCK, D_QK), F32),
        ],
        compiler_params=_cparams(("arbitrary",)),
        name="q_proj",
    )(cq, wqn, wqr, wqrr, wukt, cos, sin)


def _pool_groups(read, pos, w_ref, scale_ref, write):
    for g, w in enumerate(POOL_WINDOWS):
        lanes = slice(g * POOL_GROUP, (g + 1) * POOL_GROUP)
        cur = read(0, lanes)
        win = cur
        for j in range(1, w):
            win = win + read(j, lanes)
        cnt = jnp.minimum(pos + 1, w).astype(F32)
        diff = (win / cnt - cur).astype(BF16)
        write(lanes, _dot(diff, w_ref[g]) * scale_ref[:, lanes])


def _pool_prompt_kernel(u_ref, w_ref, scale_ref, y_ref, xp_ref, *, T):
    t = pl.program_id(1)

    @pl.when(t == 0)
    def _():
        xp_ref[0:POOL_HIST, :] = jnp.zeros((POOL_HIST, D_POOL), F32)

    xp_ref[POOL_HIST:POOL_HIST + T, :] = u_ref[...]
    pos = t * T + lax.broadcasted_iota(jnp.int32, (T, 1), 0)

    def read(j, lanes):
        return xp_ref[POOL_HIST - j:POOL_HIST - j + T, lanes]

    def write(lanes, y):
        y_ref[:, lanes] = y.astype(y_ref.dtype)

    _pool_groups(read, pos, w_ref, scale_ref, write)
    xp_ref[0:POOL_HIST, :] = xp_ref[T:T + POOL_HIST, :]


def _pool_prompt(u, pool_w, pool_scale, T=512):
    nt = SEQ // T
    return pl.pallas_call(
        functools.partial(_pool_prompt_kernel, T=T),
        grid=(BATCH, nt),
        in_specs=[
            pl.BlockSpec((T, D_POOL), lambda b, t: (b * nt + t, 0)),
            _resident((N_POOL_GROUPS, POOL_GROUP, POOL_GROUP)),
            _resident((1, D_POOL)),
        ],
        out_specs=pl.BlockSpec((T, D_POOL), lambda b, t: (b * nt + t, 0)),
        out_shape=jax.ShapeDtypeStruct((N_PROMPT, D_POOL), BF16),
        scratch_shapes=[pltpu.VMEM((T + POOL_HIST, D_POOL), F32)],
        compiler_params=_cparams(("arbitrary", "arbitrary")),
        name="pool_prompt",
    )(u, pool_w, pool_scale)


def _pool_sample_kernel(u_ref, hist_ref, w_ref, scale_ref, y_ref, xp_ref, *, S):
    xp_ref[:, 0:POOL_HIST, :] = hist_ref[...]
    xp_ref[:, POOL_HIST:, :] = u_ref[...]
    pos = PAST_LEN + lax.broadcasted_iota(jnp.int32, (S, DEC_SEQ, 1), 1)
    pos = pos.reshape(S * DEC_SEQ, 1)

    def read(j, lanes):
        return xp_ref[:, POOL_HIST - j:POOL_HIST - j + DEC_SEQ, lanes].reshape(S * DEC_SEQ, POOL_GROUP)

    def write(lanes, y):
        y_ref[:, lanes] = y.astype(y_ref.dtype)

    _pool_groups(read, pos, w_ref, scale_ref, write)


def _pool_sample(u3, hist, pool_w, pool_scale, S=32):
    rows = S * DEC_SEQ
    return pl.pallas_call(
        functools.partial(_pool_sample_kernel, S=S),
        grid=(DEC_BATCH // S,),
        in_specs=[
            pl.BlockSpec((S, DEC_SEQ, D_POOL), lambda i: (i, 0, 0)),
            pl.BlockSpec((S, POOL_HIST, D_POOL), lambda i: (i, 0, 0)),
            _resident((N_POOL_GROUPS, POOL_GROUP, POOL_GROUP)),
            _resident((1, D_POOL)),
        ],
        out_specs=pl.BlockSpec((rows, D_POOL), lambda i: (i, 0)),
        out_shape=jax.ShapeDtypeStruct((N_SAMPLE, D_POOL), BF16),
        scratch_shapes=[pltpu.VMEM((S, POOL_HIST + DEC_SEQ, D_POOL), F32)],
        compiler_params=_cparams(("arbitrary",)),
        name="pool_sample",
    )(u3, hist, pool_w, pool_scale)


def _prompt_attn_block(i, q_ref, kt_ref, v_ref, wuv_ref, o_ref, m_ref, l_ref, acc_ref, sa_ref, sb_ref, heads):
    rows_all = heads * Q_BLOCK
    m_ref[...] = jnp.full((rows_all, 1), -jnp.inf, F32)
    l_ref[...] = jnp.zeros((rows_all, 1), F32)
    acc_ref[...] = jnp.zeros((rows_all, KV_LORA), F32)
    q = q_ref[0].reshape(rows_all, D_QK)
    q_pos = i * Q_BLOCK + (lax.broadcasted_iota(jnp.int32, (rows_all, KV_CHUNK), 0) & (Q_BLOCK - 1))
    key = lax.broadcasted_iota(jnp.int32, (rows_all, KV_CHUNK), 1)

    def scores(j, s_ref):
        s = _dot(q, kt_ref[j])
        s_ref[...] = jnp.where(j * KV_CHUNK + key <= q_pos, s, -jnp.inf)

    def attend(j, s_ref):
        v = v_ref[pl.ds(pl.multiple_of(j * KV_CHUNK, KV_CHUNK), KV_CHUNK), :]
        m_new, l_new, acc_new = _softmax_update(m_ref[...], l_ref[...], acc_ref[...], s_ref[...], v)
        m_ref[...] = m_new
        l_ref[...] = l_new
        acc_ref[...] = acc_new

    n_chunks = (i * Q_BLOCK + Q_BLOCK + KV_CHUNK - 1) // KV_CHUNK
    n_pairs = (n_chunks - 1) // 2
    scores(0, sa_ref)

    def body(t, c):
        scores(2 * t + 1, sb_ref)
        attend(2 * t, sa_ref)
        scores(2 * t + 2, sa_ref)
        attend(2 * t + 1, sb_ref)
        return c

    lax.fori_loop(0, n_pairs, body, 0)
    done = 2 * n_pairs

    @pl.when(done + 2 == n_chunks)
    def _():
        scores(done + 1, sb_ref)
        attend(done, sa_ref)
        attend(done + 1, sb_ref)

    @pl.when(done + 1 == n_chunks)
    def _():
        attend(done, sa_ref)

    for h in range(heads):
        rows = slice(h * Q_BLOCK, (h + 1) * Q_BLOCK)
        o_lat = (acc_ref[rows] * (1.0 / l_ref[rows])).astype(BF16)
        o_ref[:, h * V_HEAD:(h + 1) * V_HEAD] = _dot(o_lat, wuv_ref[h]).astype(o_ref.dtype)


PAGES_PER_CHUNK = 16
CHUNK_KEYS = PAGES_PER_CHUNK * PAGE_SIZE
CHUNKS_PER_SEQ = N_PAGES // PAGES_PER_CHUNK
PAGES_PER_GROUP = KV_LORA // PAGE_SIZE
GROUP_KEYS = PAGES_PER_GROUP * PAGE_SIZE
ROPE_ROWS = CHUNK_KEYS // GROUP_KEYS * QK_ROPE
DEC_SLOTS = 4
DEC_AHEAD = DEC_SLOTS - 1
NEW_KEYS_PAD = 16
ATTN_HEADS_PER_STEP = N_HEADS // 2
BLOCKS_PER_SEQ = SEQ // Q_BLOCK
assert CHUNKS_PER_SEQ % DEC_SLOTS == 0
assert DEC_BATCH * ATTN_HEADS_PER_STEP == BATCH * BLOCKS_PER_SEQ * N_HEADS


def _prompt_block(k):
    b, r = k // BLOCKS_PER_SEQ, k % BLOCKS_PER_SEQ
    return b, jnp.where(r % 2 == 0, r // 2, BLOCKS_PER_SEQ - 1 - r // 2)


def _attn_kernel(pt_ref, q_ref, ckvn_ref, kpen_ref, cckv_hbm, ckpet_hbm, qp_ref, kt_ref, v_ref, wuv_ref,
                 o_ref, op_ref, kv_buf, kvb_a, kvb_b, sc_a, sc_b, sem,
                 m_ref, l_ref, acc_ref, sa_ref, sb_ref):
    s = pl.program_id(0)
    nseq = pl.num_programs(0)
    rows = N_HEADS * DEC_SEQ

    def rope_rows(group):
        return pl.ds(CHUNK_KEYS + group * QK_ROPE, QK_ROPE)

    def start_chunk(seq, c):
        slot = c % DEC_SLOTS
        for p in range(PAGES_PER_CHUNK):
            page = pt_ref[(seq * CHUNKS_PER_SEQ + c) * PAGES_PER_CHUNK + p]
            lanes = pl.ds(p % PAGES_PER_GROUP * PAGE_SIZE, PAGE_SIZE)
            pltpu.make_async_copy(cckv_hbm.at[page], kv_buf.at[slot, pl.ds(p * PAGE_SIZE, PAGE_SIZE), :],
                                  sem.at[slot]).start(priority=1)
            pltpu.make_async_copy(ckpet_hbm.at[page], kv_buf.at[slot, rope_rows(p // PAGES_PER_GROUP), lanes],
                                  sem.at[slot]).start(priority=0)

    def wait_chunk(c):
        slot = c % DEC_SLOTS
        pltpu.make_async_copy(kv_buf.at[slot], kv_buf.at[slot], sem.at[slot]).wait()

    @pl.when(s == 0)
    def _():
        for c in range(DEC_AHEAD):
            start_chunk(0, c)

    q = q_ref[0].reshape(rows, D_QK)
    q_lat = q[:, :KV_LORA].astype(BF16)
    q_pe = q[:, KV_LORA:].astype(BF16)

    def scores(c, kvb_ref, sc_ref):
        nxt = c + DEC_AHEAD
        if nxt < CHUNKS_PER_SEQ:
            start_chunk(s, nxt)
        else:
            @pl.when(s + 1 < nseq)
            def _():
                start_chunk(s + 1, nxt - CHUNKS_PER_SEQ)
        wait_chunk(c)
        slot = c % DEC_SLOTS
        kvb_ref[...] = kv_buf[slot, :CHUNK_KEYS, :].astype(BF16)
        lat = _dot_nt(q_lat, kvb_ref[...])
        for g in range(CHUNK_KEYS // GROUP_KEYS):
            keys = slice(g * GROUP_KEYS, (g + 1) * GROUP_KEYS)
            sc_ref[:, keys] = lat[:, keys] + _dot(q_pe, kv_buf[slot, rope_rows(g), :].astype(BF16))

    def attend(carry, kvb_ref, sc_ref):
        return _softmax_update(*carry, sc_ref[...], kvb_ref[...])

    bufs = ((kvb_a, sc_a), (kvb_b, sc_b))
    carry = (jnp.full((rows, 1), -jnp.inf, F32), jnp.zeros((rows, 1), F32), jnp.zeros((rows, KV_LORA), F32))
    scores(0, *bufs[0])
    for c in range(1, CHUNKS_PER_SEQ):
        scores(c, *bufs[c % 2])
        carry = attend(carry, *bufs[(c - 1) % 2])
        if c == CHUNKS_PER_SEQ // 2:
            _, blk = _prompt_block(s // 2)
            _prompt_attn_block(blk, qp_ref, kt_ref, v_ref, wuv_ref, op_ref, m_ref, l_ref, acc_ref,
                               sa_ref, sb_ref, ATTN_HEADS_PER_STEP)
    carry = attend(carry, *bufs[(CHUNKS_PER_SEQ - 1) % 2])

    pad = NEW_KEYS_PAD - DEC_SEQ
    kvn = jnp.concatenate([ckvn_ref[...], jnp.zeros((pad, KV_LORA), F32)], axis=0).astype(BF16)
    kpn = jnp.concatenate([kpen_ref[...], jnp.zeros((pad, QK_ROPE), F32)], axis=0).astype(BF16)
    sc = _dot_nt(q_lat, kvn) + _dot_nt(q_pe, kpn)
    tok = lax.broadcasted_iota(jnp.int32, (rows, NEW_KEYS_PAD), 0) & (DEC_SEQ - 1)
    key = lax.broadcasted_iota(jnp.int32, (rows, NEW_KEYS_PAD), 1)
    sc = jnp.where(key <= tok, sc, -jnp.inf)
    _, l_fin, acc = _softmax_update(*carry, sc, kvn)
    o_ref[0] = (acc * (1.0 / l_fin)).reshape(N_HEADS, DEC_SEQ, KV_LORA)


def _attention(page_table, qs, ckv_s, kpe_s, cache_ckv, cache_kpet, qp, kt, vb, wuv):
    seq_per_blk = Q_BLOCK // DEC_SEQ
    hps = ATTN_HEADS_PER_STEP
    halves = N_HEADS // hps
    cps = SEQ // KV_CHUNK
    rows_p = hps * Q_BLOCK

    def qblock(s):
        b, i = _prompt_block(s // halves)
        return b * BLOCKS_PER_SEQ + i

    grid_spec = pltpu.PrefetchScalarGridSpec(
        num_scalar_prefetch=1,
        grid=(DEC_BATCH,),
        in_specs=[
            pl.BlockSpec((1, N_HEADS, DEC_SEQ, D_QK), lambda s, pt: (s // seq_per_blk, 0, s % seq_per_blk, 0)),
            pl.BlockSpec((DEC_SEQ, KV_LORA), lambda s, pt: (s, 0)),
            pl.BlockSpec((DEC_SEQ, QK_ROPE), lambda s, pt: (s, 0)),
            pl.BlockSpec(memory_space=pl.ANY),
            pl.BlockSpec(memory_space=pl.ANY),
            pl.BlockSpec((1, hps, Q_BLOCK, D_QK), lambda s, pt: (qblock(s), s % halves, 0, 0)),
            pl.BlockSpec((cps, D_QK, KV_CHUNK), lambda s, pt: (_prompt_block(s // halves)[0], 0, 0)),
            pl.BlockSpec((SEQ, KV_LORA), lambda s, pt: (_prompt_block(s // halves)[0], 0)),
            pl.BlockSpec((hps, KV_LORA, V_HEAD), lambda s, pt: (s % halves, 0, 0)),
        ],
        out_specs=[
            pl.BlockSpec((1, N_HEADS, DEC_SEQ, KV_LORA), lambda s, pt: (s // seq_per_blk, 0, s % seq_per_blk, 0)),
            pl.BlockSpec((Q_BLOCK, hps * V_HEAD), lambda s, pt: (qblock(s), s % halves)),
        ],
        scratch_shapes=[
            pltpu.VMEM((DEC_SLOTS, CHUNK_KEYS + ROPE_ROWS, KV_LORA), F32),
            pltpu.VMEM((CHUNK_KEYS, KV_LORA), BF16),
            pltpu.VMEM((CHUNK_KEYS, KV_LORA), BF16),
            pltpu.VMEM((N_HEADS * DEC_SEQ, CHUNK_KEYS), F32),
            pltpu.VMEM((N_HEADS * DEC_SEQ, CHUNK_KEYS), F32),
            pltpu.SemaphoreType.DMA((DEC_SLOTS,)),
            pltpu.VMEM((rows_p, 1), F32),
            pltpu.VMEM((rows_p, 1), F32),
            pltpu.VMEM((rows_p, KV_LORA), F32),
            pltpu.VMEM((rows_p, KV_CHUNK), F32),
            pltpu.VMEM((rows_p, KV_CHUNK), F32),
        ],
    )
    return pl.pallas_call(
        _attn_kernel,
        grid_spec=grid_spec,
        out_shape=[
            jax.ShapeDtypeStruct((N_SAMPLE // Q_BLOCK, N_HEADS, Q_BLOCK, KV_LORA), F32),
            jax.ShapeDtypeStruct((N_PROMPT, D_MLA_OUT), BF16),
        ],
        compiler_params=_cparams(("arbitrary",)),
        name="attention",
    )(page_table.reshape(-1), qs, ckv_s, kpe_s, cache_ckv, cache_kpet, qp, kt, vb, wuv)


def _uv_proj_kernel(ol_ref, wuv_ref, o_ref, *, nsub):
    for j in range(nsub):
        for h in range(N_HEADS):
            o = _dot(ol_ref[j, h].astype(BF16), wuv_ref[h])
            o_ref[j * Q_BLOCK:(j + 1) * Q_BLOCK, h * V_HEAD:(h + 1) * V_HEAD] = o.astype(o_ref.dtype)


def _uv_proj_sample(o_lat, wuv, nsub=4):
    nb = o_lat.shape[0]
    rows = nsub * Q_BLOCK
    return pl.pallas_call(
        functools.partial(_uv_proj_kernel, nsub=nsub),
        grid=(nb // nsub,),
        in_specs=[
            pl.BlockSpec((nsub, N_HEADS, Q_BLOCK, KV_LORA), lambda i: (i, 0, 0, 0)),
            _resident((N_HEADS, KV_LORA, V_HEAD)),
        ],
        out_specs=pl.BlockSpec((rows, D_MLA_OUT), lambda i: (i, 0)),
        out_shape=jax.ShapeDtypeStruct((nb * Q_BLOCK, D_MLA_OUT), BF16),
        compiler_params=_cparams(("arbitrary",)),
        name="uv_proj_sample",
    )(o_lat, wuv)


def _mm_ln_kernel(*refs, lhs_split, split_resid, split_out, tm):
    is_prompt = pl.program_id(0) < N_PROMPT // tm
    refs = list(refs)

    def take(split):
        if split:
            p_ref, s_ref = refs.pop(0), refs.pop(0)
            return lambda rows: jnp.where(is_prompt, p_ref[rows, :], s_ref[rows, :])
        ref = refs.pop(0)
        return lambda rows: ref[rows, :]

    lhs = [take(split) for split in lhs_split]
    w_ref = refs.pop(0)
    resid = take(split_resid)
    g_ref, b_ref = refs.pop(0), refs.pop(0)
    out_refs = refs

    sub = min(tm, FFN_SUB_ROWS)
    for r in range(0, tm, sub):
        rows = slice(r, r + sub)
        acc = None
        off = 0
        for piece in lhs:
            a = piece(rows)
            part = _dot(a, w_ref[off:off + a.shape[1], :])
            acc = part if acc is None else acc + part
            off += a.shape[1]
        y = _layer_norm(DEEPNORM_ALPHA * resid(rows) + acc, g_ref[...], b_ref[...])
        if split_out:
            @pl.when(is_prompt)
            def _():
                out_refs[0][rows, :] = y

            @pl.when(jnp.logical_not(is_prompt))
            def _():
                out_refs[1][rows, :] = y
        else:
            out_refs[0][rows, :] = y
            out_refs[1][rows, :] = y.astype(BF16)


def _mm_ln(lhs_list, w, resid, g, b, *, tm, split_out=False, layer=None):
    operands, in_specs = [], []

    def add_rows(v):
        if isinstance(v, tuple):
            operands.extend(v)
            in_specs.extend([_prompt_rows(tm, v[0].shape[1]), _sample_rows(tm, v[0].shape[1])])
        else:
            operands.append(v)
            in_specs.append(pl.BlockSpec((tm, v.shape[1]), lambda i: (i, 0)))
        return isinstance(v, tuple)

    lhs_split = tuple(add_rows(v) for v in lhs_list)
    operands.append(w)
    if layer is None:
        in_specs.append(_resident(w.shape))
    else:
        in_specs.append(pl.BlockSpec((None,) + w.shape[1:], lambda i: (layer, 0, 0), pipeline_mode=pl.Buffered(1)))
    split_resid = add_rows(resid)
    operands.extend([g, b])
    in_specs.extend([_resident((1, D_MODEL)), _resident((1, D_MODEL))])
    if split_out:
        out_specs = [_prompt_rows(tm, D_MODEL), _sample_rows(tm, D_MODEL)]
        out_shape = [jax.ShapeDtypeStruct((N_PROMPT, D_MODEL), F32), jax.ShapeDtypeStruct((N_SAMPLE, D_MODEL), F32)]
    else:
        row = pl.BlockSpec((tm, D_MODEL), lambda i: (i, 0))
        out_specs = [row, row]
        out_shape = [jax.ShapeDtypeStruct((N_TOK, D_MODEL), F32), jax.ShapeDtypeStruct((N_TOK, D_MODEL), BF16)]
    return pl.pallas_call(
        functools.partial(_mm_ln_kernel, lhs_split=lhs_split, split_resid=split_resid, split_out=split_out, tm=tm),
        grid=(N_TOK // tm,),
        in_specs=in_specs,
        out_specs=out_specs,
        out_shape=out_shape,
        compiler_params=_cparams(("arbitrary",)),
        name="mm_ln",
    )(*operands)


def _ffn_up_kernel(x_ref, wg_ref, wu_ref, h_ref, wgb_ref, wub_ref):
    @pl.when(pl.program_id(1) == 0)
    def _():
        wgb_ref[...] = wg_ref[...].astype(BF16)
        wub_ref[...] = wu_ref[...].astype(BF16)

    for r in range(0, x_ref.shape[0], FFN_SUB_ROWS):
        x = x_ref[r:r + FFN_SUB_ROWS, :]
        h = jax.nn.silu(_dot(x, wgb_ref[...])) * _dot(x, wub_ref[...])
        h_ref[r:r + FFN_SUB_ROWS, :] = h.astype(h_ref.dtype)


def _ffn_up(xb, wg, wu, layer, tm=N_TOK // 4, tn=512):
    wblk = pl.BlockSpec((None, D_MODEL, tn), lambda j, i: (layer, 0, j))
    return pl.pallas_call(
        _ffn_up_kernel,
        grid=(D_FF // tn, N_TOK // tm),
        in_specs=[pl.BlockSpec((tm, D_MODEL), lambda j, i: (i, 0)), wblk, wblk],
        out_specs=pl.BlockSpec((tm, tn), lambda j, i: (i, j)),
        out_shape=jax.ShapeDtypeStruct((N_TOK, D_FF), BF16),
        scratch_shapes=[pltpu.VMEM((D_MODEL, tn), BF16), pltpu.VMEM((D_MODEL, tn), BF16)],
        compiler_params=_cparams(("arbitrary", "arbitrary")),
        name="ffn_up",
    )(xb, wg, wu)


def _conv_taps(read, cw_ref):
    conv = read(CONV_W - 1) * cw_ref[0:1, :]
    for j in range(1, CONV_W):
        conv = conv + read(CONV_W - 1 - j) * cw_ref[j:j + 1, :]
    return conv


def _in_odd_cast_weights(wv_ref, wb_ref, wc_ref, wvb_ref, wbb_ref, wcb_ref):
    @pl.when(pl.program_id(1) == 0)
    def _():
        wvb_ref[...] = wv_ref[...].astype(BF16)
        wbb_ref[...] = wb_ref[...].astype(BF16)
        wcb_ref[...] = wc_ref[...].astype(BF16)


def _in_odd_prompt_kernel(x_ref, wv_ref, wb_ref, wc_ref, cw_ref, g_ref, ztail_ref,
                          wvb_ref, wbb_ref, wcb_ref, zp_ref, *, tm):
    i = pl.program_id(1)
    tiles_per_seq = SEQ // tm
    _in_odd_cast_weights(wv_ref, wb_ref, wc_ref, wvb_ref, wbb_ref, wcb_ref)

    @pl.when(i % tiles_per_seq == 0)
    def _():
        zp_ref[0:CONV_HIST, :] = jnp.zeros((CONV_HIST, zp_ref.shape[1]), F32)

    for r in range(0, tm, FFN_SUB_ROWS):
        x = x_ref[r:r + FFN_SUB_ROWS, :]
        new = slice(CONV_HIST + r, CONV_HIST + r + FFN_SUB_ROWS)
        zp_ref[new, :] = _dot(x, wcb_ref[...]) * _dot(x, wvb_ref[...])
        conv = _conv_taps(lambda back: zp_ref[new.start - back:new.stop - back, :], cw_ref)
        g_ref[r:r + FFN_SUB_ROWS, :] = (_dot(x, wbb_ref[...]) * conv).astype(g_ref.dtype)
    zp_ref[0:CONV_HIST, :] = zp_ref[tm:tm + CONV_HIST, :]

    @pl.when(i % tiles_per_seq == tiles_per_seq - 1)
    def _():
        ztail_ref[...] = zp_ref[tm:tm + CONV_HIST, :]


def _in_odd_sample_kernel(x_ref, wv_ref, wb_ref, wc_ref, zs_ref, gbs_ref, wvb_ref, wbb_ref, wcb_ref):
    _in_odd_cast_weights(wv_ref, wb_ref, wc_ref, wvb_ref, wbb_ref, wcb_ref)
    x = x_ref[...]
    zs_ref[...] = _dot(x, wcb_ref[...]) * _dot(x, wvb_ref[...])
    gbs_ref[...] = _dot(x, wbb_ref[...])


def _in_odd(xb, w_in, conv_w, tm=1024, tn=512):
    nb = D_MODEL // tn
    nbp = N_PROMPT // tm
    tiles_per_seq = SEQ // tm
    wblks = [pl.BlockSpec((D_MODEL, tn), lambda j, i, part=part: (0, part * nb + j)) for part in range(3)]
    wscratch = [pltpu.VMEM((D_MODEL, tn), BF16)] * 3
    g_prompt, z_tail = pl.pallas_call(
        functools.partial(_in_odd_prompt_kernel, tm=tm),
        grid=(nb, nbp),
        in_specs=[pl.BlockSpec((tm, D_MODEL), lambda j, i: (i, 0))] + wblks
        + [pl.BlockSpec((CONV_W, tn), lambda j, i: (0, j))],
        out_specs=[
            pl.BlockSpec((tm, tn), lambda j, i: (i, j)),
            pl.BlockSpec((CONV_HIST, tn), lambda j, i: (i // tiles_per_seq, j)),
        ],
        out_shape=[
            jax.ShapeDtypeStruct((N_PROMPT, D_MODEL), BF16),
            jax.ShapeDtypeStruct((BATCH * CONV_HIST, D_MODEL), F32),
        ],
        scratch_shapes=wscratch + [pltpu.VMEM((tm + CONV_HIST, tn), F32)],
        compiler_params=_cparams(("arbitrary", "arbitrary")),
        name="in_odd_prompt",
    )(xb, w_in, w_in, w_in, conv_w)
    sample_blk = pl.BlockSpec((tm, tn), lambda j, i: (i, j))
    z_s, gb_s = pl.pallas_call(
        _in_odd_sample_kernel,
        grid=(nb, N_SAMPLE // tm),
        in_specs=[pl.BlockSpec((tm, D_MODEL), lambda j, i: (nbp + i, 0))] + wblks,
        out_specs=[sample_blk, sample_blk],
        out_shape=[jax.ShapeDtypeStruct((N_SAMPLE, D_MODEL), F32)] * 2,
        scratch_shapes=wscratch,
        compiler_params=_cparams(("arbitrary", "arbitrary")),
        name="in_odd_sample",
    )(xb, w_in, w_in, w_in)
    return g_prompt, z_tail, z_s, gb_s


def _conv_sample_kernel(z_ref, hist_ref, gb_ref, cw_ref, g_ref, zp_ref, *, S):
    zp_ref[:, 0:CONV_HIST, :] = hist_ref[...]
    zp_ref[:, CONV_HIST:, :] = z_ref[...]
    conv = _conv_taps(lambda back: zp_ref[:, CONV_HIST - back:CONV_HIST - back + DEC_SEQ, :], cw_ref)
    g_ref[...] = (gb_ref[...] * conv).reshape(S * DEC_SEQ, D_MODEL).astype(g_ref.dtype)


def _conv_sample(z3, hist, gb3, conv_w, S=32):
    blk = lambda i: (i, 0, 0)
    rows = S * DEC_SEQ
    return pl.pallas_call(
        functools.partial(_conv_sample_kernel, S=S),
        grid=(DEC_BATCH // S,),
        in_specs=[
            pl.BlockSpec((S, DEC_SEQ, D_MODEL), blk),
            pl.BlockSpec((S, CONV_HIST, D_MODEL), blk),
            pl.BlockSpec((S, DEC_SEQ, D_MODEL), blk),
            _resident((CONV_W, D_MODEL)),
        ],
        out_specs=pl.BlockSpec((rows, D_MODEL), lambda i: (i, 0)),
        out_shape=jax.ShapeDtypeStruct((N_SAMPLE, D_MODEL), BF16),
        scratch_shapes=[pltpu.VMEM((S, CONV_HIST + DEC_SEQ, D_MODEL), F32)],
        compiler_params=_cparams(("arbitrary",)),
        name="conv_sample",
    )(z3, hist, gb3, conv_w)


def _rope_tables():
    pos = jnp.concatenate([jnp.tile(jnp.arange(SEQ), BATCH),
                           jnp.tile(PAST_LEN + jnp.arange(DEC_SEQ), DEC_BATCH)])
    inv = ROPE_THETA ** (-jnp.arange(0, QK_ROPE, 2, dtype=F32) / QK_ROPE)
    ang = pos.astype(F32)[:, None] * inv[None, :]
    cos, sin = jnp.cos(ang), jnp.sin(ang)
    cos2 = jnp.concatenate([cos, cos], axis=1)
    sin2 = jnp.concatenate([-sin, sin], axis=1)
    return jnp.tile(cos2, (1, 2)), jnp.tile(sin2, (1, 2)), jnp.concatenate([cos2, sin2], axis=1)


def _swap_halves(w):
    half = QK_ROPE // 2
    return jnp.concatenate([w[..., half:], w[..., :half]], axis=-1)


def kernel(x_prompt, x_sample, cache_ckv, cache_kpe, state_pool, state_conv, page_table, w_in_even, pool_w, pool_scale, q_norm_g, w_q_b, kv_norm_g, w_uk, w_uv, w_out_even, w_in_odd, conv_w, w_out_odd, ffn_w_gate, ffn_w_up, ffn_w_down, ln1_g, ln1_b, ln2_g, ln2_b):
    assert DEPTH == 2
    n_pool_pages = cache_ckv.shape[1]
    cos, sin, cos_sin = _rope_tables()
    xp = x_prompt.reshape(N_PROMPT, D_MODEL)
    xs = x_sample.reshape(N_SAMPLE, D_MODEL)

    def row(v):
        return v.reshape(1, -1)

    w_down = ffn_w_down.astype(BF16)

    def ffn(layer, xf, xb, last):
        hmid = _ffn_up(xb, ffn_w_gate, ffn_w_up, layer)
        return _mm_ln([hmid], w_down, xf, row(ln2_g[layer]), row(ln2_b[layer]),
                      tm=256, split_out=last, layer=layer)

    e = 0
    w_in = w_in_even[e]
    w_all = jnp.concatenate([w_in, _swap_halves(w_in[:, D_POOL + Q_LORA + KV_LORA:])], axis=1).astype(BF16)
    u, cq, ckv_p, ckv_s, kpe_p, kpe_s, vb, kt = _in_even(
        xp, xs, w_all, row(q_norm_g[e]), row(kv_norm_g[e]), cos_sin)

    wq = w_q_b[e]
    wq_rope = wq[:, :, QK_NOPE:]
    qp, qs = _q_proj(
        cq,
        wq[:, :, :QK_NOPE].reshape(Q_LORA, N_HEADS * QK_NOPE).astype(BF16),
        wq_rope.reshape(Q_LORA, N_HEADS * QK_ROPE).astype(BF16),
        _swap_halves(wq_rope).reshape(Q_LORA, N_HEADS * QK_ROPE).astype(BF16),
        jnp.transpose(w_uk[e], (1, 2, 0)).astype(BF16), cos, sin)

    pw = pool_w[e].astype(BF16)
    ps = row(pool_scale[e])
    u_sample = u[N_PROMPT:].reshape(DEC_BATCH, DEC_SEQ, D_POOL)
    pool_hist = jnp.pad(state_pool[e], ((0, 0), (POOL_HIST - POOL_BUF, 0), (0, 0)))
    y_pool = (_pool_prompt(u, pw, ps), _pool_sample(u_sample, pool_hist, pw, ps))

    wuv = jnp.transpose(w_uv[e], (1, 0, 2)).astype(BF16)
    o_lat_s, o_prompt = _attention(page_table, qs, ckv_s, kpe_s,
                                   cache_ckv[e].reshape(n_pool_pages, PAGE_SIZE, KV_LORA),
                                   jnp.swapaxes(cache_kpe[e], 1, 2), qp, kt, vb, wuv)
    o_sample = _uv_proj_sample(o_lat_s, wuv)
    xf, xb = _mm_ln([y_pool, (o_prompt, o_sample)], w_out_even[e].astype(BF16), (xp, xs),
                    row(ln1_g[0]), row(ln1_b[0]), tm=512)
    xf, xb = ffn(0, xf, xb, False)

    od = 0
    g_prompt, z_tail, z_s, gb_s = _in_odd(xb, w_in_odd[od], conv_w[od])
    z_sample = z_s.reshape(DEC_BATCH, DEC_SEQ, D_MODEL)
    gb_sample = gb_s.reshape(DEC_BATCH, DEC_SEQ, D_MODEL)
    conv_hist = jnp.pad(state_conv[od], ((0, 0), (CONV_HIST - CONV_BUF, 0), (0, 0)))
    g = (g_prompt, _conv_sample(z_sample, conv_hist, gb_sample, conv_w[od]))
    xf, xb = _mm_ln([g], w_out_odd[od].astype(BF16), xf, row(ln1_g[1]), row(ln1_b[1]), tm=512)
    y_p, y_s = ffn(1, xf, xb, True)

    pool_prompt = jnp.stack([u[(b + 1) * SEQ - POOL_BUF:(b + 1) * SEQ] for b in range(BATCH)])
    pool_sample = jnp.concatenate([state_pool[e][:, DEC_SEQ:], u_sample], axis=1)
    conv_prompt = z_tail.reshape(BATCH, CONV_HIST, D_MODEL)[:, CONV_HIST - CONV_BUF:]
    conv_sample = z_sample[:, DEC_SEQ - CONV_BUF:]
    return (y_p.reshape(BATCH, SEQ, D_MODEL), y_s.reshape(DEC_BATCH, DEC_SEQ, D_MODEL),
            ckv_p.reshape(1, BATCH, SEQ, KV_LORA), kpe_p.reshape(1, BATCH, SEQ, QK_ROPE),
            pool_prompt[None], conv_prompt[None],
            ckv_s.reshape(1, DEC_BATCH, DEC_SEQ, KV_LORA), kpe_s.reshape(1, DEC_BATCH, DEC_SEQ, QK_ROPE),
            pool_sample[None], conv_sample[None])
```

```python
import functools

import jax
import jax.numpy as jnp
from jax import lax
from jax.experimental import pallas as pl
from jax.experimental.pallas import tpu as pltpu

F32 = jnp.float32
BF16 = jnp.bfloat16

D_MODEL = 2048
BATCH = 4
SEQ = 2048
DEPTH = 2
DEC_BATCH = 128
DEC_SEQ = 8
PAST_LEN = 16384
PAGE_SIZE = 128
N_PAGES = PAST_LEN // PAGE_SIZE
POOL_WINDOWS = (2, 4, 8, 16)
N_POOL_GROUPS = len(POOL_WINDOWS)
D_POOL = D_MODEL // 2
POOL_GROUP = D_POOL // N_POOL_GROUPS
POOL_BUF = max(POOL_WINDOWS) - 1
N_HEADS = 8
QK_NOPE = 128
QK_ROPE = 64
V_HEAD = 128
Q_LORA = 512
KV_LORA = 512
ROPE_THETA = 10000.0
ATTN_SCALE = (QK_NOPE + QK_ROPE) ** -0.5
Q_BLOCK = 128
D_MLA_OUT = N_HEADS * V_HEAD
CONV_W = 3
CONV_BUF = CONV_W - 1
D_FF = ((8 * D_MODEL + 3 * 256 - 1) // (3 * 256)) * 256
DEEPNORM_ALPHA = (2 * DEPTH) ** 0.25

N_PROMPT = BATCH * SEQ
N_SAMPLE = DEC_BATCH * DEC_SEQ
N_TOK = N_PROMPT + N_SAMPLE
D_QK = KV_LORA + QK_ROPE
POOL_HIST = 16
CONV_HIST = 8
KV_CHUNK = 512
FFN_SUB_ROWS = 256
LN_SUB_ROWS = 128
VMEM_LIMIT_MB = 56


def _cparams(dims, vmem_mb=VMEM_LIMIT_MB):
    return pltpu.CompilerParams(dimension_semantics=dims, vmem_limit_bytes=vmem_mb * 1024 * 1024)


def _resident(shape):
    zeros = (0,) * len(shape)
    return pl.BlockSpec(shape, lambda *_: zeros, pipeline_mode=pl.Buffered(1))


def _prompt_rows(tm, width):
    last = N_PROMPT // tm - 1
    return pl.BlockSpec((tm, width), lambda i, *_: (jnp.minimum(i, last), 0))


def _sample_rows(tm, width):
    first = N_PROMPT // tm
    return pl.BlockSpec((tm, width), lambda i, *_: (jnp.maximum(i - first, 0), 0))


def _dot(a, b):
    return jnp.dot(a, b, preferred_element_type=F32)


def _dot_nt(a, b):
    return lax.dot_general(a, b, (((1,), (1,)), ((), ())), preferred_element_type=F32)


def _rms(h, g, eps=1e-6):
    return h * lax.rsqrt(jnp.mean(h * h, axis=-1, keepdims=True) + eps) * g


def _layer_norm(v, g, b, eps=1e-5):
    mu = jnp.mean(v, axis=-1, keepdims=True)
    vc = v - mu
    var = jnp.mean(vc * vc, axis=-1, keepdims=True)
    return vc * lax.rsqrt(var + eps) * g + b


def _softmax_update(m_old, l_old, acc, s, v):
    m_new = jnp.maximum(m_old, jnp.max(s, axis=-1, keepdims=True))
    alpha = jnp.exp(m_old - m_new)
    p = jnp.exp(s - m_new)
    l_new = alpha * l_old + jnp.sum(p, axis=-1, keepdims=True)
    return m_new, l_new, alpha * acc + _dot(p.astype(BF16), v)


D_IN_EVEN = D_POOL + Q_LORA + KV_LORA + 2 * QK_ROPE


def _in_even_kernel(xp_ref, xs_ref, w_ref, qg_ref, kvg_ref, cs_ref,
                    u_ref, cq_ref, ckvp_ref, ckvs_ref, kpep_ref, kpes_ref, vb_ref, kt_ref, *, tm):
    i = pl.program_id(0)
    is_prompt = i < N_PROMPT // tm
    xb = jnp.where(is_prompt, xp_ref[...], xs_ref[...]).astype(BF16)
    h = _dot(xb, w_ref[...])
    o1, o2 = D_POOL + Q_LORA, D_POOL + Q_LORA + KV_LORA
    u_ref[...] = h[:, :D_POOL]
    cq_ref[...] = _rms(h[:, D_POOL:o1], qg_ref[...]).astype(BF16)
    ckv = _rms(h[:, o1:o2], kvg_ref[...])
    prod = h[:, o2:] * cs_ref[...]
    kpe2 = prod + pltpu.roll(prod, QK_ROPE, 1)
    kpe = kpe2[:, :QK_ROPE]

    @pl.when(is_prompt)
    def _():
        ckvp_ref[...] = ckv
        kpep_ref[...] = kpe
        vb_ref[...] = ckv.astype(BF16)
        for c in range(tm // KV_CHUNK):
            rows = slice(c * KV_CHUNK, (c + 1) * KV_CHUNK)
            kt_ref[c, :KV_LORA, :] = ckv[rows, :].T.astype(BF16)
            kt_ref[c, KV_LORA:, :] = kpe2[rows, :].T[:QK_ROPE, :].astype(BF16)

    @pl.when(jnp.logical_not(is_prompt))
    def _():
        ckvs_ref[...] = ckv
        kpes_ref[...] = kpe


def _in_even(xp, xs, w_all, qg, kvg, cs, tm=512):
    row = lambda i: (i, 0)
    cpb = tm // KV_CHUNK
    last = N_PROMPT // tm - 1
    return pl.pallas_call(
        functools.partial(_in_even_kernel, tm=tm),
        grid=(N_TOK // tm,),
        in_specs=[
            _prompt_rows(tm, D_MODEL),
            _sample_rows(tm, D_MODEL),
            _resident((D_MODEL, D_IN_EVEN)),
            _resident((1, Q_LORA)),
            _resident((1, KV_LORA)),
            pl.BlockSpec((tm, 2 * QK_ROPE), row),
        ],
        out_specs=[
            pl.BlockSpec((tm, D_POOL), row),
            pl.BlockSpec((tm, Q_LORA), row),
            _prompt_rows(tm, KV_LORA),
            _sample_rows(tm, KV_LORA),
            _prompt_rows(tm, QK_ROPE),
            _sample_rows(tm, QK_ROPE),
            _prompt_rows(tm, KV_LORA),
            pl.BlockSpec((cpb, D_QK, KV_CHUNK), lambda i: (jnp.minimum(i, last), 0, 0)),
        ],
        out_shape=[
            jax.ShapeDtypeStruct((N_TOK, D_POOL), F32),
            jax.ShapeDtypeStruct((N_TOK, Q_LORA), BF16),
            jax.ShapeDtypeStruct((N_PROMPT, KV_LORA), F32),
            jax.ShapeDtypeStruct((N_SAMPLE, KV_LORA), F32),
            jax.ShapeDtypeStruct((N_PROMPT, QK_ROPE), F32),
            jax.ShapeDtypeStruct((N_SAMPLE, QK_ROPE), F32),
            jax.ShapeDtypeStruct((N_PROMPT, KV_LORA), BF16),
            jax.ShapeDtypeStruct((N_PROMPT // KV_CHUNK, D_QK, KV_CHUNK), BF16),
        ],
        compiler_params=_cparams(("arbitrary",)),
        name="in_even",
    )(xp, xs, w_all, qg, kvg, cs)


def _q_proj_kernel(cq_ref, wqn_ref, wqr_ref, wqrr_ref, wukt_ref, cos_ref, sin_ref, qp_ref, qs_ref, *, nsub):
    i = pl.program_id(0)
    is_prompt = i < N_PROMPT // (nsub * Q_BLOCK)
    cq = cq_ref[...]
    qn = _dot(cq, wqn_ref[...])
    cos = jnp.concatenate([cos_ref[...]] * (N_HEADS // 2), axis=1)
    sin = jnp.concatenate([sin_ref[...]] * (N_HEADS // 2), axis=1)
    qpe = (_dot(cq, wqr_ref[...]) * cos + _dot(cq, wqrr_ref[...]) * sin) * ATTN_SCALE

    def emit(q_ref):
        for h in range(N_HEADS):
            ql = _dot(qn[:, h * QK_NOPE:(h + 1) * QK_NOPE].astype(BF16), wukt_ref[h]) * ATTN_SCALE
            for j in range(nsub):
                rows = slice(j * Q_BLOCK, (j + 1) * Q_BLOCK)
                q_ref[j, h, :, :KV_LORA] = ql[rows].astype(q_ref.dtype)
                q_ref[j, h, :, KV_LORA:] = qpe[rows, h * QK_ROPE:(h + 1) * QK_ROPE].astype(q_ref.dtype)

    @pl.when(is_prompt)
    def _():
        emit(qp_ref)

    @pl.when(jnp.logical_not(is_prompt))
    def _():
        emit(qs_ref)


def _q_proj(cq, wqn, wqr, wqrr, wukt, cos, sin, tm=512):
    nsub = tm // Q_BLOCK
    nbp = N_PROMPT // tm
    row = lambda i: (i, 0)
    blk = (nsub, N_HEADS, Q_BLOCK, D_QK)
    return pl.pallas_call(
        functools.partial(_q_proj_kernel, nsub=nsub),
        grid=(N_TOK // tm,),
        in_specs=[
            pl.BlockSpec((tm, Q_LORA), row),
            _resident((Q_LORA, N_HEADS * QK_NOPE)),
            _resident((Q_LORA, N_HEADS * QK_ROPE)),
            _resident((Q_LORA, N_HEADS * QK_ROPE)),
            _resident((N_HEADS, QK_NOPE, KV_LORA)),
            pl.BlockSpec((tm, 2 * QK_ROPE), row),
            pl.BlockSpec((tm, 2 * QK_ROPE), row),
        ],
        out_specs=[
            pl.BlockSpec(blk, lambda i: (jnp.minimum(i, nbp - 1), 0, 0, 0)),
            pl.BlockSpec(blk, lambda i: (jnp.maximum(i - nbp, 0), 0, 0, 0)),
        ],
        out_shape=[
            jax.ShapeDtypeStruct((N_PROMPT // Q_BLOCK, N_HEADS, Q_BLOCK, D_QK), BF16),
            jax.ShapeDtypeStruct((N_SAMPLE // Q_BLOCK, N_HEADS, Q_BLOCK, D_QK), F32),
        ],
        compiler_params=_cparams(("arbitrary",)),
        name="q_proj",
    )(cq, wqn, wqr, wqrr, wukt, cos, sin)


def _pool_groups(read, pos, w_ref, scale_ref, write):
    for g, w in enumerate(POOL_WINDOWS):
        lanes = slice(g * POOL_GROUP, (g + 1) * POOL_GROUP)
        cur = read(0, lanes)
        win = cur
        for j in range(1, w):
            win = win + read(j, lanes)
        cnt = jnp.minimum(pos + 1, w).astype(F32)
        diff = (win / cnt - cur).astype(BF16)
        write(lanes, _dot(diff, w_ref[g]) * scale_ref[:, lanes])


def _pool_prompt_kernel(u_ref, w_ref, scale_ref, y_ref, xp_ref, *, T):
    t = pl.program_id(1)

    @pl.when(t == 0)
    def _():
        xp_ref[0:POOL_HIST, :] = jnp.zeros((POOL_HIST, D_POOL), F32)

    xp_ref[POOL_HIST:POOL_HIST + T, :] = u_ref[...]
    pos = t * T + lax.broadcasted_iota(jnp.int32, (T, 1), 0)

    def read(j, lanes):
        return xp_ref[POOL_HIST - j:POOL_HIST - j + T, lanes]

    def write(lanes, y):
        y_ref[:, lanes] = y.astype(y_ref.dtype)

    _pool_groups(read, pos, w_ref, scale_ref, write)
    xp_ref[0:POOL_HIST, :] = xp_ref[T:T + POOL_HIST, :]


def _pool_prompt(u, pool_w, pool_scale, T=512):
    nt = SEQ // T
    return pl.pallas_call(
        functools.partial(_pool_prompt_kernel, T=T),
        grid=(BATCH, nt),
        in_specs=[
            pl.BlockSpec((T, D_POOL), lambda b, t: (b * nt + t, 0)),
            _resident((N_POOL_GROUPS, POOL_GROUP, POOL_GROUP)),
            _resident((1, D_POOL)),
        ],
        out_specs=pl.BlockSpec((T, D_POOL), lambda b, t: (b * nt + t, 0)),
        out_shape=jax.ShapeDtypeStruct((N_PROMPT, D_POOL), BF16),
        scratch_shapes=[pltpu.VMEM((T + POOL_HIST, D_POOL), F32)],
        compiler_params=_cparams(("arbitrary", "arbitrary")),
        name="pool_prompt",
    )(u, pool_w, pool_scale)


def _pool_sample_kernel(u_ref, hist_ref, w_ref, scale_ref, y_ref, xp_ref, *, S):
    xp_ref[:, 0:POOL_HIST, :] = hist_ref[...]
    xp_ref[:, POOL_HIST:, :] = u_ref[...]
    pos = PAST_LEN + lax.broadcasted_iota(jnp.int32, (S, DEC_SEQ, 1), 1)
    pos = pos.reshape(S * DEC_SEQ, 1)

    def read(j, lanes):
        return xp_ref[:, POOL_HIST - j:POOL_HIST - j + DEC_SEQ, lanes].reshape(S * DEC_SEQ, POOL_GROUP)

    def write(lanes, y):
        y_ref[:, lanes] = y.astype(y_ref.dtype)

    _pool_groups(read, pos, w_ref, scale_ref, write)


def _pool_sample(u3, hist, pool_w, pool_scale, S=32):
    rows = S * DEC_SEQ
    return pl.pallas_call(
        functools.partial(_pool_sample_kernel, S=S),
        grid=(DEC_BATCH // S,),
        in_specs=[
            pl.BlockSpec((S, DEC_SEQ, D_POOL), lambda i: (i, 0, 0)),
            pl.BlockSpec((S, POOL_HIST, D_POOL), lambda i: (i, 0, 0)),
            _resident((N_POOL_GROUPS, POOL_GROUP, POOL_GROUP)),
            _resident((1, D_POOL)),
        ],
        out_specs=pl.BlockSpec((rows, D_POOL), lambda i: (i, 0)),
        out_shape=jax.ShapeDtypeStruct((N_SAMPLE, D_POOL), BF16),
        scratch_shapes=[pltpu.VMEM((S, POOL_HIST + DEC_SEQ, D_POOL), F32)],
        compiler_params=_cparams(("arbitrary",)),
        name="pool_sample",
    )(u3, hist, pool_w, pool_scale)


def _prompt_attn_block(i, q_ref, kt_ref, v_ref, wuv_ref, o_ref, m_ref, l_ref, acc_ref, sa_ref, sb_ref, heads):
    rows_all = heads * Q_BLOCK
    m_ref[...] = jnp.full((rows_all, 1), -jnp.inf, F32)
    l_ref[...] = jnp.zeros((rows_all, 1), F32)
    acc_ref[...] = jnp.zeros((rows_all, KV_LORA), F32)
    q = q_ref[0].reshape(rows_all, D_QK)
    q_pos = i * Q_BLOCK + (lax.broadcasted_iota(jnp.int32, (rows_all, KV_CHUNK), 0) & (Q_BLOCK - 1))
    key = lax.broadcasted_iota(jnp.int32, (rows_all, KV_CHUNK), 1)

    def scores(j, s_ref):
        s = _dot(q, kt_ref[j])
        s_ref[...] = jnp.where(j * KV_CHUNK + key <= q_pos, s, -jnp.inf)

    def attend(j, s_ref):
        v = v_ref[pl.ds(pl.multiple_of(j * KV_CHUNK, KV_CHUNK), KV_CHUNK), :]
        m_new, l_new, acc_new = _softmax_update(m_ref[...], l_ref[...], acc_ref[...], s_ref[...], v)
        m_ref[...] = m_new
        l_ref[...] = l_new
        acc_ref[...] = acc_new

    n_chunks = (i * Q_BLOCK + Q_BLOCK + KV_CHUNK - 1) // KV_CHUNK
    n_pairs = (n_chunks - 1) // 2
    scores(0, sa_ref)

    def body(t, c):
        scores(2 * t + 1, sb_ref)
        attend(2 * t, sa_ref)
        scores(2 * t + 2, sa_ref)
        attend(2 * t + 1, sb_ref)
        return c

    lax.fori_loop(0, n_pairs, body, 0)
    done = 2 * n_pairs

    @pl.when(done + 2 == n_chunks)
    def _():
        scores(done + 1, sb_ref)
        attend(done, sa_ref)
        attend(done + 1, sb_ref)

    @pl.when(done + 1 == n_chunks)
    def _():
        attend(done, sa_ref)

    for h in range(heads):
        rows = slice(h * Q_BLOCK, (h + 1) * Q_BLOCK)
        o_lat = (acc_ref[rows] * (1.0 / l_ref[rows])).astype(BF16)
        o_ref[:, h * V_HEAD:(h + 1) * V_HEAD] = _dot(o_lat, wuv_ref[h]).astype(o_ref.dtype)


PAGES_PER_CHUNK = 16
CHUNK_KEYS = PAGES_PER_CHUNK * PAGE_SIZE
CHUNKS_PER_SEQ = N_PAGES // PAGES_PER_CHUNK
PAGES_PER_GROUP = KV_LORA // PAGE_SIZE
GROUP_KEYS = PAGES_PER_GROUP * PAGE_SIZE
ROPE_ROWS = CHUNK_KEYS // GROUP_KEYS * QK_ROPE
DEC_SLOTS = 4
DEC_AHEAD = DEC_SLOTS - 1
NEW_KEYS_PAD = 16
ATTN_HEADS_PER_STEP = N_HEADS // 2
BLOCKS_PER_SEQ = SEQ // Q_BLOCK
assert CHUNKS_PER_SEQ % DEC_SLOTS == 0
assert DEC_BATCH * ATTN_HEADS_PER_STEP == BATCH * BLOCKS_PER_SEQ * N_HEADS


def _prompt_block(k):
    b, r = k // BLOCKS_PER_SEQ, k % BLOCKS_PER_SEQ
    return b, jnp.where(r % 2 == 0, r // 2, BLOCKS_PER_SEQ - 1 - r // 2)


def _attn_kernel(pt_ref, q_ref, ckvn_ref, kpen_ref, cckv_hbm, ckpet_hbm, qp_ref, kt_ref, v_ref, wuv_ref,
                 o_ref, op_ref, kv_buf, kvb_a, kvb_b, sc_a, sc_b, sem,
                 m_ref, l_ref, acc_ref, sa_ref, sb_ref):
    s = pl.program_id(0)
    nseq = pl.num_programs(0)
    rows = N_HEADS * DEC_SEQ

    def rope_rows(group):
        return pl.ds(CHUNK_KEYS + group * QK_ROPE, QK_ROPE)

    def start_chunk(seq, c):
        slot = c % DEC_SLOTS
        for p in range(PAGES_PER_CHUNK):
            page = pt_ref[(seq * CHUNKS_PER_SEQ + c) * PAGES_PER_CHUNK + p]
            lanes = pl.ds(p % PAGES_PER_GROUP * PAGE_SIZE, PAGE_SIZE)
            pltpu.make_async_copy(cckv_hbm.at[page], kv_buf.at[slot, pl.ds(p * PAGE_SIZE, PAGE_SIZE), :],
                                  sem.at[slot]).start(priority=1)
            pltpu.make_async_copy(ckpet_hbm.at[page], kv_buf.at[slot, rope_rows(p // PAGES_PER_GROUP), lanes],
                                  sem.at[slot]).start(priority=0)

    def wait_chunk(c):
        slot = c % DEC_SLOTS
        pltpu.make_async_copy(kv_buf.at[slot], kv_buf.at[slot], sem.at[slot]).wait()

    @pl.when(s == 0)
    def _():
        for c in range(DEC_AHEAD):
            start_chunk(0, c)

    q = q_ref[0].reshape(rows, D_QK)
    q_lat = q[:, :KV_LORA].astype(BF16)
    q_pe = q[:, KV_LORA:].astype(BF16)

    def scores(c, kvb_ref, sc_ref):
        nxt = c + DEC_AHEAD
        if nxt < CHUNKS_PER_SEQ:
            start_chunk(s, nxt)
        else:
            @pl.when(s + 1 < nseq)
            def _():
                start_chunk(s + 1, nxt - CHUNKS_PER_SEQ)
        wait_chunk(c)
        slot = c % DEC_SLOTS
        kvb_ref[...] = kv_buf[slot, :CHUNK_KEYS, :].astype(BF16)
        lat = _dot_nt(q_lat, kvb_ref[...])
        for g in range(CHUNK_KEYS // GROUP_KEYS):
            keys = slice(g * GROUP_KEYS, (g + 1) * GROUP_KEYS)
            sc_ref[:, keys] = lat[:, keys] + _dot(q_pe, kv_buf[slot, rope_rows(g), :].astype(BF16))

    def attend(carry, kvb_ref, sc_ref):
        return _softmax_update(*carry, sc_ref[...], kvb_ref[...])

    bufs = ((kvb_a, sc_a), (kvb_b, sc_b))
    carry = (jnp.full((rows, 1), -jnp.inf, F32), jnp.zeros((rows, 1), F32), jnp.zeros((rows, KV_LORA), F32))
    scores(0, *bufs[0])
    for c in range(1, CHUNKS_PER_SEQ):
        scores(c, *bufs[c % 2])
        carry = attend(carry, *bufs[(c - 1) % 2])
        if c == CHUNKS_PER_SEQ // 2:
            _, blk = _prompt_block(s // 2)
            _prompt_attn_block(blk, qp_ref, kt_ref, v_ref, wuv_ref, op_ref, m_ref, l_ref, acc_ref,
                               sa_ref, sb_ref, ATTN_HEADS_PER_STEP)
    carry = attend(carry, *bufs[(CHUNKS_PER_SEQ - 1) % 2])

    pad = NEW_KEYS_PAD - DEC_SEQ
    kvn = jnp.concatenate([ckvn_ref[...], jnp.zeros((pad, KV_LORA), F32)], axis=0).astype(BF16)
    kpn = jnp.concatenate([kpen_ref[...], jnp.zeros((pad, QK_ROPE), F32)], axis=0).astype(BF16)
    sc = _dot_nt(q_lat, kvn) + _dot_nt(q_pe, kpn)
    tok = lax.broadcasted_iota(jnp.int32, (rows, NEW_KEYS_PAD), 0) & (DEC_SEQ - 1)
    key = lax.broadcasted_iota(jnp.int32, (rows, NEW_KEYS_PAD), 1)
    sc = jnp.where(key <= tok, sc, -jnp.inf)
    _, l_fin, acc = _softmax_update(*carry, sc, kvn)
    o_ref[0] = (acc * (1.0 / l_fin)).reshape(N_HEADS, DEC_SEQ, KV_LORA)


def _attention(page_table, qs, ckv_s, kpe_s, cache_ckv, cache_kpet, qp, kt, vb, wuv):
    seq_per_blk = Q_BLOCK // DEC_SEQ
    hps = ATTN_HEADS_PER_STEP
    halves = N_HEADS // hps
    cps = SEQ // KV_CHUNK
    rows_p = hps * Q_BLOCK

    def qblock(s):
        b, i = _prompt_block(s // halves)
        return b * BLOCKS_PER_SEQ + i

    grid_spec = pltpu.PrefetchScalarGridSpec(
        num_scalar_prefetch=1,
        grid=(DEC_BATCH,),
        in_specs=[
            pl.BlockSpec((1, N_HEADS, DEC_SEQ, D_QK), lambda s, pt: (s // seq_per_blk, 0, s % seq_per_blk, 0)),
            pl.BlockSpec((DEC_SEQ, KV_LORA), lambda s, pt: (s, 0)),
            pl.BlockSpec((DEC_SEQ, QK_ROPE), lambda s, pt: (s, 0)),
            pl.BlockSpec(memory_space=pl.ANY),
            pl.BlockSpec(memory_space=pl.ANY),
            pl.BlockSpec((1, hps, Q_BLOCK, D_QK), lambda s, pt: (qblock(s), s % halves, 0, 0)),
            pl.BlockSpec((cps, D_QK, KV_CHUNK), lambda s, pt: (_prompt_block(s // halves)[0], 0, 0)),
            pl.BlockSpec((SEQ, KV_LORA), lambda s, pt: (_prompt_block(s // halves)[0], 0)),
            pl.BlockSpec((hps, KV_LORA, V_HEAD), lambda s, pt: (s % halves, 0, 0)),
        ],
        out_specs=[
            pl.BlockSpec((1, N_HEADS, DEC_SEQ, KV_LORA), lambda s, pt: (s // seq_per_blk, 0, s % seq_per_blk, 0)),
            pl.BlockSpec((Q_BLOCK, hps * V_HEAD), lambda s, pt: (qblock(s), s % halves)),
        ],
        scratch_shapes=[
            pltpu.VMEM((DEC_SLOTS, CHUNK_KEYS + ROPE_ROWS, KV_LORA), F32),
            pltpu.VMEM((CHUNK_KEYS, KV_LORA), BF16),
            pltpu.VMEM((CHUNK_KEYS, KV_LORA), BF16),
            pltpu.VMEM((N_HEADS * DEC_SEQ, CHUNK_KEYS), F32),
            pltpu.VMEM((N_HEADS * DEC_SEQ, CHUNK_KEYS), F32),
            pltpu.SemaphoreType.DMA((DEC_SLOTS,)),
            pltpu.VMEM((rows_p, 1), F32),
            pltpu.VMEM((rows_p, 1), F32),
            pltpu.VMEM((rows_p, KV_LORA), F32),
            pltpu.VMEM((rows_p, KV_CHUNK), F32),
            pltpu.VMEM((rows_p, KV_CHUNK), F32),
        ],
    )
    return pl.pallas_call(
        _attn_kernel,
        grid_spec=grid_spec,
        out_shape=[
            jax.ShapeDtypeStruct((N_SAMPLE // Q_BLOCK, N_HEADS, Q_BLOCK, KV_LORA), F32),
            jax.ShapeDtypeStruct((N_PROMPT, D_MLA_OUT), BF16),
        ],
        compiler_params=_cparams(("arbitrary",)),
        name="attention",
    )(page_table.reshape(-1), qs, ckv_s, kpe_s, cache_ckv, cache_kpet, qp, kt, vb, wuv)


def _uv_proj_kernel(ol_ref, wuv_ref, o_ref, *, nsub):
    for j in range(nsub):
        for h in range(N_HEADS):
            o = _dot(ol_ref[j, h].astype(BF16), wuv_ref[h])
            o_ref[j * Q_BLOCK:(j + 1) * Q_BLOCK, h * V_HEAD:(h + 1) * V_HEAD] = o.astype(o_ref.dtype)


def _uv_proj_sample(o_lat, wuv, nsub=4):
    nb = o_lat.shape[0]
    rows = nsub * Q_BLOCK
    return pl.pallas_call(
        functools.partial(_uv_proj_kernel, nsub=nsub),
        grid=(nb // nsub,),
        in_specs=[
            pl.BlockSpec((nsub, N_HEADS, Q_BLOCK, KV_LORA), lambda i: (i, 0, 0, 0)),
            _resident((N_HEADS, KV_LORA, V_HEAD)),
        ],
        out_specs=pl.BlockSpec((rows, D_MLA_OUT), lambda i: (i, 0)),
        out_shape=jax.ShapeDtypeStruct((nb * Q_BLOCK, D_MLA_OUT), BF16),
        compiler_params=_cparams(("arbitrary",)),
        name="uv_proj_sample",
    )(o_lat, wuv)


def _mm_ln_kernel(*refs, lhs_split, split_resid, split_out, tm):
    is_prompt = pl.program_id(0) < N_PROMPT // tm
    refs = list(refs)

    def take(split):
        if split:
            p_ref, s_ref = refs.pop(0), refs.pop(0)
            return lambda rows: jnp.where(is_prompt, p_ref[rows, :], s_ref[rows, :])
        ref = refs.pop(0)
        return lambda rows: ref[rows, :]

    lhs = [take(split) for split in lhs_split]
    w_ref = refs.pop(0)
    resid = take(split_resid)
    g_ref, b_ref = refs.pop(0), refs.pop(0)
    out_refs = refs

    sub = min(tm, LN_SUB_ROWS)
    for r in range(0, tm, sub):
        rows = slice(r, r + sub)
        acc = None
        off = 0
        for piece in lhs:
            a = piece(rows)
            part = _dot(a, w_ref[off:off + a.shape[1], :])
            acc = part if acc is None else acc + part
            off += a.shape[1]
        y = _layer_norm(DEEPNORM_ALPHA * resid(rows) + acc, g_ref[...], b_ref[...])
        if split_out:
            @pl.when(is_prompt)
            def _():
                out_refs[0][rows, :] = y

            @pl.when(jnp.logical_not(is_prompt))
            def _():
                out_refs[1][rows, :] = y
        else:
            out_refs[0][rows, :] = y
            out_refs[1][rows, :] = y.astype(BF16)


def _mm_ln(lhs_list, w, resid, g, b, *, tm, split_out=False, layer=None):
    operands, in_specs = [], []

    def add_rows(v):
        if isinstance(v, tuple):
            operands.extend(v)
            in_specs.extend([_prompt_rows(tm, v[0].shape[1]), _sample_rows(tm, v[0].shape[1])])
        else:
            operands.append(v)
            in_specs.append(pl.BlockSpec((tm, v.shape[1]), lambda i: (i, 0)))
        return isinstance(v, tuple)

    lhs_split = tuple(add_rows(v) for v in lhs_list)
    operands.append(w)
    if layer is None:
        in_specs.append(_resident(w.shape))
    else:
        in_specs.append(pl.BlockSpec((None,) + w.shape[1:], lambda i: (layer, 0, 0), pipeline_mode=pl.Buffered(1)))
    split_resid = add_rows(resid)
    operands.extend([g, b])
    in_specs.extend([_resident((1, D_MODEL)), _resident((1, D_MODEL))])
    if split_out:
        out_specs = [_prompt_rows(tm, D_MODEL), _sample_rows(tm, D_MODEL)]
        out_shape = [jax.ShapeDtypeStruct((N_PROMPT, D_MODEL), F32), jax.ShapeDtypeStruct((N_SAMPLE, D_MODEL), F32)]
    else:
        row = pl.BlockSpec((tm, D_MODEL), lambda i: (i, 0))
        out_specs = [row, row]
        out_shape = [jax.ShapeDtypeStruct((N_TOK, D_MODEL), F32), jax.ShapeDtypeStruct((N_TOK, D_MODEL), BF16)]
    return pl.pallas_call(
        functools.partial(_mm_ln_kernel, lhs_split=lhs_split, split_resid=split_resid, split_out=split_out, tm=tm),
        grid=(N_TOK // tm,),
        in_specs=in_specs,
        out_specs=out_specs,
        out_shape=out_shape,
        compiler_params=_cparams(("arbitrary",)),
        name="mm_ln",
    )(*operands)


def _ffn_up_kernel(x_ref, wg_ref, wu_ref, h_ref, wgb_ref, wub_ref):
    @pl.when(pl.program_id(1) == 0)
    def _():
        wgb_ref[...] = wg_ref[...].astype(BF16)
        wub_ref[...] = wu_ref[...].astype(BF16)

    for r in range(0, x_ref.shape[0], FFN_SUB_ROWS):
        x = x_ref[r:r + FFN_SUB_ROWS, :]
        h = jax.nn.silu(_dot(x, wgb_ref[...])) * _dot(x, wub_ref[...])
        h_ref[r:r + FFN_SUB_ROWS, :] = h.astype(h_ref.dtype)


def _ffn_up(xb, wg, wu, layer, tm=N_TOK // 4, tn=512):
    wblk = pl.BlockSpec((None, D_MODEL, tn), lambda j, i: (layer, 0, j))
    return pl.pallas_call(
        _ffn_up_kernel,
        grid=(D_FF // tn, N_TOK // tm),
        in_specs=[pl.BlockSpec((tm, D_MODEL), lambda j, i: (i, 0)), wblk, wblk],
        out_specs=pl.BlockSpec((tm, tn), lambda j, i: (i, j)),
        out_shape=jax.ShapeDtypeStruct((N_TOK, D_FF), BF16),
        scratch_shapes=[pltpu.VMEM((D_MODEL, tn), BF16), pltpu.VMEM((D_MODEL, tn), BF16)],
        compiler_params=_cparams(("arbitrary", "arbitrary")),
        name="ffn_up",
    )(xb, wg, wu)


def _conv_taps(read, cw_ref):
    conv = read(CONV_W - 1) * cw_ref[0:1, :]
    for j in range(1, CONV_W):
        conv = conv + read(CONV_W - 1 - j) * cw_ref[j:j + 1, :]
    return conv


def _in_odd_cast_weights(wv_ref, wb_ref, wc_ref, wvb_ref, wbb_ref, wcb_ref):
    @pl.when(pl.program_id(1) == 0)
    def _():
        wvb_ref[...] = wv_ref[...].astype(BF16)
        wbb_ref[...] = wb_ref[...].astype(BF16)
        wcb_ref[...] = wc_ref[...].astype(BF16)


def _in_odd_prompt_kernel(x_ref, wv_ref, wb_ref, wc_ref, cw_ref, g_ref, ztail_ref,
                          wvb_ref, wbb_ref, wcb_ref, zp_ref, *, tm):
    i = pl.program_id(1)
    tiles_per_seq = SEQ // tm
    _in_odd_cast_weights(wv_ref, wb_ref, wc_ref, wvb_ref, wbb_ref, wcb_ref)

    @pl.when(i % tiles_per_seq == 0)
    def _():
        zp_ref[0:CONV_HIST, :] = jnp.zeros((CONV_HIST, zp_ref.shape[1]), F32)

    for r in range(0, tm, FFN_SUB_ROWS):
        x = x_ref[r:r + FFN_SUB_ROWS, :]
        new = slice(CONV_HIST + r, CONV_HIST + r + FFN_SUB_ROWS)
        zp_ref[new, :] = _dot(x, wcb_ref[...]) * _dot(x, wvb_ref[...])
        conv = _conv_taps(lambda back: zp_ref[new.start - back:new.stop - back, :], cw_ref)
        g_ref[r:r + FFN_SUB_ROWS, :] = (_dot(x, wbb_ref[...]) * conv).astype(g_ref.dtype)
    zp_ref[0:CONV_HIST, :] = zp_ref[tm:tm + CONV_HIST, :]

    @pl.when(i % tiles_per_seq == tiles_per_seq - 1)
    def _():
        ztail_ref[...] = zp_ref[tm:tm + CONV_HIST, :]


def _in_odd_sample_kernel(x_ref, wv_ref, wb_ref, wc_ref, zs_ref, gbs_ref, wvb_ref, wbb_ref, wcb_ref):
    _in_odd_cast_weights(wv_ref, wb_ref, wc_ref, wvb_ref, wbb_ref, wcb_ref)
    x = x_ref[...]
    zs_ref[...] = _dot(x, wcb_ref[...]) * _dot(x, wvb_ref[...])
    gbs_ref[...] = _dot(x, wbb_ref[...])


def _in_odd(xb, w_in, conv_w, tm=1024, tn=512):
    nb = D_MODEL // tn
    nbp = N_PROMPT // tm
    tiles_per_seq = SEQ // tm
    wblks = [pl.BlockSpec((D_MODEL, tn), lambda j, i, part=part: (0, part * nb + j)) for part in range(3)]
    wscratch = [pltpu.VMEM((D_MODEL, tn), BF16)] * 3
    g_prompt, z_tail = pl.pallas_call(
        functools.partial(_in_odd_prompt_kernel, tm=tm),
        grid=(nb, nbp),
        in_specs=[pl.BlockSpec((tm, D_MODEL), lambda j, i: (i, 0))] + wblks
        + [pl.BlockSpec((CONV_W, tn), lambda j, i: (0, j))],
        out_specs=[
            pl.BlockSpec((tm, tn), lambda j, i: (i, j)),
            pl.BlockSpec((CONV_HIST, tn), lambda j, i: (i // tiles_per_seq, j)),
        ],
        out_shape=[
            jax.ShapeDtypeStruct((N_PROMPT, D_MODEL), BF16),
            jax.ShapeDtypeStruct((BATCH * CONV_HIST, D_MODEL), F32),
        ],
        scratch_shapes=wscratch + [pltpu.VMEM((tm + CONV_HIST, tn), F32)],
        compiler_params=_cparams(("arbitrary", "arbitrary")),
        name="in_odd_prompt",
    )(xb, w_in, w_in, w_in, conv_w)
    sample_blk = pl.BlockSpec((tm, tn), lambda j, i: (i, j))
    z_s, gb_s = pl.pallas_call(
        _in_odd_sample_kernel,
        grid=(nb, N_SAMPLE // tm),
        in_specs=[pl.BlockSpec((tm, D_MODEL), lambda j, i: (nbp + i, 0))] + wblks,
        out_specs=[sample_blk, sample_blk],
        out_shape=[jax.ShapeDtypeStruct((N_SAMPLE, D_MODEL), F32)] * 2,
        scratch_shapes=wscratch,
        compiler_params=_cparams(("arbitrary", "arbitrary")),
        name="in_odd_sample",
    )(xb, w_in, w_in, w_in)
    return g_prompt, z_tail, z_s, gb_s


def _conv_sample_kernel(z_ref, hist_ref, gb_ref, cw_ref, g_ref, zp_ref, *, S):
    zp_ref[:, 0:CONV_HIST, :] = hist_ref[...]
    zp_ref[:, CONV_HIST:, :] = z_ref[...]
    conv = _conv_taps(lambda back: zp_ref[:, CONV_HIST - back:CONV_HIST - back + DEC_SEQ, :], cw_ref)
    g_ref[...] = (gb_ref[...] * conv).reshape(S * DEC_SEQ, D_MODEL).astype(g_ref.dtype)


def _conv_sample(z3, hist, gb3, conv_w, S=32):
    blk = lambda i: (i, 0, 0)
    rows = S * DEC_SEQ
    return pl.pallas_call(
        functools.partial(_conv_sample_kernel, S=S),
        grid=(DEC_BATCH // S,),
        in_specs=[
            pl.BlockSpec((S, DEC_SEQ, D_MODEL), blk),
            pl.BlockSpec((S, CONV_HIST, D_MODEL), blk),
            pl.BlockSpec((S, DEC_SEQ, D_MODEL), blk),
            _resident((CONV_W, D_MODEL)),
        ],
        out_specs=pl.BlockSpec((rows, D_MODEL), lambda i: (i, 0)),
        out_shape=jax.ShapeDtypeStruct((N_SAMPLE, D_MODEL), BF16),
        scratch_shapes=[pltpu.VMEM((S, CONV_HIST + DEC_SEQ, D_MODEL), F32)],
        compiler_params=_cparams(("arbitrary",)),
        name="conv_sample",
    )(z3, hist, gb3, conv_w)


def _rope_tables():
    pos = jnp.concatenate([jnp.arange(SEQ), PAST_LEN + jnp.arange(DEC_SEQ)])
    inv = ROPE_THETA ** (-jnp.arange(0, QK_ROPE, 2, dtype=F32) / QK_ROPE)
    ang = pos.astype(F32)[:, None] * inv[None, :]
    cos, sin = jnp.cos(ang), jnp.sin(ang)
    cos2 = jnp.concatenate([cos, cos], axis=1)
    sin2 = jnp.concatenate([-sin, sin], axis=1)

    def per_token(t):
        return jnp.concatenate([jnp.tile(t[:SEQ], (BATCH, 1)), jnp.tile(t[SEQ:], (DEC_BATCH, 1))], axis=0)

    return (per_token(jnp.tile(cos2, (1, 2))), per_token(jnp.tile(sin2, (1, 2))),
            per_token(jnp.concatenate([cos2, sin2], axis=1)))


def _swap_halves(w):
    half = QK_ROPE // 2
    return jnp.concatenate([w[..., half:], w[..., :half]], axis=-1)


def kernel(x_prompt, x_sample, cache_ckv, cache_kpe, state_pool, state_conv, page_table, w_in_even, pool_w, pool_scale, q_norm_g, w_q_b, kv_norm_g, w_uk, w_uv, w_out_even, w_in_odd, conv_w, w_out_odd, ffn_w_gate, ffn_w_up, ffn_w_down, ln1_g, ln1_b, ln2_g, ln2_b):
    assert DEPTH == 2
    n_pool_pages = cache_ckv.shape[1]
    cos, sin, cos_sin = _rope_tables()
    xp = x_prompt.reshape(N_PROMPT, D_MODEL)
    xs = x_sample.reshape(N_SAMPLE, D_MODEL)

    def row(v):
        return v.reshape(1, -1)

    w_down = ffn_w_down.astype(BF16)

    def ffn(layer, xf, xb, last):
        hmid = _ffn_up(xb, ffn_w_gate, ffn_w_up, layer)
        return _mm_ln([hmid], w_down, xf, row(ln2_g[layer]), row(ln2_b[layer]),
                      tm=256, split_out=last, layer=layer)

    e = 0
    w_in = w_in_even[e]
    w_all = jnp.concatenate([w_in, _swap_halves(w_in[:, D_POOL + Q_LORA + KV_LORA:])], axis=1).astype(BF16)
    u, cq, ckv_p, ckv_s, kpe_p, kpe_s, vb, kt = _in_even(
        xp, xs, w_all, row(q_norm_g[e]), row(kv_norm_g[e]), cos_sin)

    wq = w_q_b[e]
    wq_rope = wq[:, :, QK_NOPE:]
    qp, qs = _q_proj(
        cq,
        wq[:, :, :QK_NOPE].reshape(Q_LORA, N_HEADS * QK_NOPE).astype(BF16),
        wq_rope.reshape(Q_LORA, N_HEADS * QK_ROPE).astype(BF16),
        _swap_halves(wq_rope).reshape(Q_LORA, N_HEADS * QK_ROPE).astype(BF16),
        jnp.transpose(w_uk[e], (1, 2, 0)).astype(BF16), cos, sin)

    pw = pool_w[e].astype(BF16)
    ps = row(pool_scale[e])
    u_sample = u[N_PROMPT:].reshape(DEC_BATCH, DEC_SEQ, D_POOL)
    pool_hist = jnp.pad(state_pool[e], ((0, 0), (POOL_HIST - POOL_BUF, 0), (0, 0)))
    y_pool = (_pool_prompt(u, pw, ps), _pool_sample(u_sample, pool_hist, pw, ps))

    wuv = jnp.transpose(w_uv[e], (1, 0, 2)).astype(BF16)
    o_lat_s, o_prompt = _attention(page_table, qs, ckv_s, kpe_s,
                                   cache_ckv[e].reshape(n_pool_pages, PAGE_SIZE, KV_LORA),
                                   jnp.swapaxes(cache_kpe[e], 1, 2), qp, kt, vb, wuv)
    o_sample = _uv_proj_sample(o_lat_s, wuv)
    xf, xb = _mm_ln([y_pool, (o_prompt, o_sample)], w_out_even[e].astype(BF16), (xp, xs),
                    row(ln1_g[0]), row(ln1_b[0]), tm=512)
    xf, xb = ffn(0, xf, xb, False)

    od = 0
    g_prompt, z_tail, z_s, gb_s = _in_odd(xb, w_in_odd[od], conv_w[od])
    z_sample = z_s.reshape(DEC_BATCH, DEC_SEQ, D_MODEL)
    gb_sample = gb_s.reshape(DEC_BATCH, DEC_SEQ, D_MODEL)
    conv_hist = jnp.pad(state_conv[od], ((0, 0), (CONV_HIST - CONV_BUF, 0), (0, 0)))
    g = (g_prompt, _conv_sample(z_sample, conv_hist, gb_sample, conv_w[od]))
    xf, xb = _mm_ln([g], w_out_odd[od].astype(BF16), xf, row(ln1_g[1]), row(ln1_b[1]), tm=512)
    y_p, y_s = ffn(1, xf, xb, True)

    pool_prompt = jnp.stack([u[(b + 1) * SEQ - POOL_BUF:(b + 1) * SEQ] for b in range(BATCH)])
    pool_sample = jnp.concatenate([state_pool[e][:, DEC_SEQ:], u_sample], axis=1)
    conv_prompt = z_tail.reshape(BATCH, CONV_HIST, D_MODEL)[:, CONV_HIST - CONV_BUF:]
    conv_sample = z_sample[:, DEC_SEQ - CONV_BUF:]
    return (y_p.reshape(BATCH, SEQ, D_MODEL), y_s.reshape(DEC_BATCH, DEC_SEQ, D_MODEL),
            ckv_p.reshape(1, BATCH, SEQ, KV_LORA), kpe_p.reshape(1, BATCH, SEQ, QK_ROPE),
            pool_prompt[None], conv_prompt[None],
            ckv_s.reshape(1, DEC_BATCH, DEC_SEQ, KV_LORA), kpe_s.reshape(1, DEC_BATCH, DEC_SEQ, QK_ROPE),
            pool_sample[None], conv_sample[None])
```

```python
import functools

import jax
import jax.numpy as jnp
from jax import lax
from jax.experimental import pallas as pl
from jax.experimental.pallas import tpu as pltpu

F32 = jnp.float32
BF16 = jnp.bfloat16

D_MODEL = 2048
BATCH = 4
SEQ = 2048
DEPTH = 2
DEC_BATCH = 128
DEC_SEQ = 8
PAST_LEN = 16384
PAGE_SIZE = 128
N_PAGES = PAST_LEN // PAGE_SIZE
POOL_WINDOWS = (2, 4, 8, 16)
N_POOL_GROUPS = len(POOL_WINDOWS)
D_POOL = D_MODEL // 2
POOL_GROUP = D_POOL // N_POOL_GROUPS
POOL_BUF = max(POOL_WINDOWS) - 1
N_HEADS = 8
QK_NOPE = 128
QK_ROPE = 64
V_HEAD = 128
Q_LORA = 512
KV_LORA = 512
ROPE_THETA = 10000.0
ATTN_SCALE = (QK_NOPE + QK_ROPE) ** -0.5
Q_BLOCK = 128
D_MLA_OUT = N_HEADS * V_HEAD
CONV_W = 3
CONV_BUF = CONV_W - 1
D_FF = ((8 * D_MODEL + 3 * 256 - 1) // (3 * 256)) * 256
DEEPNORM_ALPHA = (2 * DEPTH) ** 0.25

N_PROMPT = BATCH * SEQ
N_SAMPLE = DEC_BATCH * DEC_SEQ
N_TOK = N_PROMPT + N_SAMPLE
D_QK = KV_LORA + QK_ROPE
POOL_HIST = 16
CONV_HIST = 8
KV_CHUNK = 512
FFN_SUB_ROWS = 256
LN_SUB_ROWS = 256
VMEM_LIMIT_MB = 56


def _cparams(dims, vmem_mb=VMEM_LIMIT_MB):
    return pltpu.CompilerParams(dimension_semantics=dims, vmem_limit_bytes=vmem_mb * 1024 * 1024)


def _resident(shape):
    zeros = (0,) * len(shape)
    return pl.BlockSpec(shape, lambda *_: zeros, pipeline_mode=pl.Buffered(1))


def _prompt_rows(tm, width):
    last = N_PROMPT // tm - 1
    return pl.BlockSpec((tm, width), lambda i, *_: (jnp.minimum(i, last), 0))


def _sample_rows(tm, width):
    first = N_PROMPT // tm
    return pl.BlockSpec((tm, width), lambda i, *_: (jnp.maximum(i - first, 0), 0))


def _dot(a, b):
    return jnp.dot(a, b, preferred_element_type=F32)


def _dot_nt(a, b):
    return lax.dot_general(a, b, (((1,), (1,)), ((), ())), preferred_element_type=F32)


def _rms(h, g, eps=1e-6):
    return h * lax.rsqrt(jnp.mean(h * h, axis=-1, keepdims=True) + eps) * g


def _layer_norm(v, g, b, eps=1e-5):
    mu = jnp.mean(v, axis=-1, keepdims=True)
    vc = v - mu
    var = jnp.mean(vc * vc, axis=-1, keepdims=True)
    return vc * lax.rsqrt(var + eps) * g + b


def _softmax_update(m_old, l_old, acc, s, v):
    m_new = jnp.maximum(m_old, jnp.max(s, axis=-1, keepdims=True))
    alpha = jnp.exp(m_old - m_new)
    p = jnp.exp(s - m_new)
    l_new = alpha * l_old + jnp.sum(p, axis=-1, keepdims=True)
    return m_new, l_new, alpha * acc + _dot(p.astype(BF16), v)


D_IN_EVEN = D_POOL + Q_LORA + KV_LORA + 2 * QK_ROPE


def _in_even_kernel(xp_ref, xs_ref, w_ref, qg_ref, kvg_ref, cs_ref,
                    u_ref, cq_ref, ckvp_ref, ckvs_ref, kpep_ref, kpes_ref, vb_ref, kt_ref, *, tm):
    i = pl.program_id(0)
    is_prompt = i < N_PROMPT // tm
    xb = jnp.where(is_prompt, xp_ref[...], xs_ref[...]).astype(BF16)
    h = _dot(xb, w_ref[...])
    o1, o2 = D_POOL + Q_LORA, D_POOL + Q_LORA + KV_LORA
    u_ref[...] = h[:, :D_POOL]
    cq_ref[...] = _rms(h[:, D_POOL:o1], qg_ref[...]).astype(BF16)
    ckv = _rms(h[:, o1:o2], kvg_ref[...])
    prod = h[:, o2:] * cs_ref[...]
    kpe2 = prod + pltpu.roll(prod, QK_ROPE, 1)
    kpe = kpe2[:, :QK_ROPE]

    @pl.when(is_prompt)
    def _():
        ckvp_ref[...] = ckv
        kpep_ref[...] = kpe
        vb_ref[...] = ckv.astype(BF16)
        for c in range(tm // KV_CHUNK):
            rows = slice(c * KV_CHUNK, (c + 1) * KV_CHUNK)
            kt_ref[c, :KV_LORA, :] = ckv[rows, :].T.astype(BF16)
            kt_ref[c, KV_LORA:, :] = kpe2[rows, :].T[:QK_ROPE, :].astype(BF16)

    @pl.when(jnp.logical_not(is_prompt))
    def _():
        ckvs_ref[...] = ckv
        kpes_ref[...] = kpe


def _in_even(xp, xs, w_all, qg, kvg, cs, tm=512):
    row = lambda i: (i, 0)
    cpb = tm // KV_CHUNK
    last = N_PROMPT // tm - 1
    return pl.pallas_call(
        functools.partial(_in_even_kernel, tm=tm),
        grid=(N_TOK // tm,),
        in_specs=[
            _prompt_rows(tm, D_MODEL),
            _sample_rows(tm, D_MODEL),
            _resident((D_MODEL, D_IN_EVEN)),
            _resident((1, Q_LORA)),
            _resident((1, KV_LORA)),
            pl.BlockSpec((tm, 2 * QK_ROPE), row),
        ],
        out_specs=[
            pl.BlockSpec((tm, D_POOL), row),
            pl.BlockSpec((tm, Q_LORA), row),
            _prompt_rows(tm, KV_LORA),
            _sample_rows(tm, KV_LORA),
            _prompt_rows(tm, QK_ROPE),
            _sample_rows(tm, QK_ROPE),
            _prompt_rows(tm, KV_LORA),
            pl.BlockSpec((cpb, D_QK, KV_CHUNK), lambda i: (jnp.minimum(i, last), 0, 0)),
        ],
        out_shape=[
            jax.ShapeDtypeStruct((N_TOK, D_POOL), F32),
            jax.ShapeDtypeStruct((N_TOK, Q_LORA), BF16),
            jax.ShapeDtypeStruct((N_PROMPT, KV_LORA), F32),
            jax.ShapeDtypeStruct((N_SAMPLE, KV_LORA), F32),
            jax.ShapeDtypeStruct((N_PROMPT, QK_ROPE), F32),
            jax.ShapeDtypeStruct((N_SAMPLE, QK_ROPE), F32),
            jax.ShapeDtypeStruct((N_PROMPT, KV_LORA), BF16),
            jax.ShapeDtypeStruct((N_PROMPT // KV_CHUNK, D_QK, KV_CHUNK), BF16),
        ],
        compiler_params=_cparams(("arbitrary",)),
        name="in_even",
    )(xp, xs, w_all, qg, kvg, cs)


def _q_proj_kernel(cq_ref, wqn_ref, wqr_ref, wqrr_ref, wukt_ref, cos_ref, sin_ref, qp_ref, qs_ref, *, nsub):
    i = pl.program_id(0)
    is_prompt = i < N_PROMPT // (nsub * Q_BLOCK)
    cq = cq_ref[...]
    qn = _dot(cq, wqn_ref[...])
    cos = jnp.concatenate([cos_ref[...]] * (N_HEADS // 2), axis=1)
    sin = jnp.concatenate([sin_ref[...]] * (N_HEADS // 2), axis=1)
    qpe = (_dot(cq, wqr_ref[...]) * cos + _dot(cq, wqrr_ref[...]) * sin) * ATTN_SCALE

    def emit(q_ref):
        for h in range(N_HEADS):
            ql = _dot(qn[:, h * QK_NOPE:(h + 1) * QK_NOPE].astype(BF16), wukt_ref[h]) * ATTN_SCALE
            for j in range(nsub):
                rows = slice(j * Q_BLOCK, (j + 1) * Q_BLOCK)
                q_ref[j, h, :, :KV_LORA] = ql[rows].astype(q_ref.dtype)
                q_ref[j, h, :, KV_LORA:] = qpe[rows, h * QK_ROPE:(h + 1) * QK_ROPE].astype(q_ref.dtype)

    @pl.when(is_prompt)
    def _():
        emit(qp_ref)

    @pl.when(jnp.logical_not(is_prompt))
    def _():
        emit(qs_ref)


def _q_proj(cq, wqn, wqr, wqrr, wukt, cos, sin, tm=512):
    nsub = tm // Q_BLOCK
    nbp = N_PROMPT // tm
    row = lambda i: (i, 0)
    blk = (nsub, N_HEADS, Q_BLOCK, D_QK)
    return pl.pallas_call(
        functools.partial(_q_proj_kernel, nsub=nsub),
        grid=(N_TOK // tm,),
        in_specs=[
            pl.BlockSpec((tm, Q_LORA), row),
            _resident((Q_LORA, N_HEADS * QK_NOPE)),
            _resident((Q_LORA, N_HEADS * QK_ROPE)),
            _resident((Q_LORA, N_HEADS * QK_ROPE)),
            _resident((N_HEADS, QK_NOPE, KV_LORA)),
            pl.BlockSpec((tm, 2 * QK_ROPE), row),
            pl.BlockSpec((tm, 2 * QK_ROPE), row),
        ],
        out_specs=[
            pl.BlockSpec(blk, lambda i: (jnp.minimum(i, nbp - 1), 0, 0, 0)),
            pl.BlockSpec(blk, lambda i: (jnp.maximum(i - nbp, 0), 0, 0, 0)),
        ],
        out_shape=[
            jax.ShapeDtypeStruct((N_PROMPT // Q_BLOCK, N_HEADS, Q_BLOCK, D_QK), BF16),
            jax.ShapeDtypeStruct((N_SAMPLE // Q_BLOCK, N_HEADS, Q_BLOCK, D_QK), F32),
        ],
        compiler_params=_cparams(("arbitrary",)),
        name="q_proj",
    )(cq, wqn, wqr, wqrr, wukt, cos, sin)


def _pool_groups(read, pos, w_ref, scale_ref, write):
    for g, w in enumerate(POOL_WINDOWS):
        lanes = slice(g * POOL_GROUP, (g + 1) * POOL_GROUP)
        cur = read(0, lanes)
        win = cur
        for j in range(1, w):
            win = win + read(j, lanes)
        cnt = jnp.minimum(pos + 1, w).astype(F32)
        diff = (win / cnt - cur).astype(BF16)
        write(lanes, _dot(diff, w_ref[g]) * scale_ref[:, lanes])


def _pool_prompt_kernel(u_ref, w_ref, scale_ref, y_ref, xp_ref, *, T):
    t = pl.program_id(1)

    @pl.when(t == 0)
    def _():
        xp_ref[0:POOL_HIST, :] = jnp.zeros((POOL_HIST, D_POOL), F32)

    xp_ref[POOL_HIST:POOL_HIST + T, :] = u_ref[...]
    pos = t * T + lax.broadcasted_iota(jnp.int32, (T, 1), 0)

    def read(j, lanes):
        return xp_ref[POOL_HIST - j:POOL_HIST - j + T, lanes]

    def write(lanes, y):
        y_ref[:, lanes] = y.astype(y_ref.dtype)

    _pool_groups(read, pos, w_ref, scale_ref, write)
    xp_ref[0:POOL_HIST, :] = xp_ref[T:T + POOL_HIST, :]


def _pool_prompt(u, pool_w, pool_scale, T=512):
    nt = SEQ // T
    return pl.pallas_call(
        functools.partial(_pool_prompt_kernel, T=T),
        grid=(BATCH, nt),
        in_specs=[
            pl.BlockSpec((T, D_POOL), lambda b, t: (b * nt + t, 0)),
            _resident((N_POOL_GROUPS, POOL_GROUP, POOL_GROUP)),
            _resident((1, D_POOL)),
        ],
        out_specs=pl.BlockSpec((T, D_POOL), lambda b, t: (b * nt + t, 0)),
        out_shape=jax.ShapeDtypeStruct((N_PROMPT, D_POOL), BF16),
        scratch_shapes=[pltpu.VMEM((T + POOL_HIST, D_POOL), F32)],
        compiler_params=_cparams(("arbitrary", "arbitrary")),
        name="pool_prompt",
    )(u, pool_w, pool_scale)


def _pool_sample_kernel(u_ref, hist_ref, w_ref, scale_ref, y_ref, xp_ref, *, S):
    xp_ref[:, 0:POOL_HIST, :] = hist_ref[...]
    xp_ref[:, POOL_HIST:, :] = u_ref[...]
    pos = PAST_LEN + lax.broadcasted_iota(jnp.int32, (S, DEC_SEQ, 1), 1)
    pos = pos.reshape(S * DEC_SEQ, 1)

    def read(j, lanes):
        return xp_ref[:, POOL_HIST - j:POOL_HIST - j + DEC_SEQ, lanes].reshape(S * DEC_SEQ, POOL_GROUP)

    def write(lanes, y):
        y_ref[:, lanes] = y.astype(y_ref.dtype)

    _pool_groups(read, pos, w_ref, scale_ref, write)


def _pool_sample(u3, hist, pool_w, pool_scale, S=32):
    rows = S * DEC_SEQ
    return pl.pallas_call(
        functools.partial(_pool_sample_kernel, S=S),
        grid=(DEC_BATCH // S,),
        in_specs=[
            pl.BlockSpec((S, DEC_SEQ, D_POOL), lambda i: (i, 0, 0)),
            pl.BlockSpec((S, POOL_HIST, D_POOL), lambda i: (i, 0, 0)),
            _resident((N_POOL_GROUPS, POOL_GROUP, POOL_GROUP)),
            _resident((1, D_POOL)),
        ],
        out_specs=pl.BlockSpec((rows, D_POOL), lambda i: (i, 0)),
        out_shape=jax.ShapeDtypeStruct((N_SAMPLE, D_POOL), BF16),
        scratch_shapes=[pltpu.VMEM((S, POOL_HIST + DEC_SEQ, D_POOL), F32)],
        compiler_params=_cparams(("arbitrary",)),
        name="pool_sample",
    )(u3, hist, pool_w, pool_scale)


def _prompt_attn_block(i, q_ref, kt_ref, v_ref, wuv_ref, o_ref, m_ref, l_ref, acc_ref, sa_ref, sb_ref, heads):
    rows_all = heads * Q_BLOCK
    m_ref[...] = jnp.full((rows_all, 1), -jnp.inf, F32)
    l_ref[...] = jnp.zeros((rows_all, 1), F32)
    acc_ref[...] = jnp.zeros((rows_all, KV_LORA), F32)
    q = q_ref[0].reshape(rows_all, D_QK)
    q_pos = i * Q_BLOCK + (lax.broadcasted_iota(jnp.int32, (rows_all, KV_CHUNK), 0) & (Q_BLOCK - 1))
    key = lax.broadcasted_iota(jnp.int32, (rows_all, KV_CHUNK), 1)

    def scores(j, s_ref):
        s = _dot(q, kt_ref[j])
        s_ref[...] = jnp.where(j * KV_CHUNK + key <= q_pos, s, -jnp.inf)

    def attend(j, s_ref):
        v = v_ref[pl.ds(pl.multiple_of(j * KV_CHUNK, KV_CHUNK), KV_CHUNK), :]
        m_new, l_new, acc_new = _softmax_update(m_ref[...], l_ref[...], acc_ref[...], s_ref[...], v)
        m_ref[...] = m_new
        l_ref[...] = l_new
        acc_ref[...] = acc_new

    n_chunks = (i * Q_BLOCK + Q_BLOCK + KV_CHUNK - 1) // KV_CHUNK
    n_pairs = (n_chunks - 1) // 2
    scores(0, sa_ref)

    def body(t, c):
        scores(2 * t + 1, sb_ref)
        attend(2 * t, sa_ref)
        scores(2 * t + 2, sa_ref)
        attend(2 * t + 1, sb_ref)
        return c

    lax.fori_loop(0, n_pairs, body, 0)
    done = 2 * n_pairs

    @pl.when(done + 2 == n_chunks)
    def _():
        scores(done + 1, sb_ref)
        attend(done, sa_ref)
        attend(done + 1, sb_ref)

    @pl.when(done + 1 == n_chunks)
    def _():
        attend(done, sa_ref)

    for h in range(heads):
        rows = slice(h * Q_BLOCK, (h + 1) * Q_BLOCK)
        o_lat = (acc_ref[rows] * (1.0 / l_ref[rows])).astype(BF16)
        o_ref[:, h * V_HEAD:(h + 1) * V_HEAD] = _dot(o_lat, wuv_ref[h]).astype(o_ref.dtype)


PAGES_PER_CHUNK = 16
CHUNK_KEYS = PAGES_PER_CHUNK * PAGE_SIZE
CHUNKS_PER_SEQ = N_PAGES // PAGES_PER_CHUNK
PAGES_PER_GROUP = KV_LORA // PAGE_SIZE
GROUP_KEYS = PAGES_PER_GROUP * PAGE_SIZE
ROPE_ROWS = CHUNK_KEYS // GROUP_KEYS * QK_ROPE
DEC_SLOTS = 4
DEC_AHEAD = DEC_SLOTS - 1
NEW_KEYS_PAD = 16
ATTN_HEADS_PER_STEP = N_HEADS // 2
BLOCKS_PER_SEQ = SEQ // Q_BLOCK
assert CHUNKS_PER_SEQ % DEC_SLOTS == 0
assert DEC_BATCH * ATTN_HEADS_PER_STEP == BATCH * BLOCKS_PER_SEQ * N_HEADS


def _prompt_block(k):
    b, r = k // BLOCKS_PER_SEQ, k % BLOCKS_PER_SEQ
    return b, jnp.where(r % 2 == 0, r // 2, BLOCKS_PER_SEQ - 1 - r // 2)


def _attn_kernel(pt_ref, q_ref, ckvn_ref, kpen_ref, cckv_hbm, ckpet_hbm, qp_ref, kt_ref, v_ref, wuv_ref,
                 o_ref, op_ref, kv_buf, kvb_a, kvb_b, sc_a, sc_b, sem,
                 m_ref, l_ref, acc_ref, sa_ref, sb_ref):
    s = pl.program_id(0)
    nseq = pl.num_programs(0)
    rows = N_HEADS * DEC_SEQ

    def rope_rows(group):
        return pl.ds(CHUNK_KEYS + group * QK_ROPE, QK_ROPE)

    def start_chunk(seq, c):
        slot = c % DEC_SLOTS
        for p in range(PAGES_PER_CHUNK):
            page = pt_ref[(seq * CHUNKS_PER_SEQ + c) * PAGES_PER_CHUNK + p]
            lanes = pl.ds(p % PAGES_PER_GROUP * PAGE_SIZE, PAGE_SIZE)
            pltpu.make_async_copy(cckv_hbm.at[page], kv_buf.at[slot, pl.ds(p * PAGE_SIZE, PAGE_SIZE), :],
                                  sem.at[slot]).start(priority=1)
            pltpu.make_async_copy(ckpet_hbm.at[page], kv_buf.at[slot, rope_rows(p // PAGES_PER_GROUP), lanes],
                                  sem.at[slot]).start(priority=0)

    def wait_chunk(c):
        slot = c % DEC_SLOTS
        pltpu.make_async_copy(kv_buf.at[slot], kv_buf.at[slot], sem.at[slot]).wait()

    @pl.when(s == 0)
    def _():
        for c in range(DEC_AHEAD):
            start_chunk(0, c)

    q = q_ref[0].reshape(rows, D_QK)
    q_lat = q[:, :KV_LORA].astype(BF16)
    q_pe = q[:, KV_LORA:].astype(BF16)

    def scores(c, kvb_ref, sc_ref):
        nxt = c + DEC_AHEAD
        if nxt < CHUNKS_PER_SEQ:
            start_chunk(s, nxt)
        else:
            @pl.when(s + 1 < nseq)
            def _():
                start_chunk(s + 1, nxt - CHUNKS_PER_SEQ)
        wait_chunk(c)
        slot = c % DEC_SLOTS
        kvb_ref[...] = kv_buf[slot, :CHUNK_KEYS, :].astype(BF16)
        lat = _dot_nt(q_lat, kvb_ref[...])
        for g in range(CHUNK_KEYS // GROUP_KEYS):
            keys = slice(g * GROUP_KEYS, (g + 1) * GROUP_KEYS)
            sc_ref[:, keys] = lat[:, keys] + _dot(q_pe, kv_buf[slot, rope_rows(g), :].astype(BF16))

    def attend(carry, kvb_ref, sc_ref):
        return _softmax_update(*carry, sc_ref[...], kvb_ref[...])

    bufs = ((kvb_a, sc_a), (kvb_b, sc_b))
    carry = (jnp.full((rows, 1), -jnp.inf, F32), jnp.zeros((rows, 1), F32), jnp.zeros((rows, KV_LORA), F32))
    scores(0, *bufs[0])
    for c in range(1, CHUNKS_PER_SEQ):
        scores(c, *bufs[c % 2])
        carry = attend(carry, *bufs[(c - 1) % 2])
        if c == CHUNKS_PER_SEQ // 2:
            _, blk = _prompt_block(s // 2)
            _prompt_attn_block(blk, qp_ref, kt_ref, v_ref, wuv_ref, op_ref, m_ref, l_ref, acc_ref,
                               sa_ref, sb_ref, ATTN_HEADS_PER_STEP)
    carry = attend(carry, *bufs[(CHUNKS_PER_SEQ - 1) % 2])

    pad = NEW_KEYS_PAD - DEC_SEQ
    kvn = jnp.concatenate([ckvn_ref[...], jnp.zeros((pad, KV_LORA), F32)], axis=0).astype(BF16)
    kpn = jnp.concatenate([kpen_ref[...], jnp.zeros((pad, QK_ROPE), F32)], axis=0).astype(BF16)
    sc = _dot_nt(q_lat, kvn) + _dot_nt(q_pe, kpn)
    tok = lax.broadcasted_iota(jnp.int32, (rows, NEW_KEYS_PAD), 0) & (DEC_SEQ - 1)
    key = lax.broadcasted_iota(jnp.int32, (rows, NEW_KEYS_PAD), 1)
    sc = jnp.where(key <= tok, sc, -jnp.inf)
    _, l_fin, acc = _softmax_update(*carry, sc, kvn)
    o_ref[0] = (acc * (1.0 / l_fin)).reshape(N_HEADS, DEC_SEQ, KV_LORA)


def _attention(page_table, qs, ckv_s, kpe_s, cache_ckv, cache_kpet, qp, kt, vb, wuv):
    seq_per_blk = Q_BLOCK // DEC_SEQ
    hps = ATTN_HEADS_PER_STEP
    halves = N_HEADS // hps
    cps = SEQ // KV_CHUNK
    rows_p = hps * Q_BLOCK

    def qblock(s):
        b, i = _prompt_block(s // halves)
        return b * BLOCKS_PER_SEQ + i

    grid_spec = pltpu.PrefetchScalarGridSpec(
        num_scalar_prefetch=1,
        grid=(DEC_BATCH,),
        in_specs=[
            pl.BlockSpec((1, N_HEADS, DEC_SEQ, D_QK), lambda s, pt: (s // seq_per_blk, 0, s % seq_per_blk, 0)),
            pl.BlockSpec((DEC_SEQ, KV_LORA), lambda s, pt: (s, 0)),
            pl.BlockSpec((DEC_SEQ, QK_ROPE), lambda s, pt: (s, 0)),
            pl.BlockSpec(memory_space=pl.ANY),
            pl.BlockSpec(memory_space=pl.ANY),
            pl.BlockSpec((1, hps, Q_BLOCK, D_QK), lambda s, pt: (qblock(s), s % halves, 0, 0)),
            pl.BlockSpec((cps, D_QK, KV_CHUNK), lambda s, pt: (_prompt_block(s // halves)[0], 0, 0)),
            pl.BlockSpec((SEQ, KV_LORA), lambda s, pt: (_prompt_block(s // halves)[0], 0)),
            pl.BlockSpec((hps, KV_LORA, V_HEAD), lambda s, pt: (s % halves, 0, 0)),
        ],
        out_specs=[
            pl.BlockSpec((1, N_HEADS, DEC_SEQ, KV_LORA), lambda s, pt: (s // seq_per_blk, 0, s % seq_per_blk, 0)),
            pl.BlockSpec((Q_BLOCK, hps * V_HEAD), lambda s, pt: (qblock(s), s % halves)),
        ],
        scratch_shapes=[
            pltpu.VMEM((DEC_SLOTS, CHUNK_KEYS + ROPE_ROWS, KV_LORA), F32),
            pltpu.VMEM((CHUNK_KEYS, KV_LORA), BF16),
            pltpu.VMEM((CHUNK_KEYS, KV_LORA), BF16),
            pltpu.VMEM((N_HEADS * DEC_SEQ, CHUNK_KEYS), F32),
            pltpu.VMEM((N_HEADS * DEC_SEQ, CHUNK_KEYS), F32),
            pltpu.SemaphoreType.DMA((DEC_SLOTS,)),
            pltpu.VMEM((rows_p, 1), F32),
            pltpu.VMEM((rows_p, 1), F32),
            pltpu.VMEM((rows_p, KV_LORA), F32),
            pltpu.VMEM((rows_p, KV_CHUNK), F32),
            pltpu.VMEM((rows_p, KV_CHUNK), F32),
        ],
    )
    return pl.pallas_call(
        _attn_kernel,
        grid_spec=grid_spec,
        out_shape=[
            jax.ShapeDtypeStruct((N_SAMPLE // Q_BLOCK, N_HEADS, Q_BLOCK, KV_LORA), F32),
            jax.ShapeDtypeStruct((N_PROMPT, D_MLA_OUT), BF16),
        ],
        compiler_params=_cparams(("arbitrary",)),
        name="attention",
    )(page_table.reshape(-1), qs, ckv_s, kpe_s, cache_ckv, cache_kpet, qp, kt, vb, wuv)


def _uv_proj_kernel(ol_ref, wuv_ref, o_ref, *, nsub):
    for j in range(nsub):
        for h in range(N_HEADS):
            o = _dot(ol_ref[j, h].astype(BF16), wuv_ref[h])
            o_ref[j * Q_BLOCK:(j + 1) * Q_BLOCK, h * V_HEAD:(h + 1) * V_HEAD] = o.astype(o_ref.dtype)


def _uv_proj_sample(o_lat, wuv, nsub=4):
    nb = o_lat.shape[0]
    rows = nsub * Q_BLOCK
    return pl.pallas_call(
        functools.partial(_uv_proj_kernel, nsub=nsub),
        grid=(nb // nsub,),
        in_specs=[
            pl.BlockSpec((nsub, N_HEADS, Q_BLOCK, KV_LORA), lambda i: (i, 0, 0, 0)),
            _resident((N_HEADS, KV_LORA, V_HEAD)),
        ],
        out_specs=pl.BlockSpec((rows, D_MLA_OUT), lambda i: (i, 0)),
        out_shape=jax.ShapeDtypeStruct((nb * Q_BLOCK, D_MLA_OUT), BF16),
        compiler_params=_cparams(("arbitrary",)),
        name="uv_proj_sample",
    )(o_lat, wuv)


def _mm_ln_kernel(*refs, lhs_split, split_resid, split_out, tm):
    is_prompt = pl.program_id(0) < N_PROMPT // tm
    refs = list(refs)

    def take(split):
        if split:
            p_ref, s_ref = refs.pop(0), refs.pop(0)
            return lambda rows: jnp.where(is_prompt, p_ref[rows, :], s_ref[rows, :])
        ref = refs.pop(0)
        return lambda rows: ref[rows, :]

    lhs = [take(split) for split in lhs_split]
    w_ref = refs.pop(0)
    resid = take(split_resid)
    g_ref, b_ref = refs.pop(0), refs.pop(0)
    out_refs = refs

    sub = min(tm, LN_SUB_ROWS)
    for r in range(0, tm, sub):
        rows = slice(r, r + sub)
        acc = None
        off = 0
        for piece in lhs:
            a = piece(rows)
            part = _dot(a, w_ref[off:off + a.shape[1], :])
            acc = part if acc is None else acc + part
            off += a.shape[1]
        y = _layer_norm(DEEPNORM_ALPHA * resid(rows) + acc, g_ref[...], b_ref[...])
        if split_out:
            @pl.when(is_prompt)
            def _():
                out_refs[0][rows, :] = y

            @pl.when(jnp.logical_not(is_prompt))
            def _():
                out_refs[1][rows, :] = y
        else:
            out_refs[0][rows, :] = y
            out_refs[1][rows, :] = y.astype(BF16)


def _mm_ln(lhs_list, w, resid, g, b, *, tm, split_out=False, layer=None):
    operands, in_specs = [], []

    def add_rows(v):
        if isinstance(v, tuple):
            operands.extend(v)
            in_specs.extend([_prompt_rows(tm, v[0].shape[1]), _sample_rows(tm, v[0].shape[1])])
        else:
            operands.append(v)
            in_specs.append(pl.BlockSpec((tm, v.shape[1]), lambda i: (i, 0)))
        return isinstance(v, tuple)

    lhs_split = tuple(add_rows(v) for v in lhs_list)
    operands.append(w)
    if layer is None:
        in_specs.append(_resident(w.shape))
    else:
        in_specs.append(pl.BlockSpec((None,) + w.shape[1:], lambda i: (layer, 0, 0), pipeline_mode=pl.Buffered(1)))
    split_resid = add_rows(resid)
    operands.extend([g, b])
    in_specs.extend([_resident((1, D_MODEL)), _resident((1, D_MODEL))])
    if split_out:
        out_specs = [_prompt_rows(tm, D_MODEL), _sample_rows(tm, D_MODEL)]
        out_shape = [jax.ShapeDtypeStruct((N_PROMPT, D_MODEL), F32), jax.ShapeDtypeStruct((N_SAMPLE, D_MODEL), F32)]
    else:
        row = pl.BlockSpec((tm, D_MODEL), lambda i: (i, 0))
        out_specs = [row, row]
        out_shape = [jax.ShapeDtypeStruct((N_TOK, D_MODEL), F32), jax.ShapeDtypeStruct((N_TOK, D_MODEL), BF16)]
    return pl.pallas_call(
        functools.partial(_mm_ln_kernel, lhs_split=lhs_split, split_resid=split_resid, split_out=split_out, tm=tm),
        grid=(N_TOK // tm,),
        in_specs=in_specs,
        out_specs=out_specs,
        out_shape=out_shape,
        compiler_params=_cparams(("arbitrary",)),
        name="mm_ln",
    )(*operands)


def _ffn_up_kernel(x_ref, wg_ref, wu_ref, h_ref, wgb_ref, wub_ref):
    @pl.when(pl.program_id(1) == 0)
    def _():
        wgb_ref[...] = wg_ref[...].astype(BF16)
        wub_ref[...] = wu_ref[...].astype(BF16)

    for r in range(0, x_ref.shape[0], FFN_SUB_ROWS):
        x = x_ref[r:r + FFN_SUB_ROWS, :]
        h = jax.nn.silu(_dot(x, wgb_ref[...])) * _dot(x, wub_ref[...])
        h_ref[r:r + FFN_SUB_ROWS, :] = h.astype(h_ref.dtype)


def _ffn_up(xb, wg, wu, layer, tm=N_TOK // 4, tn=512):
    wblk = pl.BlockSpec((None, D_MODEL, tn), lambda j, i: (layer, 0, j))
    return pl.pallas_call(
        _ffn_up_kernel,
        grid=(D_FF // tn, N_TOK // tm),
        in_specs=[pl.BlockSpec((tm, D_MODEL), lambda j, i: (i, 0)), wblk, wblk],
        out_specs=pl.BlockSpec((tm, tn), lambda j, i: (i, j)),
        out_shape=jax.ShapeDtypeStruct((N_TOK, D_FF), BF16),
        scratch_shapes=[pltpu.VMEM((D_MODEL, tn), BF16), pltpu.VMEM((D_MODEL, tn), BF16)],
        compiler_params=_cparams(("arbitrary", "arbitrary")),
        name="ffn_up",
    )(xb, wg, wu)


def _conv_taps(read, cw_ref):
    conv = read(CONV_W - 1) * cw_ref[0:1, :]
    for j in range(1, CONV_W):
        conv = conv + read(CONV_W - 1 - j) * cw_ref[j:j + 1, :]
    return conv


def _in_odd_cast_weights(wv_ref, wb_ref, wc_ref, wvb_ref, wbb_ref, wcb_ref):
    @pl.when(pl.program_id(1) == 0)
    def _():
        wvb_ref[...] = wv_ref[...].astype(BF16)
        wbb_ref[...] = wb_ref[...].astype(BF16)
        wcb_ref[...] = wc_ref[...].astype(BF16)


def _in_odd_prompt_kernel(x_ref, wv_ref, wb_ref, wc_ref, cw_ref, g_ref, ztail_ref,
                          wvb_ref, wbb_ref, wcb_ref, zp_ref, *, tm):
    i = pl.program_id(1)
    tiles_per_seq = SEQ // tm
    _in_odd_cast_weights(wv_ref, wb_ref, wc_ref, wvb_ref, wbb_ref, wcb_ref)

    @pl.when(i % tiles_per_seq == 0)
    def _():
        zp_ref[0:CONV_HIST, :] = jnp.zeros((CONV_HIST, zp_ref.shape[1]), F32)

    for r in range(0, tm, FFN_SUB_ROWS):
        x = x_ref[r:r + FFN_SUB_ROWS, :]
        new = slice(CONV_HIST + r, CONV_HIST + r + FFN_SUB_ROWS)
        zp_ref[new, :] = _dot(x, wcb_ref[...]) * _dot(x, wvb_ref[...])
        conv = _conv_taps(lambda back: zp_ref[new.start - back:new.stop - back, :], cw_ref)
        g_ref[r:r + FFN_SUB_ROWS, :] = (_dot(x, wbb_ref[...]) * conv).astype(g_ref.dtype)
    zp_ref[0:CONV_HIST, :] = zp_ref[tm:tm + CONV_HIST, :]

    @pl.when(i % tiles_per_seq == tiles_per_seq - 1)
    def _():
        ztail_ref[...] = zp_ref[tm:tm + CONV_HIST, :]


def _in_odd_sample_kernel(x_ref, wv_ref, wb_ref, wc_ref, zs_ref, gbs_ref, wvb_ref, wbb_ref, wcb_ref):
    _in_odd_cast_weights(wv_ref, wb_ref, wc_ref, wvb_ref, wbb_ref, wcb_ref)
    x = x_ref[...]
    zs_ref[...] = _dot(x, wcb_ref[...]) * _dot(x, wvb_ref[...])
    gbs_ref[...] = _dot(x, wbb_ref[...])


def _in_odd(xb, w_in, conv_w, tm=1024, tn=512):
    nb = D_MODEL // tn
    nbp = N_PROMPT // tm
    tiles_per_seq = SEQ // tm
    wblks = [pl.BlockSpec((D_MODEL, tn), lambda j, i, part=part: (0, part * nb + j)) for part in range(3)]
    wscratch = [pltpu.VMEM((D_MODEL, tn), BF16)] * 3
    g_prompt, z_tail = pl.pallas_call(
        functools.partial(_in_odd_prompt_kernel, tm=tm),
        grid=(nb, nbp),
        in_specs=[pl.BlockSpec((tm, D_MODEL), lambda j, i: (i, 0))] + wblks
        + [pl.BlockSpec((CONV_W, tn), lambda j, i: (0, j))],
        out_specs=[
            pl.BlockSpec((tm, tn), lambda j, i: (i, j)),
            pl.BlockSpec((CONV_HIST, tn), lambda j, i: (i // tiles_per_seq, j)),
        ],
        out_shape=[
            jax.ShapeDtypeStruct((N_PROMPT, D_MODEL), BF16),
            jax.ShapeDtypeStruct((BATCH * CONV_HIST, D_MODEL), F32),
        ],
        scratch_shapes=wscratch + [pltpu.VMEM((tm + CONV_HIST, tn), F32)],
        compiler_params=_cparams(("arbitrary", "arbitrary")),
        name="in_odd_prompt",
    )(xb, w_in, w_in, w_in, conv_w)
    sample_blk = pl.BlockSpec((tm, tn), lambda j, i: (i, j))
    z_s, gb_s = pl.pallas_call(
        _in_odd_sample_kernel,
        grid=(nb, N_SAMPLE // tm),
        in_specs=[pl.BlockSpec((tm, D_MODEL), lambda j, i: (nbp + i, 0))] + wblks,
        out_specs=[sample_blk, sample_blk],
        out_shape=[jax.ShapeDtypeStruct((N_SAMPLE, D_MODEL), F32)] * 2,
        scratch_shapes=wscratch,
        compiler_params=_cparams(("arbitrary", "arbitrary")),
        name="in_odd_sample",
    )(xb, w_in, w_in, w_in)
    return g_prompt, z_tail, z_s, gb_s


def _conv_sample_kernel(z_ref, hist_ref, gb_ref, cw_ref, g_ref, zp_ref, *, S):
    zp_ref[:, 0:CONV_HIST, :] = hist_ref[...]
    zp_ref[:, CONV_HIST:, :] = z_ref[...]
    conv = _conv_taps(lambda back: zp_ref[:, CONV_HIST - back:CONV_HIST - back + DEC_SEQ, :], cw_ref)
    g_ref[...] = (gb_ref[...] * conv).reshape(S * DEC_SEQ, D_MODEL).astype(g_ref.dtype)


def _conv_sample(z3, hist, gb3, conv_w, S=32):
    blk = lambda i: (i, 0, 0)
    rows = S * DEC_SEQ
    return pl.pallas_call(
        functools.partial(_conv_sample_kernel, S=S),
        grid=(DEC_BATCH // S,),
        in_specs=[
            pl.BlockSpec((S, DEC_SEQ, D_MODEL), blk),
            pl.BlockSpec((S, CONV_HIST, D_MODEL), blk),
            pl.BlockSpec((S, DEC_SEQ, D_MODEL), blk),
            _resident((CONV_W, D_MODEL)),
        ],
        out_specs=pl.BlockSpec((rows, D_MODEL), lambda i: (i, 0)),
        out_shape=jax.ShapeDtypeStruct((N_SAMPLE, D_MODEL), BF16),
        scratch_shapes=[pltpu.VMEM((S, CONV_HIST + DEC_SEQ, D_MODEL), F32)],
        compiler_params=_cparams(("arbitrary",)),
        name="conv_sample",
    )(z3, hist, gb3, conv_w)


def _rope_tables():
    pos = jnp.concatenate([jnp.arange(SEQ), PAST_LEN + jnp.arange(DEC_SEQ)])
    inv = ROPE_THETA ** (-jnp.arange(0, QK_ROPE, 2, dtype=F32) / QK_ROPE)
    ang = pos.astype(F32)[:, None] * inv[None, :]
    cos, sin = jnp.cos(ang), jnp.sin(ang)
    cos2 = jnp.concatenate([cos, cos], axis=1)
    sin2 = jnp.concatenate([-sin, sin], axis=1)

    def per_token(t):
        return jnp.concatenate([jnp.tile(t[:SEQ], (BATCH, 1)), jnp.tile(t[SEQ:], (DEC_BATCH, 1))], axis=0)

    return (per_token(jnp.tile(cos2, (1, 2))), per_token(jnp.tile(sin2, (1, 2))),
            per_token(jnp.concatenate([cos2, sin2], axis=1)))


def _swap_halves(w):
    half = QK_ROPE // 2
    return jnp.concatenate([w[..., half:], w[..., :half]], axis=-1)


def kernel(x_prompt, x_sample, cache_ckv, cache_kpe, state_pool, state_conv, page_table, w_in_even, pool_w, pool_scale, q_norm_g, w_q_b, kv_norm_g, w_uk, w_uv, w_out_even, w_in_odd, conv_w, w_out_odd, ffn_w_gate, ffn_w_up, ffn_w_down, ln1_g, ln1_b, ln2_g, ln2_b):
    assert DEPTH == 2
    n_pool_pages = cache_ckv.shape[1]
    cos, sin, cos_sin = _rope_tables()
    xp = x_prompt.reshape(N_PROMPT, D_MODEL)
    xs = x_sample.reshape(N_SAMPLE, D_MODEL)

    def row(v):
        return v.reshape(1, -1)

    w_down = ffn_w_down.astype(BF16)

    def ffn(layer, xf, xb, last):
        hmid = _ffn_up(xb, ffn_w_gate, ffn_w_up, layer)
        return _mm_ln([hmid], w_down, xf, row(ln2_g[layer]), row(ln2_b[layer]),
                      tm=256, split_out=last, layer=layer)

    e = 0
    w_in = w_in_even[e]
    w_all = jnp.concatenate([w_in, _swap_halves(w_in[:, D_POOL + Q_LORA + KV_LORA:])], axis=1).astype(BF16)
    u, cq, ckv_p, ckv_s, kpe_p, kpe_s, vb, kt = _in_even(
        xp, xs, w_all, row(q_norm_g[e]), row(kv_norm_g[e]), cos_sin)

    wq = w_q_b[e]
    wq_rope = wq[:, :, QK_NOPE:]
    qp, qs = _q_proj(
        cq,
        wq[:, :, :QK_NOPE].reshape(Q_LORA, N_HEADS * QK_NOPE).astype(BF16),
        wq_rope.reshape(Q_LORA, N_HEADS * QK_ROPE).astype(BF16),
        _swap_halves(wq_rope).reshape(Q_LORA, N_HEADS * QK_ROPE).astype(BF16),
        jnp.transpose(w_uk[e], (1, 2, 0)).astype(BF16), cos, sin)

    pw = pool_w[e].astype(BF16)
    ps = row(pool_scale[e])
    u_sample = u[N_PROMPT:].reshape(DEC_BATCH, DEC_SEQ, D_POOL)
    pool_hist = jnp.pad(state_pool[e], ((0, 0), (POOL_HIST - POOL_BUF, 0), (0, 0)))
    y_pool = (_pool_prompt(u, pw, ps), _pool_sample(u_sample, pool_hist, pw, ps))

    wuv = jnp.transpose(w_uv[e], (1, 0, 2)).astype(BF16)
    o_lat_s, o_prompt = _attention(page_table, qs, ckv_s, kpe_s,
                                   cache_ckv[e].reshape(n_pool_pages, PAGE_SIZE, KV_LORA),
                                   jnp.swapaxes(cache_kpe[e], 1, 2), qp, kt, vb, wuv)
    o_sample = _uv_proj_sample(o_lat_s, wuv)
    xf, xb = _mm_ln([y_pool, (o_prompt, o_sample)], w_out_even[e].astype(BF16), (xp, xs),
                    row(ln1_g[0]), row(ln1_b[0]), tm=512)
    xf, xb = ffn(0, xf, xb, False)

    od = 0
    g_prompt, z_tail, z_s, gb_s = _in_odd(xb, w_in_odd[od], conv_w[od])
    z_sample = z_s.reshape(DEC_BATCH, DEC_SEQ, D_MODEL)
    gb_sample = gb_s.reshape(DEC_BATCH, DEC_SEQ, D_MODEL)
    conv_hist = jnp.pad(state_conv[od], ((0, 0), (CONV_HIST - CONV_BUF, 0), (0, 0)))
    g = (g_prompt, _conv_sample(z_sample, conv_hist, gb_sample, conv_w[od]))
    xf, xb = _mm_ln([g], w_out_odd[od].astype(BF16), xf, row(ln1_g[1]), row(ln1_b[1]), tm=512)
    y_p, y_s = ffn(1, xf, xb, True)

    pool_prompt = jnp.stack([u[(b + 1) * SEQ - POOL_BUF:(b + 1) * SEQ] for b in range(BATCH)])
    pool_sample = jnp.concatenate([state_pool[e][:, DEC_SEQ:], u_sample], axis=1)
    conv_prompt = z_tail.reshape(BATCH, CONV_HIST, D_MODEL)[:, CONV_HIST - CONV_BUF:]
    conv_sample = z_sample[:, DEC_SEQ - CONV_BUF:]
    return (y_p.reshape(BATCH, SEQ, D_MODEL), y_s.reshape(DEC_BATCH, DEC_SEQ, D_MODEL),
            ckv_p.reshape(1, BATCH, SEQ, KV_LORA), kpe_p.reshape(1, BATCH, SEQ, QK_ROPE),
            pool_prompt[None], conv_prompt[None],
            ckv_s.reshape(1, DEC_BATCH, DEC_SEQ, KV_LORA), kpe_s.reshape(1, DEC_BATCH, DEC_SEQ, QK_ROPE),
            pool_sample[None], conv_sample[None])
```
